```python
import jax, jax.numpy as jnp
from jax import lax
import numpy as np

D_MODEL = 2048
BATCH = 2
SEQ = 4096
DEPTH = 2
DEC_BATCH = 128
DEC_SEQ = 8
PAST_LEN = 8192
PAGE_SIZE = 128

N_A = DEPTH // 2
N_B = DEPTH - N_A
RW_HEAD = 64
RW_HEADS = D_MODEL // RW_HEAD
LORA_W = max(32, int(round(1.8 * D_MODEL ** 0.5 / 32)) * 32)
LORA_A = max(32, int(round(1.8 * D_MODEL ** 0.5 / 32)) * 32)
LORA_G = max(32, int(round(0.6 * D_MODEL ** 0.8 / 32)) * 32)
RW_LN_EPS = 64e-5
ATT_HEAD_DIM = 64
ATT_HEADS = D_MODEL // ATT_HEAD_DIM
KV_HEADS = ATT_HEADS // 8
GQA = ATT_HEADS // KV_HEADS
WINDOW = 128
BLOCK = 128
N_EXPERTS = 32
N_GROUPS = 8
EXPERTS_PER_GROUP = N_EXPERTS // N_GROUPS
TOP_K = 2
GROUP_SCORE_TOP = 2
D_EXPERT = 512
ALPHA = (2.0 * DEPTH) ** 0.25
BETA = (8.0 * DEPTH) ** -0.25
LN_EPS = 1e-5
NEG_INF = -1e30

kernel_name = 'yoco_rwkv7_swa_sink_alibi_moe_adaln_step'


def _layer_norm(x, g, b, eps=LN_EPS):
    xf = x.astype(jnp.float32)
    mu = xf.mean(-1, keepdims=True)
    var = jnp.square(xf - mu).mean(-1, keepdims=True)
    return ((xf - mu) * lax.rsqrt(var + eps) * g.astype(jnp.float32) + b.astype(jnp.float32)).astype(x.dtype)


def _ada(c, w, b, n):
    mod = jax.nn.silu(c) @ w + b
    return tuple(t[:, None] for t in jnp.split(mod, n, axis=-1))


def _alibi_slopes():
    return 2.0 ** (-8.0 * jnp.arange(1, ATT_HEADS + 1, dtype=jnp.float32) / ATT_HEADS)


def _rwkv7_time_mix(h, shift_prev, S0, mu, w_rkv, w0, w1, w2, a0, a1, a2, g1, g2,
                    k_k, k_a, r_k, lnx_w, lnx_b, wo):
    B, T, D = h.shape
    h_prev = jnp.concatenate([shift_prev[:, None].astype(h.dtype), h[:, :-1]], axis=1)
    xx = h_prev - h
    xr, xw, xk, xv, xa, xg = (h + xx * mu[i] for i in range(6))
    r = xr @ w_rkv[0]
    k = xk @ w_rkv[1]
    v = xv @ w_rkv[2]
    w = -jax.nn.softplus(-(w0 + jnp.tanh(xw @ w1) @ w2).astype(jnp.float32)) - 0.5
    decay = jnp.exp(-jnp.exp(w))
    a = jax.nn.sigmoid((a0 + (xa @ a1) @ a2).astype(jnp.float32))
    g = jax.nn.sigmoid(xg @ g1) @ g2

    def heads(t):
        return t.astype(jnp.float32).reshape(B, T, RW_HEADS, RW_HEAD)

    r, k, v, decay, a = heads(r), heads(k), heads(v), heads(decay), heads(a)
    kk = k * k_k.astype(jnp.float32).reshape(RW_HEADS, RW_HEAD)
    kk = kk * lax.rsqrt(jnp.sum(kk * kk, -1, keepdims=True) + 1e-12)
    k = k * (1.0 + (a - 1.0) * k_a.astype(jnp.float32).reshape(RW_HEADS, RW_HEAD))

    def step(S, inp):
        r_t, d_t, k_t, v_t, kk_t, a_t = inp
        s_kk = jnp.einsum('bhij,bhj->bhi', S, kk_t)
        S = (S * d_t[:, :, None, :] - s_kk[..., None] * (kk_t * a_t)[:, :, None, :]
             + v_t[..., None] * k_t[:, :, None, :])
        return S, jnp.einsum('bhij,bhj->bhi', S, r_t)

    xs = tuple(jnp.swapaxes(t, 0, 1) for t in (r, decay, k, v, kk, a))
    S_T, ys = lax.scan(step, S0.astype(jnp.float32), xs)
    y = jnp.swapaxes(ys, 0, 1)
    ym = y.mean(-1, keepdims=True)
    yv = jnp.square(y - ym).mean(-1, keepdims=True)
    yn = ((y - ym) * lax.rsqrt(yv + RW_LN_EPS)).reshape(B, T, D)
    yn = yn * lnx_w.astype(jnp.float32) + lnx_b.astype(jnp.float32)
    bonus = (jnp.sum(r * k * r_k.astype(jnp.float32), -1, keepdims=True) * v).reshape(B, T, D)
    out = ((yn + bonus) * g.astype(jnp.float32)).astype(h.dtype) @ wo
    return out, S_T.astype(S0.dtype), h[:, -1]


def _sink_attention(q, k, v, diff, valid, sinks):
    Bq, Nb, Q = q.shape[:3]
    qg = q.reshape(Bq, Nb, Q, KV_HEADS, GQA, ATT_HEAD_DIM).astype(jnp.float32)
    s = jnp.einsum('bnqkgd,bnskd->bnkgqs', qg, k.astype(jnp.float32)) * (ATT_HEAD_DIM ** -0.5)
    slopes = _alibi_slopes().reshape(KV_HEADS, GQA, 1, 1)
    s = s - slopes * diff[:, None, None].astype(jnp.float32)
    s = jnp.where(valid[:, None, None], s, NEG_INF)
    sk = sinks.astype(jnp.float32).reshape(KV_HEADS, GQA, 1, 1)
    m = jnp.maximum(s.max(-1, keepdims=True), sk)
    p = jnp.exp(s - m)
    den = p.sum(-1, keepdims=True) + jnp.exp(sk - m)
    o = jnp.einsum('bnkgqs,bnskd->bnqkgd', p / den, v.astype(jnp.float32))
    return o.reshape(Bq, Nb, Q, ATT_HEADS * ATT_HEAD_DIM)


def _swa_prompt(q, k, v, sinks):
    B, T = q.shape[:2]
    nb = T // BLOCK
    qb = q.reshape(B, nb, BLOCK, ATT_HEADS, ATT_HEAD_DIM)
    kb = k.reshape(B, nb, BLOCK, KV_HEADS, ATT_HEAD_DIM)
    vb = v.reshape(B, nb, BLOCK, KV_HEADS, ATT_HEAD_DIM)
    pad = ((0, 0), (1, 0), (0, 0), (0, 0), (0, 0))
    k2 = jnp.concatenate([jnp.pad(kb, pad)[:, :-1], kb], axis=2)
    v2 = jnp.concatenate([jnp.pad(vb, pad)[:, :-1], vb], axis=2)
    i = jnp.arange(BLOCK)[:, None]
    j = jnp.arange(2 * BLOCK)[None]
    diff = (BLOCK + i - j)[None]
    kpos = jnp.arange(nb)[:, None, None] * BLOCK - BLOCK + j[None]
    valid = (diff >= 0) & (diff < WINDOW) & (kpos >= 0)
    o = _sink_attention(qb, k2, v2, diff, valid, sinks)
    return o.reshape(B, T, ATT_HEADS * ATT_HEAD_DIM)


def _swa_sample(q, k_all, v_all, wb, sinks):
    Tq = q.shape[1]
    diff = wb + jnp.arange(Tq)[:, None] - jnp.arange(wb + Tq)[None]
    valid = (diff >= 0) & (diff < WINDOW)
    o = _sink_attention(q[:, None], k_all[:, None], v_all[:, None], diff[None], valid[None], sinks)
    return o[:, 0]


def _moe(h, router_w, router_bias, w_gate, w_up, w_down):
    s = jax.nn.sigmoid(jnp.einsum('btd,de->bte', h, router_w).astype(jnp.float32))
    sb = s + router_bias.astype(jnp.float32)
    grp = sb.reshape(*sb.shape[:-1], N_GROUPS, EXPERTS_PER_GROUP)
    gscore = lax.top_k(grp, GROUP_SCORE_TOP)[0].sum(-1)
    gsel = jnp.argmax(gscore, axis=-1)
    in_group = (jnp.arange(N_EXPERTS) // EXPERTS_PER_GROUP) == gsel[..., None]
    _, idx = lax.top_k(jnp.where(in_group, sb, NEG_INF), TOP_K)
    wsel = jnp.take_along_axis(s, idx, axis=-1)
    wsel = wsel / wsel.sum(-1, keepdims=True)
    gates = jnp.sum(jax.nn.one_hot(idx, N_EXPERTS, dtype=jnp.float32) * wsel[..., None], axis=-2)
    hg = jnp.einsum('btd,edf->btef', h, w_gate)
    hu = jnp.einsum('btd,edf->btef', h, w_up)
    act = jax.nn.silu(hg) * hu * gates[..., None].astype(h.dtype)
    return jnp.einsum('btef,efd->btd', act, w_down)


def _trunk(x, c, S0, shift0, k_buf, v_buf, p, prompt):
    B, T, _ = x.shape
    states, shifts = [], []
    k_sh = v_sh = k_win = v_win = None
    wb = None
    for l in range(DEPTH):
        if l == N_A:
            kv_shift, kv_scale = _ada(c, p['ada_kv_w'], p['ada_kv_b'], 2)
            kv = (x * (1 + kv_scale) + kv_shift) @ p['w_kv']
            k_sh, v_sh = [t.reshape(B, T, KV_HEADS, ATT_HEAD_DIM) for t in jnp.split(kv, 2, axis=-1)]
            if prompt:
                win = min(WINDOW, T)
                k_win, v_win = k_sh[:, T - win:], v_sh[:, T - win:]
            else:
                wb = k_buf.shape[1]
                k_sh = jnp.concatenate([k_buf.astype(k_sh.dtype), k_sh], axis=1)
                v_sh = jnp.concatenate([v_buf.astype(v_sh.dtype), v_sh], axis=1)
                k_win, v_win = k_sh[:, -wb:], v_sh[:, -wb:]
        shift, scale, gate = _ada(c, p['ada_w'][l, 0], p['ada_b'][l, 0], 3)
        h = x * (1 + scale) + shift
        if l < N_A:
            out, S_l, sh_l = _rwkv7_time_mix(
                h, shift0[l], S0[l], p['rw_mu'][l], p['rw_w_rkv'][l], p['rw_w0'][l], p['rw_w1'][l],
                p['rw_w2'][l], p['rw_a0'][l], p['rw_a1'][l], p['rw_a2'][l], p['rw_g1'][l], p['rw_g2'][l],
                p['rw_k_k'][l], p['rw_k_a'][l], p['rw_r_k'][l], p['rw_lnx_w'][l], p['rw_lnx_b'][l],
                p['rw_wo'][l])
            states.append(S_l)
            shifts.append(sh_l)
        else:
            jb = l - N_A
            q = (h @ p['w_q'][jb]).reshape(B, T, ATT_HEADS, ATT_HEAD_DIM)
            if prompt:
                o = _swa_prompt(q, k_sh, v_sh, p['attn_sinks'][jb])
            else:
                o = _swa_sample(q, k_sh, v_sh, wb, p['attn_sinks'][jb])
            out = o.astype(x.dtype) @ p['w_o_attn'][jb]
        x = _layer_norm(ALPHA * x + gate * out, p['ln_g'][l, 0], p['ln_b'][l, 0])
        shift, scale, gate = _ada(c, p['ada_w'][l, 1], p['ada_b'][l, 1], 3)
        h = x * (1 + scale) + shift
        out = _moe(h, p['router_w'], p['router_bias'], p['moe_w_gate'][l], p['moe_w_up'][l], p['moe_w_down'][l])
        x = _layer_norm(ALPHA * x + gate * out, p['ln_g'][l, 1], p['ln_b'][l, 1])
    return x, jnp.stack(states), jnp.stack(shifts), k_win, v_win


def setup_inputs(seed: int = 0) -> dict:
    key = jax.random.key(seed)
    ks = iter(jax.random.split(key, 64))
    f32 = jnp.float32
    D = D_MODEL
    KVD = KV_HEADS * ATT_HEAD_DIM
    WB = min(WINDOW, PAST_LEN)

    def nrm(shape, scale):
        return jax.random.normal(next(ks), shape, f32) * scale

    def uni(shape, lo, hi):
        return jax.random.uniform(next(ks), shape, f32, lo, hi)

    return {
        'x_prompt': nrm((BATCH, SEQ, D), 1.0),
        'x_sample': nrm((DEC_BATCH, DEC_SEQ, D), 1.0),
        'c_prompt': nrm((BATCH, D), 1.0),
        'c_sample': nrm((DEC_BATCH, D), 1.0),
        'state_wkv': nrm((N_A, DEC_BATCH, RW_HEADS, RW_HEAD, RW_HEAD), 0.3),
        'state_shift': nrm((N_A, DEC_BATCH, D), 1.0),
        'cache_k_win': nrm((DEC_BATCH, WB, KV_HEADS, ATT_HEAD_DIM), 1.0),
        'cache_v_win': nrm((DEC_BATCH, WB, KV_HEADS, ATT_HEAD_DIM), BETA),
        'ada_w': nrm((DEPTH, 2, D, 3 * D), 0.5 * D ** -0.5),
        'ada_b': nrm((DEPTH, 2, 3 * D), 0.02),
        'ln_g': 1.0 + nrm((DEPTH, 2, D), 0.02),
        'ln_b': nrm((DEPTH, 2, D), 0.02),
        'rw_mu': uni((N_A, 6, D), 0.0, 1.0),
        'rw_w_rkv': nrm((N_A, 3, D, D), D ** -0.5) * jnp.array([1.0, 1.0, BETA], f32).reshape(1, 3, 1, 1),
        'rw_w0': uni((N_A, D), -6.0, 1.0),
        'rw_w1': nrm((N_A, D, LORA_W), D ** -0.5),
        'rw_w2': nrm((N_A, LORA_W, D), 0.1 * LORA_W ** -0.5),
        'rw_a0': nrm((N_A, D), 0.5),
        'rw_a1': nrm((N_A, D, LORA_A), D ** -0.5),
        'rw_a2': nrm((N_A, LORA_A, D), 0.5 * LORA_A ** -0.5),
        'rw_g1': nrm((N_A, D, LORA_G), D ** -0.5),
        'rw_g2': nrm((N_A, LORA_G, D), LORA_G ** -0.5),
        'rw_k_k': 0.85 + nrm((N_A, D), 0.05),
        'rw_k_a': 1.0 + nrm((N_A, D), 0.05),
        'rw_r_k': nrm((N_A, RW_HEADS, RW_HEAD), 0.1),
        'rw_lnx_w': 1.0 + nrm((N_A, D), 0.02),
        'rw_lnx_b': nrm((N_A, D), 0.02),
        'rw_wo': nrm((N_A, D, D), BETA * D ** -0.5),
        'ada_kv_w': nrm((D, 2 * D), 0.5 * D ** -0.5),
        'ada_kv_b': nrm((2 * D,), 0.02),
        'w_kv': nrm((D, 2 * KVD), D ** -0.5) * jnp.concatenate([jnp.ones((KVD,), f32), jnp.full((KVD,), BETA, f32)]),
        'w_q': nrm((N_B, D, ATT_HEADS * ATT_HEAD_DIM), D ** -0.5),
        'attn_sinks': nrm((N_B, ATT_HEADS), 0.5),
        'w_o_attn': nrm((N_B, ATT_HEADS * ATT_HEAD_DIM, D), BETA * D ** -0.5),
        'router_w': nrm((D, N_EXPERTS), D ** -0.5),
        'router_bias': nrm((N_EXPERTS,), 0.01),
        'moe_w_gate': nrm((DEPTH, N_EXPERTS, D, D_EXPERT), D ** -0.5),
        'moe_w_up': nrm((DEPTH, N_EXPERTS, D, D_EXPERT), D ** -0.5),
        'moe_w_down': nrm((DEPTH, N_EXPERTS, D_EXPERT, D), BETA * D_EXPERT ** -0.5),
    }


def reference(x_prompt, x_sample, c_prompt, c_sample, state_wkv, state_shift, cache_k_win, cache_v_win,
              ada_w, ada_b, ln_g, ln_b, rw_mu, rw_w_rkv, rw_w0, rw_w1, rw_w2, rw_a0, rw_a1, rw_a2,
              rw_g1, rw_g2, rw_k_k, rw_k_a, rw_r_k, rw_lnx_w, rw_lnx_b, rw_wo, ada_kv_w, ada_kv_b,
              w_kv, w_q, attn_sinks, w_o_attn, router_w, router_bias, moe_w_gate, moe_w_up, moe_w_down):
    p = {
        'ada_w': ada_w, 'ada_b': ada_b, 'ln_g': ln_g, 'ln_b': ln_b,
        'rw_mu': rw_mu, 'rw_w_rkv': rw_w_rkv, 'rw_w0': rw_w0, 'rw_w1': rw_w1, 'rw_w2': rw_w2,
        'rw_a0': rw_a0, 'rw_a1': rw_a1, 'rw_a2': rw_a2, 'rw_g1': rw_g1, 'rw_g2': rw_g2,
        'rw_k_k': rw_k_k, 'rw_k_a': rw_k_a, 'rw_r_k': rw_r_k, 'rw_lnx_w': rw_lnx_w, 'rw_lnx_b': rw_lnx_b,
        'rw_wo': rw_wo, 'ada_kv_w': ada_kv_w, 'ada_kv_b': ada_kv_b, 'w_kv': w_kv, 'w_q': w_q,
        'attn_sinks': attn_sinks, 'w_o_attn': w_o_attn, 'router_w': router_w, 'router_bias': router_bias,
        'moe_w_gate': moe_w_gate, 'moe_w_up': moe_w_up, 'moe_w_down': moe_w_down,
    }
    Bp = x_prompt.shape[0]
    S0_p = jnp.zeros((N_A, Bp, RW_HEADS, RW_HEAD, RW_HEAD), state_wkv.dtype)
    shift0_p = jnp.zeros((N_A, Bp, D_MODEL), x_prompt.dtype)
    y_prompt, p_wkv, p_shift, p_k_win, p_v_win = _trunk(
        x_prompt, c_prompt, S0_p, shift0_p, None, None, p, True)
    y_sample, s_wkv, s_shift, s_k_win, s_v_win = _trunk(
        x_sample, c_sample, state_wkv, state_shift, cache_k_win, cache_v_win, p, False)
    return (y_prompt, y_sample, p_wkv, p_shift, p_k_win, p_v_win, s_wkv, s_shift, s_k_win, s_v_win)
```

```python
import functools

import jax
import jax.numpy as jnp
import numpy as np
from jax import lax
from jax.experimental import pallas as pl
from jax.experimental.pallas import tpu as pltpu

F32 = jnp.float32
BF16 = jnp.bfloat16

RW_HEAD = 64
ATT_HEAD_DIM = 64
GQA = 8
WINDOW = 128
N_EXPERTS = 32
EXPERTS_PER_GROUP = 4
N_GROUPS = N_EXPERTS // EXPERTS_PER_GROUP
DEPTH = 2
ALPHA = (2.0 * DEPTH) ** 0.25
LN_EPS = 1e-5
RW_LN_EPS = 64e-5
NEG_INF = -1e30

SUBLANES = 8
LANE_GROUP = 256
HEADS_PER_GROUP = LANE_GROUP // RW_HEAD
VMEM_LIMIT = 56 * 1024 * 1024

MM_TM = 1024
MM_TN = 512
EW_ROWS = 256
SCAN_TT = 64
SCAN_NB = 4
ATT_NB = 8
MOE_TG = 256


def _params(sem):
    return pltpu.CompilerParams(dimension_semantics=sem, vmem_limit_bytes=VMEM_LIMIT)


def _softplus(z):
    return jnp.maximum(z, 0.0) + jnp.log1p(jnp.exp(-jnp.abs(z)))


def _act(x, kind):
    if kind is None:
        return x
    if kind == "silu":
        return x * jax.nn.sigmoid(x)
    if kind == "tanh":
        return jnp.tanh(x)
    if kind == "sigmoid":
        return jax.nn.sigmoid(x)
    if kind == "decay":
        w = -_softplus(-x) - 0.5
        return jnp.exp(-jnp.exp(w))
    raise ValueError(kind)


def _mm_body(*refs, in_act, out_act, has_bias):
    x_ref, w_ref = refs[0], refs[1]
    b_ref = refs[2] if has_bias else None
    o_ref = refs[2 + has_bias]
    xs_ref = refs[3 + has_bias]

    @pl.when(pl.program_id(1) == 0)
    def _():
        xs_ref[...] = _act(x_ref[...].astype(F32), in_act).astype(BF16)

    acc = jnp.dot(xs_ref[...], w_ref[...].astype(BF16), preferred_element_type=F32)
    if has_bias:
        acc = acc + b_ref[...]
    o_ref[...] = _act(acc, out_act).astype(o_ref.dtype)


def _mm(x, w, *, x_lead=(), w_lead=(), bias=None, in_act=None, out_act=None, out_dtype=F32,
        tm=MM_TM, tn=MM_TN):
    m, k = x.shape[-2:]
    n = w.shape[-1]
    tm = min(tm, m)
    tn = min(tn, n)
    assert m % tm == 0 and n % tn == 0, (m, tm, n, tn)
    nx, nw = len(x_lead), len(w_lead)
    in_specs = [
        pl.BlockSpec((None,) * nx + (tm, k), lambda i, j: tuple(x_lead) + (i, 0)),
        pl.BlockSpec((None,) * nw + (k, tn), lambda i, j: tuple(w_lead) + (0, j)),
    ]
    args = [x, w]
    if bias is not None:
        in_specs.append(pl.BlockSpec((1, tn), lambda i, j: (0, j)))
        args.append(bias.reshape(1, n).astype(F32))
    body = functools.partial(_mm_body, in_act=in_act, out_act=out_act, has_bias=bias is not None)
    return pl.pallas_call(
        body,
        grid=(m // tm, n // tn),
        in_specs=in_specs,
        out_specs=pl.BlockSpec((tm, tn), lambda i, j: (i, j)),
        out_shape=jax.ShapeDtypeStruct((m, n), out_dtype),
        scratch_shapes=[pltpu.VMEM((tm, k), BF16)],
        compiler_params=_params(("parallel", "arbitrary")),
    )(*args)


def _geo(b, t):
    if t == SUBLANES:
        g = min(EW_ROWS // SUBLANES, b)
        return dict(groups=g, steps=b // g, mod_groups=g, steps_per_batch=1, mod_index=lambda i: i)
    assert t % EW_ROWS == 0
    spb = t // EW_ROWS
    return dict(groups=EW_ROWS // SUBLANES, steps=b * spb, mod_groups=1, steps_per_batch=spb,
                mod_index=lambda i: i // spb)


def _premix_body(x_ref, xprev_ref, first_ref, shift_ref, scale_ref, mu_ref, xmix_ref, hlast_ref, *,
                 steps_per_batch):
    i = pl.program_id(0)
    g, s, d = x_ref.shape
    gf = first_ref.shape[0]
    scale = scale_ref[...]
    shift = shift_ref[...]
    h3 = x_ref[...] * (1.0 + scale) + shift
    hlast_ref[...] = h3[g - gf:, s - 1:, :]
    h = h3.reshape(g * s, d)
    if steps_per_batch > 1:
        hprev_row = xprev_ref[:, s - 1:, :] * (1.0 + scale) + shift
        first = jnp.where(i % steps_per_batch == 0, first_ref[...], hprev_row)
        period = g * s
    else:
        first = first_ref[...]
        period = s
    first2 = jnp.broadcast_to(first, (g, s, d)).reshape(g * s, d)
    row = lax.broadcasted_iota(jnp.int32, (g * s, 1), 0)
    hp = jnp.where(row % period == 0, first2, pltpu.roll(h, 1, 0))
    xx = hp - h
    for m in range(xmix_ref.shape[0]):
        xmix_ref[m] = (h + xx * mu_ref[m]).astype(BF16)


def _premix(x, shift, scale, first, mu):
    b, t, d = x.shape
    n_mix = mu.shape[0]
    c = _geo(b, t)
    g, mg, midx = c["groups"], c["mod_groups"], c["mod_index"]
    rows = g * SUBLANES
    xg = x.reshape(-1, SUBLANES, d)
    per_batch = pl.BlockSpec((mg, 1, d), lambda i: (midx(i), 0, 0))
    return pl.pallas_call(
        functools.partial(_premix_body, steps_per_batch=c["steps_per_batch"]),
        grid=(c["steps"],),
        in_specs=[
            pl.BlockSpec((g, SUBLANES, d), lambda i: (i, 0, 0)),
            pl.BlockSpec((1, SUBLANES, d), lambda i: (jnp.maximum(i * g - 1, 0), 0, 0)),
            per_batch, per_batch, per_batch,
            pl.BlockSpec((n_mix, 1, d), lambda i: (0, 0, 0)),
        ],
        out_specs=[pl.BlockSpec((n_mix, rows, d), lambda i: (0, i, 0)), per_batch],
        out_shape=[jax.ShapeDtypeStruct((n_mix, b * t, d), BF16),
                   jax.ShapeDtypeStruct((b, 1, d), F32)],
        compiler_params=_params(("arbitrary",)),
    )(xg, xg, first, shift, scale, mu.reshape(n_mix, 1, d))


def _route(h, rw_refs, rb_refs):
    t = h.shape[0]
    s, sb = [], []
    for rw_ref, rb_ref in zip(rw_refs, rb_refs):
        se = jax.nn.sigmoid(jnp.dot(h, rw_ref[...], preferred_element_type=F32,
                                    precision=lax.Precision.HIGHEST))
        s.append(se)
        sb.append(se + rb_ref[...])
    hi01, lo01 = jnp.maximum(sb[0], sb[1]), jnp.minimum(sb[0], sb[1])
    hi23, lo23 = jnp.maximum(sb[2], sb[3]), jnp.minimum(sb[2], sb[3])
    top1 = jnp.maximum(hi01, hi23)
    top2 = jnp.maximum(jnp.minimum(hi01, hi23), jnp.maximum(lo01, lo23))
    gscore = top1 + top2
    lane = lax.broadcasted_iota(jnp.int32, (t, N_GROUPS), 1)
    gmax = jnp.max(gscore, axis=-1, keepdims=True)
    gsel = jnp.min(jnp.where(gscore == gmax, lane, N_GROUPS), axis=-1, keepdims=True)
    in_group = lane == gsel
    masked = [jnp.where(in_group, v, NEG_INF) for v in sb]
    idx = [lane * EXPERTS_PER_GROUP + e for e in range(EXPERTS_PER_GROUP)]

    def pick(vals):
        best = functools.reduce(jnp.maximum, [jnp.max(v, axis=-1, keepdims=True) for v in vals])
        return functools.reduce(jnp.minimum, [
            jnp.min(jnp.where(v == best, ix, N_EXPERTS), axis=-1, keepdims=True)
            for v, ix in zip(vals, idx)])

    def weight(sel):
        return functools.reduce(jnp.add, [
            jnp.sum(jnp.where(ix == sel, v, 0.0), axis=-1, keepdims=True) for v, ix in zip(s, idx)])

    e1 = pick(masked)
    e2 = pick([jnp.where(ix == e1, -jnp.inf, v) for v, ix in zip(masked, idx)])
    w1, w2 = weight(e1), weight(e2)
    wsum = w1 + w2
    return jnp.where(lane == 0, e1.astype(F32),
                     jnp.where(lane == 1, e2.astype(F32),
                               jnp.where(lane == 2, w1 / wsum, jnp.where(lane == 3, w2 / wsum, 0.0))))


def _ln_mod_body(*refs, n_planes, n_mods, with_route):
    it = iter(refs)
    x_ref = next(it)
    o_refs = [next(it) for _ in range(n_planes)]
    gate_ref, lng_ref, lnb_ref = next(it), next(it), next(it)
    mod_refs = [(next(it), next(it)) for _ in range(n_mods)]
    rw_refs = [next(it) for _ in range(EXPERTS_PER_GROUP)] if with_route else None
    rb_refs = [next(it) for _ in range(EXPERTS_PER_GROUP)] if with_route else None
    xnew_ref = next(it)
    h_refs = [next(it) for _ in range(n_mods)]
    route_ref = next(it) if with_route else None

    g, s, d = x_ref.shape
    out = o_refs[0][...]
    for o_ref in o_refs[1:]:
        out = out + o_ref[...]
    y = ALPHA * x_ref[...] + gate_ref[...] * out
    mu = jnp.mean(y, axis=-1, keepdims=True)
    yc = y - mu
    var = jnp.mean(yc * yc, axis=-1, keepdims=True)
    xn = yc * lax.rsqrt(var + LN_EPS) * lng_ref[...] + lnb_ref[...]
    xnew_ref[...] = xn
    for k, ((shift_ref, scale_ref), h_ref) in enumerate(zip(mod_refs, h_refs)):
        h = (xn * (1.0 + scale_ref[...]) + shift_ref[...]).reshape(g * s, d)
        h_ref[...] = h.astype(BF16)
        if with_route and k == 0:
            route_ref[...] = _route(h, rw_refs, rb_refs)


def _ln_mod(x, planes, gate, ln_g, ln_b, mods, router):
    b, t, d = x.shape
    n = b * t
    c = _geo(b, t)
    g, mg, midx = c["groups"], c["mod_groups"], c["mod_index"]
    rows = g * SUBLANES
    with_route = router is not None
    tok = pl.BlockSpec((g, SUBLANES, d), lambda i: (i, 0, 0))
    per_batch = pl.BlockSpec((mg, 1, d), lambda i: (midx(i), 0, 0))
    const = pl.BlockSpec((1, 1, d), lambda i: (0, 0, 0))
    in_specs = [tok] * (1 + len(planes)) + [per_batch, const, const]
    args = [x.reshape(-1, SUBLANES, d)] + [p.reshape(-1, SUBLANES, d) for p in planes]
    args += [gate, ln_g.reshape(1, 1, d), ln_b.reshape(1, 1, d)]
    for shift, scale in mods:
        in_specs += [per_batch, per_batch]
        args += [shift, scale]
    if with_route:
        rws, rbs = router
        in_specs += [pl.BlockSpec(a.shape, lambda i: (0, 0)) for a in list(rws) + list(rbs)]
        args += list(rws) + list(rbs)
    out_shape = [jax.ShapeDtypeStruct((n // SUBLANES, SUBLANES, d), F32)]
    out_specs = [tok]
    for _ in mods:
        out_shape.append(jax.ShapeDtypeStruct((n, d), BF16))
        out_specs.append(pl.BlockSpec((rows, d), lambda i: (i, 0)))
    if with_route:
        out_shape.append(jax.ShapeDtypeStruct((n, N_GROUPS), F32))
        out_specs.append(pl.BlockSpec((rows, N_GROUPS), lambda i: (i, 0)))
    outs = pl.pallas_call(
        functools.partial(_ln_mod_body, n_planes=len(planes), n_mods=len(mods), with_route=with_route),
        grid=(c["steps"],),
        in_specs=in_specs,
        out_specs=out_specs,
        out_shape=out_shape,
        compiler_params=_params(("parallel",)),
    )(*args)
    return [outs[0].reshape(b, t, d)] + list(outs[1:])


def _group_consts():
    lane = np.arange(LANE_GROUP)
    bd = (lane[:, None] // RW_HEAD == lane[None, :] // RW_HEAD).astype(np.float32)
    eye = (np.arange(RW_HEAD)[:, None] == lane[None, :] % RW_HEAD).astype(np.float32)
    return jnp.asarray(bd, BF16), jnp.asarray(eye, F32)


def _head_sum(x, bd):
    hi = x.astype(BF16)
    lo = (x - hi.astype(F32)).astype(BF16)
    return jnp.dot(hi, bd, preferred_element_type=F32) + jnp.dot(lo, bd, preferred_element_type=F32)


def _head_sum_rows(x, bd):
    d = x.shape[-1]
    return jnp.concatenate(
        [_head_sum(x[:, c:c + LANE_GROUP], bd) for c in range(0, d, LANE_GROUP)], axis=-1)


def _scan_body(*refs, has_s0):
    it = iter(refs)
    r_ref, d_ref, k_ref, v_ref, a_ref = (next(it) for _ in range(5))
    kk_ref, ka_ref, bd_ref, eye_ref = next(it), next(it), next(it), next(it)
    s0_ref = next(it) if has_s0 else None
    y_ref, st_ref = next(it), next(it)
    kn_ref, al_ref, km_ref, lhs1_ref, lhs2_ref = (next(it) for _ in range(5))

    nb, tt, d = r_ref.shape
    ng = st_ref.shape[1]
    half = nb * ng * RW_HEAD

    @pl.when(pl.program_id(1) == 0)
    def _():
        if has_s0:
            st_ref[...] = s0_ref[...]
        else:
            st_ref[...] = jnp.zeros_like(st_ref)

    bd = bd_ref[...]
    eye = eye_ref[...]
    k = k_ref[...].reshape(nb * tt, d)
    a = a_ref[...].reshape(nb * tt, d)
    kk = k * kk_ref[...]
    kn = kk * lax.rsqrt(_head_sum_rows(kk * kk, bd) + 1e-12)
    kn_ref[...] = kn.reshape(nb, tt, d)
    al_ref[...] = (kn * a).reshape(nb, tt, d)
    km_ref[...] = (k * (1.0 + (a - 1.0) * ka_ref[...])).reshape(nb, tt, d)

    def step(t, carry):
        def row(ref, n, g):
            return ref[n, pl.ds(t, 1), pl.ds(g * LANE_GROUP, LANE_GROUP)]

        for n in range(nb):
            for g in range(ng):
                q = (n * ng + g) * RW_HEAD
                lhs1_ref[pl.ds(q, RW_HEAD), :] = (st_ref[n, g] * row(kn_ref, n, g)).astype(BF16)
                lhs1_ref[pl.ds(half + q, RW_HEAD), :] = (eye * row(v_ref, n, g)).astype(BF16)
        res1 = jnp.dot(lhs1_ref[...], bd, preferred_element_type=F32)
        for n in range(nb):
            for g in range(ng):
                q = (n * ng + g) * RW_HEAD
                skk = res1[q:q + RW_HEAD]
                vcol = res1[half + q:half + q + RW_HEAD]
                s_new = (st_ref[n, g] * row(d_ref, n, g) - skk * row(al_ref, n, g)
                         + vcol * row(km_ref, n, g))
                st_ref[n, g] = s_new
                lhs2_ref[pl.ds(q, RW_HEAD), :] = (s_new * row(r_ref, n, g)).astype(BF16)
        res2 = jnp.dot(lhs2_ref[...], bd, preferred_element_type=F32)
        for n in range(nb):
            for g in range(ng):
                q = (n * ng + g) * RW_HEAD
                yrow = jnp.sum(res2[q:q + RW_HEAD] * eye, axis=0, keepdims=True)
                y_ref[n, pl.ds(t, 1), pl.ds(g * LANE_GROUP, LANE_GROUP)] = yrow
        return carry

    lax.fori_loop(0, tt, step, 0)


def _rwkv_scan(r, dcy, k, v, a, kk, ka, s0, b, t):
    n, d = r.shape
    ng = d // LANE_GROUP
    bd, eye = _group_consts()
    if t > SCAN_TT:
        nb, tt = b, SCAN_TT
    else:
        nb, tt = min(SCAN_NB, b), t
    tok = pl.BlockSpec((nb, tt, d), lambda i, c: (i, c, 0))
    const2 = lambda shape: pl.BlockSpec(shape, lambda i, c: (0, 0))
    state = pl.BlockSpec((nb, ng, RW_HEAD, LANE_GROUP), lambda i, c: (i, 0, 0, 0))
    in_specs = [tok] * 5 + [const2((1, d)), const2((1, d)), const2(bd.shape), const2(eye.shape)]
    args = [x.reshape(b, t, d) for x in (r, dcy, k, v, a)] + [kk.reshape(1, d), ka.reshape(1, d), bd, eye]
    if s0 is not None:
        in_specs.append(state)
        args.append(s0)
    rows = nb * ng * RW_HEAD
    y, st = pl.pallas_call(
        functools.partial(_scan_body, has_s0=s0 is not None),
        grid=(b // nb, t // tt),
        in_specs=in_specs,
        out_specs=[tok, state],
        out_shape=[jax.ShapeDtypeStruct((b, t, d), F32),
                   jax.ShapeDtypeStruct((b, ng, RW_HEAD, LANE_GROUP), F32)],
        scratch_shapes=[pltpu.VMEM((nb, tt, d), F32)] * 3
        + [pltpu.VMEM((2 * rows, LANE_GROUP), BF16), pltpu.VMEM((rows, LANE_GROUP), BF16)],
        compiler_params=_params(("parallel", "arbitrary")),
    )(*args)
    return y.reshape(n, d), st


def _state_to_groups(s):
    b, h, nv, nk = s.shape
    s = s.reshape(b, h // HEADS_PER_GROUP, HEADS_PER_GROUP, nv, nk)
    return s.transpose(0, 1, 3, 2, 4).reshape(b, h // HEADS_PER_GROUP, nv, HEADS_PER_GROUP * nk)


def _state_from_groups(s):
    b, ng, nv, _ = s.shape
    s = s.reshape(b, ng, nv, HEADS_PER_GROUP, RW_HEAD)
    return s.transpose(0, 1, 3, 2, 4).reshape(b, ng * HEADS_PER_GROUP, nv, RW_HEAD)


def _rwkv_post_body(y_ref, r_ref, k_ref, a_ref, v_ref, g_ref, lnw_ref, lnb_ref, rk_ref, ka_ref,
                    bd_ref, z_ref):
    bd = bd_ref[...]
    y = y_ref[...]
    inv = 1.0 / RW_HEAD
    mean = _head_sum_rows(y, bd) * inv
    yc = y - mean
    var = _head_sum_rows(yc * yc, bd) * inv
    yn = yc * lax.rsqrt(var + RW_LN_EPS) * lnw_ref[...] + lnb_ref[...]
    a = a_ref[...]
    km = k_ref[...] * (1.0 + (a - 1.0) * ka_ref[...])
    bonus = _head_sum_rows(r_ref[...] * km * rk_ref[...], bd) * v_ref[...]
    z_ref[...] = ((yn + bonus) * g_ref[...]).astype(BF16)


def _rwkv_post(y, r, k, a, v, g, lnw, lnb, rk, ka):
    n, d = y.shape
    bd, _ = _group_consts()
    rows = min(EW_ROWS, n)
    tok = pl.BlockSpec((rows, d), lambda i: (i, 0))
    vec = pl.BlockSpec((1, d), lambda i: (0, 0))
    return pl.pallas_call(
        _rwkv_post_body,
        grid=(n // rows,),
        in_specs=[tok] * 6 + [vec] * 4 + [pl.BlockSpec(bd.shape, lambda i: (0, 0))],
        out_specs=tok,
        out_shape=jax.ShapeDtypeStruct((n, d), BF16),
        compiler_params=_params(("parallel",)),
    )(y, r, k, a, v, g, lnw.reshape(1, d), lnb.reshape(1, d), rk.reshape(1, d), ka.reshape(1, d), bd)


def _attn_unit(q, k_prev, v_prev, k_cur, v_cur, sink_ref, prev_limit, o_ref, row0):
    tq = q.shape[0]
    kvh = k_cur.shape[-1] // ATT_HEAD_DIM
    m_rows = GQA * tq
    ri = lax.broadcasted_iota(jnp.int32, (m_rows, 1), 0)
    qi = ri % tq
    head_in_group = ri // tq
    jp = lax.broadcasted_iota(jnp.int32, (1, WINDOW), 1)
    jc = lax.broadcasted_iota(jnp.int32, (1, tq), 1)
    diff_p = WINDOW + qi - jp
    diff_c = qi - jc
    valid_p = diff_p < prev_limit
    valid_c = diff_c >= 0
    nt = (((1,), (1,)), ((), ()))
    for kh in range(kvh):
        heads = [kh * GQA + h for h in range(GQA)]
        qs = jnp.concatenate(
            [q[:, h * ATT_HEAD_DIM:(h + 1) * ATT_HEAD_DIM] for h in heads], axis=0).astype(BF16)
        sl = slice(kh * ATT_HEAD_DIM, (kh + 1) * ATT_HEAD_DIM)
        kp, vp = k_prev[:, sl].astype(BF16), v_prev[:, sl].astype(BF16)
        kc, vc = k_cur[:, sl].astype(BF16), v_cur[:, sl].astype(BF16)
        s_p = lax.dot_general(qs, kp, nt, preferred_element_type=F32) * (ATT_HEAD_DIM ** -0.5)
        s_c = lax.dot_general(qs, kc, nt, preferred_element_type=F32) * (ATT_HEAD_DIM ** -0.5)
        hidx = (kh * GQA + head_in_group).astype(F32)
        slope = jnp.exp2(-8.0 * (hidx + 1.0) / (kvh * GQA))
        s_p = jnp.where(valid_p, s_p - slope * diff_p.astype(F32), NEG_INF)
        s_c = jnp.where(valid_c, s_c - slope * diff_c.astype(F32), NEG_INF)
        sink = sink_ref[kh]
        m = jnp.maximum(jnp.maximum(jnp.max(s_p, axis=-1, keepdims=True),
                                    jnp.max(s_c, axis=-1, keepdims=True)), sink)
        p_p = jnp.exp(s_p - m)
        p_c = jnp.exp(s_c - m)
        den = (jnp.sum(p_p, axis=-1, keepdims=True) + jnp.sum(p_c, axis=-1, keepdims=True)
               + jnp.exp(sink - m))
        o = (jnp.dot((p_p / den).astype(BF16), vp, preferred_element_type=F32)
             + jnp.dot((p_c / den).astype(BF16), vc, preferred_element_type=F32))
        for hq, h in enumerate(heads):
            o_ref[pl.ds(row0, tq), pl.ds(h * ATT_HEAD_DIM, ATT_HEAD_DIM)] = o[hq * tq:(hq + 1) * tq]


def _attn_prefill_body(q_ref, kvc_ref, kvp_ref, sink_ref, o_ref):
    kvd = kvc_ref.shape[-1] // 2
    kvc, kvp = kvc_ref[...], kvp_ref[...]
    prev_limit = jnp.where(pl.program_id(1) > 0, WINDOW, 0)
    _attn_unit(q_ref[...], kvp[:, :kvd], kvp[:, kvd:], kvc[:, :kvd], kvc[:, kvd:], sink_ref,
               prev_limit, o_ref, 0)


def _attn_decode_body(q_ref, kvc_ref, kc_ref, vc_ref, sink_ref, o_ref, *, nb, tq):
    kvd = kvc_ref.shape[-1] // 2
    for n in range(nb):
        kvc = kvc_ref[pl.ds(n * tq, tq), :]
        _attn_unit(q_ref[pl.ds(n * tq, tq), :], kc_ref[n], vc_ref[n], kvc[:, :kvd], kvc[:, kvd:],
                   sink_ref, WINDOW, o_ref, n * tq)


def _sink_rows(sinks, kvh, tq):
    return jnp.repeat(sinks.reshape(kvh, GQA), tq, axis=1).reshape(kvh, GQA * tq, 1).astype(F32)


def _attn_prefill(q, kv, sinks, b, t):
    n, dq = q.shape
    kv2 = kv.shape[-1]
    nblk = t // WINDOW
    sink = _sink_rows(sinks, kv2 // 2 // ATT_HEAD_DIM, WINDOW)
    return pl.pallas_call(
        _attn_prefill_body,
        grid=(b, nblk),
        in_specs=[
            pl.BlockSpec((WINDOW, dq), lambda i, j: (i * nblk + j, 0)),
            pl.BlockSpec((WINDOW, kv2), lambda i, j: (i * nblk + j, 0)),
            pl.BlockSpec((WINDOW, kv2), lambda i, j: (i * nblk + jnp.maximum(j - 1, 0), 0)),
            pl.BlockSpec(sink.shape, lambda i, j: (0, 0, 0)),
        ],
        out_specs=pl.BlockSpec((WINDOW, dq), lambda i, j: (i * nblk + j, 0)),
        out_shape=jax.ShapeDtypeStruct((n, dq), F32),
        compiler_params=_params(("parallel", "arbitrary")),
    )(q, kv, kv, sink)


def _attn_decode(q, kv, cache_k, cache_v, sinks, b, t):
    n, dq = q.shape
    kv2 = kv.shape[-1]
    kvd = kv2 // 2
    nb = min(ATT_NB, b)
    rows = nb * t
    sink = _sink_rows(sinks, kvd // ATT_HEAD_DIM, t)
    return pl.pallas_call(
        functools.partial(_attn_decode_body, nb=nb, tq=t),
        grid=(b // nb,),
        in_specs=[
            pl.BlockSpec((rows, dq), lambda i: (i, 0)),
            pl.BlockSpec((rows, kv2), lambda i: (i, 0)),
            pl.BlockSpec((nb, WINDOW, kvd), lambda i: (i, 0, 0)),
            pl.BlockSpec((nb, WINDOW, kvd), lambda i: (i, 0, 0)),
            pl.BlockSpec(sink.shape, lambda i: (0, 0, 0)),
        ],
        out_specs=pl.BlockSpec((rows, dq), lambda i: (i, 0)),
        out_shape=jax.ShapeDtypeStruct((n, dq), F32),
        compiler_params=_params(("parallel",)),
    )(q, kv, cache_k.reshape(b, WINDOW, kvd), cache_v.reshape(b, WINDOW, kvd), sink)


def _moe_body(te_ref, tv_ref, x_ref, gate_ref, wg_ref, wu_ref, wd_ref, o_ref, wg16, wu16, wd16):
    t = pl.program_id(0)
    prev = te_ref[jnp.maximum(t - 1, 0)]

    @pl.when((t == 0) | (te_ref[t] != prev))
    def _():
        wg16[...] = wg_ref[...].astype(BF16)
        wu16[...] = wu_ref[...].astype(BF16)
        wd16[...] = wd_ref[...].astype(BF16)

    @pl.when(tv_ref[t] > 0)
    def _():
        x = x_ref[...]
        hg = jnp.dot(x, wg16[...], preferred_element_type=F32)
        hu = jnp.dot(x, wu16[...], preferred_element_type=F32)
        act = (hg * jax.nn.sigmoid(hg)) * hu * gate_ref[...]
        o_ref[...] = jnp.dot(act.astype(BF16), wd16[...], preferred_element_type=F32)

    @pl.when(tv_ref[t] == 0)
    def _():
        o_ref[...] = jnp.zeros_like(o_ref)


def _moe_plan(route, n_tiles):
    n = route.shape[0]
    e = route[:, :2].astype(jnp.int32).reshape(-1)
    w = route[:, 2:4].reshape(-1)
    onehot = (e[:, None] == jnp.arange(N_EXPERTS, dtype=jnp.int32)[None, :]).astype(jnp.int32)
    csum = jnp.cumsum(onehot, axis=0)
    rank = jnp.take_along_axis(csum, e[:, None], axis=1)[:, 0] - 1
    counts = csum[-1]
    padded = ((counts + MOE_TG - 1) // MOE_TG) * MOE_TG
    pend = jnp.cumsum(padded)
    dest = (pend - padded)[e] + rank
    p_rows = n_tiles * MOE_TG
    row_token = jnp.zeros((p_rows,), jnp.int32).at[dest].set(jnp.arange(2 * n, dtype=jnp.int32) // 2)
    row_gate = jnp.zeros((p_rows,), F32).at[dest].set(w)
    starts = jnp.arange(n_tiles, dtype=jnp.int32) * MOE_TG
    tile_valid = (starts < pend[-1]).astype(jnp.int32)
    tile_expert = jnp.minimum(jnp.searchsorted(pend, starts, side="right"), N_EXPERTS - 1)
    last_valid = jnp.max(jnp.where(tile_valid > 0, tile_expert, 0))
    tile_expert = jnp.where(tile_valid > 0, tile_expert, last_valid).astype(jnp.int32)
    return dest.reshape(n, 2), row_token, row_gate, tile_expert, tile_valid


def _moe(h16, route, w_gate, w_up, w_down, layer):
    n, d = h16.shape
    f = w_gate.shape[-1]
    n_tiles = (2 * n) // MOE_TG + N_EXPERTS
    dest, row_token, row_gate, tile_expert, tile_valid = _moe_plan(route, n_tiles)
    xs = jnp.take(h16, row_token, axis=0)
    out = pl.pallas_call(
        _moe_body,
        grid_spec=pltpu.PrefetchScalarGridSpec(
            num_scalar_prefetch=2,
            grid=(n_tiles,),
            in_specs=[
                pl.BlockSpec((MOE_TG, d), lambda t, te, tv: (t, 0)),
                pl.BlockSpec((MOE_TG, 1), lambda t, te, tv: (t, 0)),
                pl.BlockSpec((None, None, d, f), lambda t, te, tv: (layer, te[t], 0, 0)),
                pl.BlockSpec((None, None, d, f), lambda t, te, tv: (layer, te[t], 0, 0)),
                pl.BlockSpec((None, None, f, d), lambda t, te, tv: (layer, te[t], 0, 0)),
            ],
            out_specs=pl.BlockSpec((MOE_TG, d), lambda t, te, tv: (t, 0)),
            scratch_shapes=[pltpu.VMEM((d, f), BF16), pltpu.VMEM((d, f), BF16),
                            pltpu.VMEM((f, d), BF16)],
        ),
        out_shape=jax.ShapeDtypeStruct((n_tiles * MOE_TG, d), F32),
        compiler_params=_params(("arbitrary",)),
    )(tile_expert, tile_valid, xs, row_gate.reshape(-1, 1), w_gate, w_up, w_down)
    return jnp.take(out, dest[:, 0], axis=0), jnp.take(out, dest[:, 1], axis=0)


def _split_mod(mod, n_parts, b_first):
    d = mod.shape[-1] // n_parts
    parts = [mod[:, k * d:(k + 1) * d] for k in range(n_parts)]
    return [[p[:b_first, None, :] for p in parts], [p[b_first:, None, :] for p in parts]]


def kernel(x_prompt, x_sample, c_prompt, c_sample, state_wkv, state_shift, cache_k_win, cache_v_win, ada_w, ada_b, ln_g, ln_b, rw_mu, rw_w_rkv, rw_w0, rw_w1, rw_w2, rw_a0, rw_a1, rw_a2, rw_g1, rw_g2, rw_k_k, rw_k_a, rw_r_k, rw_lnx_w, rw_lnx_b, rw_wo, ada_kv_w, ada_kv_b, w_kv, w_q, attn_sinks, w_o_attn, router_w, router_bias, moe_w_gate, moe_w_up, moe_w_down):
    d = x_prompt.shape[-1]
    xs = [x_prompt, x_sample]
    shapes = [x.shape[:2] for x in xs]
    counts = [b * t for b, t in shapes]
    bp = shapes[0][0]

    c_all = jnp.concatenate([c_prompt, c_sample], axis=0)
    mods = {(l, s): _split_mod(_mm(c_all, ada_w, w_lead=(l, s), bias=ada_b[l, s], in_act="silu"), 3, bp)
            for l in range(DEPTH) for s in range(2)}
    mods_kv = _split_mod(_mm(c_all, ada_kv_w, bias=ada_kv_b, in_act="silu"), 2, bp)

    rws = [router_w[:, e::EXPERTS_PER_GROUP] for e in range(EXPERTS_PER_GROUP)]
    rbs = [router_bias[e::EXPERTS_PER_GROUP].reshape(1, N_GROUPS) for e in range(EXPERTS_PER_GROUP)]
    router = (rws, rbs)

    def moe_layer(hs, routes, layer):
        o1, o2 = _moe(jnp.concatenate(hs, axis=0), jnp.concatenate(routes, axis=0),
                      moe_w_gate, moe_w_up, moe_w_down, layer)
        return [[o1[:counts[0]], o2[:counts[0]]], [o1[counts[0]:], o2[counts[0]:]]]

    firsts = [jnp.zeros((bp, 1, d), F32), state_shift[0][:, None, :]]
    s0s = [None, _state_to_groups(state_wkv[0])]
    x1, h1, route1, states, hlasts = [], [], [], [], []
    for tr in range(2):
        b, t = shapes[tr]
        shift, scale, gate = mods[0, 0][tr]
        xmix, hlast = _premix(xs[tr], shift, scale, firsts[tr], rw_mu[0])
        r = _mm(xmix, rw_w_rkv, x_lead=(0,), w_lead=(0, 0))
        k = _mm(xmix, rw_w_rkv, x_lead=(2,), w_lead=(0, 1))
        v = _mm(xmix, rw_w_rkv, x_lead=(3,), w_lead=(0, 2))
        lw = _mm(xmix, rw_w1, x_lead=(1,), w_lead=(0,), out_act="tanh", out_dtype=BF16)
        la = _mm(xmix, rw_a1, x_lead=(4,), w_lead=(0,), out_dtype=BF16)
        lg = _mm(xmix, rw_g1, x_lead=(5,), w_lead=(0,), out_act="sigmoid", out_dtype=BF16)
        decay = _mm(lw, rw_w2, w_lead=(0,), bias=rw_w0[0], out_act="decay")
        a = _mm(la, rw_a2, w_lead=(0,), bias=rw_a0[0], out_act="sigmoid")
        g = _mm(lg, rw_g2, w_lead=(0,))
        y, st = _rwkv_scan(r, decay, k, v, a, rw_k_k[0], rw_k_a[0], s0s[tr], b, t)
        z = _rwkv_post(y, r, k, a, v, g, rw_lnx_w[0], rw_lnx_b[0], rw_r_k[0], rw_k_a[0])
        out = _mm(z, rw_wo, w_lead=(0,))
        sh, sc, _ = mods[0, 1][tr]
        xn, h, route = _ln_mod(xs[tr], [out], gate, ln_g[0, 0], ln_b[0, 0], [(sh, sc)], router)
        x1.append(xn), h1.append(h), route1.append(route), states.append(st), hlasts.append(hlast)
    moe1 = moe_layer(h1, route1, 0)

    caches = [None, (cache_k_win, cache_v_win)]
    x3, h3, route3, kvs = [], [], [], []
    for tr in range(2):
        b, t = shapes[tr]
        shq, scq, gate_q = mods[1, 0][tr]
        shk, sck = mods_kv[tr]
        x2, hq, hkv = _ln_mod(x1[tr], moe1[tr], mods[0, 1][tr][2], ln_g[0, 1], ln_b[0, 1],
                              [(shq, scq), (shk, sck)], None)
        q = _mm(hq, w_q, w_lead=(0,))
        kv = _mm(hkv, w_kv)
        if caches[tr] is None:
            o = _attn_prefill(q, kv, attn_sinks[0], b, t)
        else:
            o = _attn_decode(q, kv, caches[tr][0], caches[tr][1], attn_sinks[0], b, t)
        out = _mm(o, w_o_attn, w_lead=(0,))
        sh, sc, _ = mods[1, 1][tr]
        xn, h, route = _ln_mod(x2, [out], gate_q, ln_g[1, 0], ln_b[1, 0], [(sh, sc)], router)
        x3.append(xn), h3.append(h), route3.append(route), kvs.append(kv)
    moe3 = moe_layer(h3, route3, 1)
    ys = [_ln_mod(x3[tr], moe3[tr], mods[1, 1][tr][2], ln_g[1, 1], ln_b[1, 1], [], None)[0]
          for tr in range(2)]

    kvd = kvs[0].shape[-1] // 2
    kvh = kvd // ATT_HEAD_DIM
    (bp, tp), (bs, ts) = shapes
    kv_p = kvs[0].reshape(bp, tp, 2 * kvd)[:, tp - WINDOW:]
    p_k_win = kv_p[..., :kvd].reshape(bp, WINDOW, kvh, ATT_HEAD_DIM)
    p_v_win = kv_p[..., kvd:].reshape(bp, WINDOW, kvh, ATT_HEAD_DIM)
    kv_s = kvs[1].reshape(bs, ts, 2 * kvd)
    s_k_win = jnp.concatenate(
        [cache_k_win, kv_s[..., :kvd].reshape(bs, ts, kvh, ATT_HEAD_DIM)], axis=1)[:, -WINDOW:]
    s_v_win = jnp.concatenate(
        [cache_v_win, kv_s[..., kvd:].reshape(bs, ts, kvh, ATT_HEAD_DIM)], axis=1)[:, -WINDOW:]
    p_wkv = _state_from_groups(states[0])[None]
    s_wkv = _state_from_groups(states[1])[None]
    p_shift = hlasts[0].reshape(1, bp, d)
    s_shift = hlasts[1].reshape(1, bs, d)
    return (ys[0], ys[1], p_wkv, p_shift, p_k_win, p_v_win, s_wkv, s_shift, s_k_win, s_v_win)
```

```python
import functools

import jax
import jax.numpy as jnp
import numpy as np
from jax import lax
from jax.experimental import pallas as pl
from jax.experimental.pallas import tpu as pltpu

F32 = jnp.float32
BF16 = jnp.bfloat16

RW_HEAD = 64
ATT_HEAD_DIM = 64
GQA = 8
WINDOW = 128
N_EXPERTS = 32
EXPERTS_PER_GROUP = 4
N_GROUPS = N_EXPERTS // EXPERTS_PER_GROUP
DEPTH = 2
ALPHA = (2.0 * DEPTH) ** 0.25
LN_EPS = 1e-5
RW_LN_EPS = 64e-5
NEG_INF = -1e30

SUBLANES = 8
LANE_GROUP = 256
HEADS_PER_GROUP = LANE_GROUP // RW_HEAD
VMEM_LIMIT = 56 * 1024 * 1024

MM_TM = 1024
MM_TN = 512
EW_ROWS = 256
SCAN_TT = 64
SCAN_NB = 4
ATT_NB = 8
MOE_TG = 256


def _params(sem):
    return pltpu.CompilerParams(dimension_semantics=sem, vmem_limit_bytes=VMEM_LIMIT)


def _softplus(z):
    return jnp.maximum(z, 0.0) + jnp.log1p(jnp.exp(-jnp.abs(z)))


def _act(x, kind):
    if kind is None:
        return x
    if kind == "silu":
        return x * jax.nn.sigmoid(x)
    if kind == "tanh":
        return jnp.tanh(x)
    if kind == "sigmoid":
        return jax.nn.sigmoid(x)
    if kind == "decay":
        w = -_softplus(-x) - 0.5
        return jnp.exp(-jnp.exp(w))
    raise ValueError(kind)


def _mm_body(*refs, in_act, out_act, has_bias):
    x_ref, w_ref = refs[0], refs[1]
    b_ref = refs[2] if has_bias else None
    o_ref = refs[2 + has_bias]
    xs_ref = refs[3 + has_bias]

    @pl.when(pl.program_id(1) == 0)
    def _():
        xs_ref[...] = _act(x_ref[...].astype(F32), in_act).astype(BF16)

    acc = jnp.dot(xs_ref[...], w_ref[...].astype(BF16), preferred_element_type=F32)
    if has_bias:
        acc = acc + b_ref[...]
    o_ref[...] = _act(acc, out_act).astype(o_ref.dtype)


def _mm(x, w, *, name, x_lead=(), w_lead=(), bias=None, in_act=None, out_act=None, out_dtype=F32,
        tm=MM_TM, tn=MM_TN):
    m, k = x.shape[-2:]
    n = w.shape[-1]
    tm = min(tm, m)
    tn = min(tn, n)
    assert m % tm == 0 and n % tn == 0, (m, tm, n, tn)
    nx, nw = len(x_lead), len(w_lead)
    in_specs = [
        pl.BlockSpec((None,) * nx + (tm, k), lambda i, j: tuple(x_lead) + (i, 0)),
        pl.BlockSpec((None,) * nw + (k, tn), lambda i, j: tuple(w_lead) + (0, j)),
    ]
    args = [x, w]
    if bias is not None:
        in_specs.append(pl.BlockSpec((1, tn), lambda i, j: (0, j)))
        args.append(bias.reshape(1, n).astype(F32))
    body = functools.partial(_mm_body, in_act=in_act, out_act=out_act, has_bias=bias is not None)
    return pl.pallas_call(
        body,
        grid=(m // tm, n // tn),
        in_specs=in_specs,
        out_specs=pl.BlockSpec((tm, tn), lambda i, j: (i, j)),
        out_shape=jax.ShapeDtypeStruct((m, n), out_dtype),
        scratch_shapes=[pltpu.VMEM((tm, k), BF16)],
        compiler_params=_params(("parallel", "arbitrary")),
        name=name,
    )(*args)


def _geo(b, t):
    if t == SUBLANES:
        g = min(EW_ROWS // SUBLANES, b)
        return dict(groups=g, steps=b // g, mod_groups=g, steps_per_batch=1, mod_index=lambda i: i)
    assert t % EW_ROWS == 0
    spb = t // EW_ROWS
    return dict(groups=EW_ROWS // SUBLANES, steps=b * spb, mod_groups=1, steps_per_batch=spb,
                mod_index=lambda i: i // spb)


def _premix_body(x_ref, xprev_ref, first_ref, shift_ref, scale_ref, mu_ref, xmix_ref, hlast_ref, *,
                 steps_per_batch):
    i = pl.program_id(0)
    g, s, d = x_ref.shape
    gf = first_ref.shape[0]
    scale = scale_ref[...]
    shift = shift_ref[...]
    h3 = x_ref[...] * (1.0 + scale) + shift
    hlast_ref[...] = h3[g - gf:, s - 1:, :]
    h = h3.reshape(g * s, d)
    if steps_per_batch > 1:
        hprev_row = xprev_ref[:, s - 1:, :] * (1.0 + scale) + shift
        first = jnp.where(i % steps_per_batch == 0, first_ref[...], hprev_row)
        period = g * s
    else:
        first = first_ref[...]
        period = s
    first2 = jnp.broadcast_to(first, (g, s, d)).reshape(g * s, d)
    row = lax.broadcasted_iota(jnp.int32, (g * s, 1), 0)
    hp = jnp.where(row % period == 0, first2, pltpu.roll(h, 1, 0))
    xx = hp - h
    for m in range(xmix_ref.shape[0]):
        xmix_ref[m] = (h + xx * mu_ref[m]).astype(BF16)


def _premix(x, shift, scale, first, mu):
    b, t, d = x.shape
    n_mix = mu.shape[0]
    c = _geo(b, t)
    g, mg, midx = c["groups"], c["mod_groups"], c["mod_index"]
    rows = g * SUBLANES
    xg = x.reshape(-1, SUBLANES, d)
    per_batch = pl.BlockSpec((mg, 1, d), lambda i: (midx(i), 0, 0))
    return pl.pallas_call(
        functools.partial(_premix_body, steps_per_batch=c["steps_per_batch"]),
        grid=(c["steps"],),
        in_specs=[
            pl.BlockSpec((g, SUBLANES, d), lambda i: (i, 0, 0)),
            pl.BlockSpec((1, SUBLANES, d), lambda i: (jnp.maximum(i * g - 1, 0), 0, 0)),
            per_batch, per_batch, per_batch,
            pl.BlockSpec((n_mix, 1, d), lambda i: (0, 0, 0)),
        ],
        out_specs=[pl.BlockSpec((n_mix, rows, d), lambda i: (0, i, 0)), per_batch],
        out_shape=[jax.ShapeDtypeStruct((n_mix, b * t, d), BF16),
                   jax.ShapeDtypeStruct((b, 1, d), F32)],
        compiler_params=_params(("arbitrary",)),
        name="premix",
    )(xg, xg, first, shift, scale, mu.reshape(n_mix, 1, d))


def _route(h, rw_refs, rb_refs):
    t = h.shape[0]
    s, sb = [], []
    for rw_ref, rb_ref in zip(rw_refs, rb_refs):
        se = jax.nn.sigmoid(jnp.dot(h, rw_ref[...], preferred_element_type=F32,
                                    precision=lax.Precision.HIGHEST))
        s.append(se)
        sb.append(se + rb_ref[...])
    hi01, lo01 = jnp.maximum(sb[0], sb[1]), jnp.minimum(sb[0], sb[1])
    hi23, lo23 = jnp.maximum(sb[2], sb[3]), jnp.minimum(sb[2], sb[3])
    top1 = jnp.maximum(hi01, hi23)
    top2 = jnp.maximum(jnp.minimum(hi01, hi23), jnp.maximum(lo01, lo23))
    gscore = top1 + top2
    lane = lax.broadcasted_iota(jnp.int32, (t, N_GROUPS), 1)
    gmax = jnp.max(gscore, axis=-1, keepdims=True)
    gsel = jnp.min(jnp.where(gscore == gmax, lane, N_GROUPS), axis=-1, keepdims=True)
    in_group = lane == gsel
    masked = [jnp.where(in_group, v, NEG_INF) for v in sb]
    idx = [lane * EXPERTS_PER_GROUP + e for e in range(EXPERTS_PER_GROUP)]

    def pick(vals):
        best = functools.reduce(jnp.maximum, [jnp.max(v, axis=-1, keepdims=True) for v in vals])
        return functools.reduce(jnp.minimum, [
            jnp.min(jnp.where(v == best, ix, N_EXPERTS), axis=-1, keepdims=True)
            for v, ix in zip(vals, idx)])

    def weight(sel):
        return functools.reduce(jnp.add, [
            jnp.sum(jnp.where(ix == sel, v, 0.0), axis=-1, keepdims=True) for v, ix in zip(s, idx)])

    e1 = pick(masked)
    e2 = pick([jnp.where(ix == e1, -jnp.inf, v) for v, ix in zip(masked, idx)])
    w1, w2 = weight(e1), weight(e2)
    wsum = w1 + w2
    return jnp.where(lane == 0, e1.astype(F32),
                     jnp.where(lane == 1, e2.astype(F32),
                               jnp.where(lane == 2, w1 / wsum, jnp.where(lane == 3, w2 / wsum, 0.0))))


def _ln_mod_body(*refs, n_planes, n_mods, with_route, steps):
    if with_route:
        @pl.when(pl.program_id(0) < steps)
        def _():
            _ln_mod_compute(*refs, n_planes=n_planes, n_mods=n_mods, with_route=True)

        @pl.when(pl.program_id(0) >= steps)
        def _():
            h_ref = refs[-2]
            h_ref[...] = jnp.zeros_like(h_ref)
    else:
        _ln_mod_compute(*refs, n_planes=n_planes, n_mods=n_mods, with_route=False)


def _ln_mod_compute(*refs, n_planes, n_mods, with_route):
    it = iter(refs)
    x_ref = next(it)
    o_refs = [next(it) for _ in range(n_planes)]
    gate_ref, lng_ref, lnb_ref = next(it), next(it), next(it)
    mod_refs = [(next(it), next(it)) for _ in range(n_mods)]
    rw_refs = [next(it) for _ in range(EXPERTS_PER_GROUP)] if with_route else None
    rb_refs = [next(it) for _ in range(EXPERTS_PER_GROUP)] if with_route else None
    if with_route:
        next(it)
    xnew_ref = next(it)
    h_refs = [next(it) for _ in range(n_mods)]
    route_ref = next(it) if with_route else None

    g, s, d = x_ref.shape
    out = o_refs[0][...]
    for o_ref in o_refs[1:]:
        out = out + o_ref[...]
    y = ALPHA * x_ref[...] + gate_ref[...] * out
    mu = jnp.mean(y, axis=-1, keepdims=True)
    yc = y - mu
    var = jnp.mean(yc * yc, axis=-1, keepdims=True)
    xn = yc * lax.rsqrt(var + LN_EPS) * lng_ref[...] + lnb_ref[...]
    xnew_ref[...] = xn
    for k, ((shift_ref, scale_ref), h_ref) in enumerate(zip(mod_refs, h_refs)):
        h = (xn * (1.0 + scale_ref[...]) + shift_ref[...]).reshape(g * s, d)
        h_ref[...] = h.astype(h_ref.dtype)
        if with_route and k == 0:
            route_ref[...] = _route(h, rw_refs, rb_refs)


def _ln_mod(x, planes, gate, ln_g, ln_b, mods, router=None, moe_rows=None):
    b, t, d = x.shape
    n = b * t
    c = _geo(b, t)
    g, mg, midx = c["groups"], c["mod_groups"], c["mod_index"]
    rows = g * SUBLANES
    with_route = router is not None
    steps = c["steps"]
    tail_steps = 0
    if with_route and moe_rows[0] is None:
        assert moe_rows[2] == 0 and (moe_rows[1] - n) % rows == 0
        tail_steps = (moe_rows[1] - n) // rows
    cl = lambda i: jnp.minimum(i, steps - 1)
    tok = pl.BlockSpec((g, SUBLANES, d), lambda i: (cl(i), 0, 0))
    per_batch = pl.BlockSpec((mg, 1, d), lambda i: (midx(cl(i)), 0, 0))
    const = pl.BlockSpec((1, 1, d), lambda i: (0, 0, 0))
    in_specs = [tok]
    args = [x.reshape(-1, SUBLANES, d)]
    for arr, row0 in planes:
        assert row0 % rows == 0, (row0, rows)
        in_specs.append(pl.BlockSpec((g, SUBLANES, d), lambda i, off=row0 // rows: (off + cl(i), 0, 0)))
        args.append(arr.reshape(-1, SUBLANES, d))
    in_specs += [per_batch, const, const]
    args += [gate, ln_g.reshape(1, 1, d), ln_b.reshape(1, 1, d)]
    for shift, scale in mods:
        in_specs += [per_batch, per_batch]
        args += [shift, scale]
    out_shape = [jax.ShapeDtypeStruct((n // SUBLANES, SUBLANES, d), F32)]
    out_specs = [tok]
    aliases = {}
    if with_route:
        rws, rbs = router
        in_specs += [pl.BlockSpec(a.shape, lambda i: (0, 0)) for a in list(rws) + list(rbs)]
        args += list(rws) + list(rbs)
        buf, total, row0 = moe_rows
        if buf is None:
            in_specs.append(pl.BlockSpec((1, 128), lambda i: (0, 0)))
            args.append(jnp.zeros((1, 128), F32))
        else:
            in_specs.append(pl.BlockSpec(memory_space=pl.ANY))
            args.append(buf)
            aliases = {len(args) - 1: 1}
        out_shape.append(jax.ShapeDtypeStruct((total, d), F32))
        out_specs.append(pl.BlockSpec((rows, d), lambda i, off=row0 // rows: (off + i, 0)))
        out_shape.append(jax.ShapeDtypeStruct((n, N_GROUPS), F32))
        out_specs.append(pl.BlockSpec((rows, N_GROUPS), lambda i: (cl(i), 0)))
    else:
        for _ in mods:
            out_shape.append(jax.ShapeDtypeStruct((n, d), BF16))
            out_specs.append(pl.BlockSpec((rows, d), lambda i: (i, 0)))
    outs = pl.pallas_call(
        functools.partial(_ln_mod_body, n_planes=len(planes), n_mods=len(mods), with_route=with_route,
                          steps=steps),
        grid=(steps + tail_steps,),
        in_specs=in_specs,
        out_specs=out_specs,
        out_shape=out_shape,
        input_output_aliases=aliases,
        compiler_params=_params(("arbitrary",)),
        name="ln_mod",
    )(*args)
    return [outs[0].reshape(b, t, d)] + list(outs[1:])


def _group_consts():
    lane = np.arange(LANE_GROUP)
    bd = (lane[:, None] // RW_HEAD == lane[None, :] // RW_HEAD).astype(np.float32)
    eye = (np.arange(RW_HEAD)[:, None] == lane[None, :] % RW_HEAD).astype(np.float32)
    return jnp.asarray(bd, BF16), jnp.asarray(eye, F32)


def _head_sum(x, bd):
    hi = x.astype(BF16)
    lo = (x - hi.astype(F32)).astype(BF16)
    return jnp.dot(hi, bd, preferred_element_type=F32) + jnp.dot(lo, bd, preferred_element_type=F32)


def _head_sum_rows(x, bd):
    d = x.shape[-1]
    return jnp.concatenate(
        [_head_sum(x[:, c:c + LANE_GROUP], bd) for c in range(0, d, LANE_GROUP)], axis=-1)


def _scan_body(*refs, has_s0):
    it = iter(refs)
    r_ref, d_ref, k_ref, v_ref, a_ref = (next(it) for _ in range(5))
    kk_ref, ka_ref, bd_ref, eye_ref = next(it), next(it), next(it), next(it)
    s0_ref = next(it) if has_s0 else None
    y_ref, st_ref = next(it), next(it)
    kn_ref, al_ref, km_ref, lhs1_ref, lhs2_ref = (next(it) for _ in range(5))

    nb, tt, d = r_ref.shape
    ng = st_ref.shape[1]
    half = nb * ng * RW_HEAD

    @pl.when(pl.program_id(1) == 0)
    def _():
        if has_s0:
            st_ref[...] = s0_ref[...]
        else:
            st_ref[...] = jnp.zeros_like(st_ref)

    bd = bd_ref[...]
    eye = eye_ref[...]
    k = k_ref[...].reshape(nb * tt, d)
    a = a_ref[...].reshape(nb * tt, d)
    kk = k * kk_ref[...]
    kn = kk * lax.rsqrt(_head_sum_rows(kk * kk, bd) + 1e-12)
    kn_ref[...] = kn.reshape(nb, tt, d)
    al_ref[...] = (kn * a).reshape(nb, tt, d)
    km_ref[...] = (k * (1.0 + (a - 1.0) * ka_ref[...])).reshape(nb, tt, d)

    def step(t, carry):
        def row(ref, n, g):
            return ref[n, pl.ds(t, 1), pl.ds(g * LANE_GROUP, LANE_GROUP)]

        for n in range(nb):
            for g in range(ng):
                q = (n * ng + g) * RW_HEAD
                lhs1_ref[pl.ds(q, RW_HEAD), :] = (st_ref[n, g] * row(kn_ref, n, g)).astype(BF16)
                lhs1_ref[pl.ds(half + q, RW_HEAD), :] = (eye * row(v_ref, n, g)).astype(BF16)
        res1 = jnp.dot(lhs1_ref[...], bd, preferred_element_type=F32)
        for n in range(nb):
            for g in range(ng):
                q = (n * ng + g) * RW_HEAD
                skk = res1[q:q + RW_HEAD]
                vcol = res1[half + q:half + q + RW_HEAD]
                s_new = (st_ref[n, g] * row(d_ref, n, g) - skk * row(al_ref, n, g)
                         + vcol * row(km_ref, n, g))
                st_ref[n, g] = s_new
                lhs2_ref[pl.ds(q, RW_HEAD), :] = (s_new * row(r_ref, n, g)).astype(BF16)
        res2 = jnp.dot(lhs2_ref[...], bd, preferred_element_type=F32)
        for n in range(nb):
            for g in range(ng):
                q = (n * ng + g) * RW_HEAD
                yrow = jnp.sum(res2[q:q + RW_HEAD] * eye, axis=0, keepdims=True)
                y_ref[n, pl.ds(t, 1), pl.ds(g * LANE_GROUP, LANE_GROUP)] = yrow
        return carry

    lax.fori_loop(0, tt, step, 0)


def _rwkv_scan(r, dcy, k, v, a, kk, ka, s0, b, t):
    n, d = r.shape
    ng = d // LANE_GROUP
    bd, eye = _group_consts()
    if t > SCAN_TT:
        nb, tt = b, SCAN_TT
    else:
        nb, tt = min(SCAN_NB, b), t
    tok = pl.BlockSpec((nb, tt, d), lambda i, c: (i, c, 0))
    const2 = lambda shape: pl.BlockSpec(shape, lambda i, c: (0, 0))
    state = pl.BlockSpec((nb, ng, RW_HEAD, LANE_GROUP), lambda i, c: (i, 0, 0, 0))
    in_specs = [tok] * 5 + [const2((1, d)), const2((1, d)), const2(bd.shape), const2(eye.shape)]
    args = [x.reshape(b, t, d) for x in (r, dcy, k, v, a)] + [kk.reshape(1, d), ka.reshape(1, d), bd, eye]
    if s0 is not None:
        in_specs.append(state)
        args.append(s0)
    rows = nb * ng * RW_HEAD
    y, st = pl.pallas_call(
        functools.partial(_scan_body, has_s0=s0 is not None),
        grid=(b // nb, t // tt),
        in_specs=in_specs,
        out_specs=[tok, state],
        out_shape=[jax.ShapeDtypeStruct((b, t, d), F32),
                   jax.ShapeDtypeStruct((b, ng, RW_HEAD, LANE_GROUP), F32)],
        scratch_shapes=[pltpu.VMEM((nb, tt, d), F32)] * 3
        + [pltpu.VMEM((2 * rows, LANE_GROUP), BF16), pltpu.VMEM((rows, LANE_GROUP), BF16)],
        compiler_params=_params(("parallel", "arbitrary")),
        name="rwkv_scan",
    )(*args)
    return y.reshape(n, d), st


def _state_to_groups(s):
    b, h, nv, nk = s.shape
    s = s.reshape(b, h // HEADS_PER_GROUP, HEADS_PER_GROUP, nv, nk)
    return s.transpose(0, 1, 3, 2, 4).reshape(b, h // HEADS_PER_GROUP, nv, HEADS_PER_GROUP * nk)


def _state_from_groups(s):
    b, ng, nv, _ = s.shape
    s = s.reshape(b, ng, nv, HEADS_PER_GROUP, RW_HEAD)
    return s.transpose(0, 1, 3, 2, 4).reshape(b, ng * HEADS_PER_GROUP, nv, RW_HEAD)


def _rwkv_post_body(y_ref, r_ref, k_ref, a_ref, v_ref, g_ref, lnw_ref, lnb_ref, rk_ref, ka_ref,
                    bd_ref, z_ref):
    bd = bd_ref[...]
    y = y_ref[...]
    inv = 1.0 / RW_HEAD
    mean = _head_sum_rows(y, bd) * inv
    yc = y - mean
    var = _head_sum_rows(yc * yc, bd) * inv
    yn = yc * lax.rsqrt(var + RW_LN_EPS) * lnw_ref[...] + lnb_ref[...]
    a = a_ref[...]
    km = k_ref[...] * (1.0 + (a - 1.0) * ka_ref[...])
    bonus = _head_sum_rows(r_ref[...] * km * rk_ref[...], bd) * v_ref[...]
    z_ref[...] = ((yn + bonus) * g_ref[...]).astype(BF16)


def _rwkv_post(y, r, k, a, v, g, lnw, lnb, rk, ka):
    n, d = y.shape
    bd, _ = _group_consts()
    rows = min(EW_ROWS, n)
    tok = pl.BlockSpec((rows, d), lambda i: (i, 0))
    vec = pl.BlockSpec((1, d), lambda i: (0, 0))
    return pl.pallas_call(
        _rwkv_post_body,
        grid=(n // rows,),
        in_specs=[tok] * 6 + [vec] * 4 + [pl.BlockSpec(bd.shape, lambda i: (0, 0))],
        out_specs=tok,
        out_shape=jax.ShapeDtypeStruct((n, d), BF16),
        compiler_params=_params(("parallel",)),
        name="rwkv_post",
    )(y, r, k, a, v, g, lnw.reshape(1, d), lnb.reshape(1, d), rk.reshape(1, d), ka.reshape(1, d), bd)


def _attn_unit(q, k_prev, v_prev, k_cur, v_cur, sink_ref, prev_limit, o_ref, row0):
    tq = q.shape[0]
    kvh = k_cur.shape[-1] // ATT_HEAD_DIM
    m_rows = GQA * tq
    ri = lax.broadcasted_iota(jnp.int32, (m_rows, 1), 0)
    qi = ri % tq
    head_in_group = ri // tq
    jp = lax.broadcasted_iota(jnp.int32, (1, WINDOW), 1)
    jc = lax.broadcasted_iota(jnp.int32, (1, tq), 1)
    diff_p = WINDOW + qi - jp
    diff_c = qi - jc
    valid_p = diff_p < prev_limit
    valid_c = diff_c >= 0
    nt = (((1,), (1,)), ((), ()))
    for kh in range(kvh):
        heads = [kh * GQA + h for h in range(GQA)]
        qs = jnp.concatenate(
            [q[:, h * ATT_HEAD_DIM:(h + 1) * ATT_HEAD_DIM] for h in heads], axis=0).astype(BF16)
        sl = slice(kh * ATT_HEAD_DIM, (kh + 1) * ATT_HEAD_DIM)
        kp, vp = k_prev[:, sl].astype(BF16), v_prev[:, sl].astype(BF16)
        kc, vc = k_cur[:, sl].astype(BF16), v_cur[:, sl].astype(BF16)
        s_p = lax.dot_general(qs, kp, nt, preferred_element_type=F32) * (ATT_HEAD_DIM ** -0.5)
        s_c = lax.dot_general(qs, kc, nt, preferred_element_type=F32) * (ATT_HEAD_DIM ** -0.5)
        hidx = (kh * GQA + head_in_group).astype(F32)
        slope = jnp.exp2(-8.0 * (hidx + 1.0) / (kvh * GQA))
        s_p = jnp.where(valid_p, s_p - slope * diff_p.astype(F32), NEG_INF)
        s_c = jnp.where(valid_c, s_c - slope * diff_c.astype(F32), NEG_INF)
        sink = sink_ref[kh]
        m = jnp.maximum(jnp.maximum(jnp.max(s_p, axis=-1, keepdims=True),
                                    jnp.max(s_c, axis=-1, keepdims=True)), sink)
        p_p = jnp.exp(s_p - m)
        p_c = jnp.exp(s_c - m)
        den = (jnp.sum(p_p, axis=-1, keepdims=True) + jnp.sum(p_c, axis=-1, keepdims=True)
               + jnp.exp(sink - m))
        o = (jnp.dot((p_p / den).astype(BF16), vp, preferred_element_type=F32)
             + jnp.dot((p_c / den).astype(BF16), vc, preferred_element_type=F32))
        for hq, h in enumerate(heads):
            o_ref[pl.ds(row0, tq), pl.ds(h * ATT_HEAD_DIM, ATT_HEAD_DIM)] = o[hq * tq:(hq + 1) * tq]


def _attn_prefill_body(q_ref, kvc_ref, kvp_ref, sink_ref, o_ref):
    kvd = kvc_ref.shape[-1] // 2
    kvc, kvp = kvc_ref[...], kvp_ref[...]
    prev_limit = jnp.where(pl.program_id(1) > 0, WINDOW, 0)
    _attn_unit(q_ref[...], kvp[:, :kvd], kvp[:, kvd:], kvc[:, :kvd], kvc[:, kvd:], sink_ref,
               prev_limit, o_ref, 0)


def _attn_decode_body(q_ref, kvc_ref, kc_ref, vc_ref, sink_ref, o_ref, *, nb, tq):
    kvd = kvc_ref.shape[-1] // 2
    for n in range(nb):
        kvc = kvc_ref[pl.ds(n * tq, tq), :]
        _attn_unit(q_ref[pl.ds(n * tq, tq), :], kc_ref[n], vc_ref[n], kvc[:, :kvd], kvc[:, kvd:],
                   sink_ref, WINDOW, o_ref, n * tq)


def _sink_rows(sinks, kvh, tq):
    return jnp.repeat(sinks.reshape(kvh, GQA), tq, axis=1).reshape(kvh, GQA * tq, 1).astype(F32)


def _attn_prefill(q, kv, sinks, b, t):
    n, dq = q.shape
    kv2 = kv.shape[-1]
    nblk = t // WINDOW
    sink = _sink_rows(sinks, kv2 // 2 // ATT_HEAD_DIM, WINDOW)
    return pl.pallas_call(
        _attn_prefill_body,
        grid=(b, nblk),
        in_specs=[
            pl.BlockSpec((WINDOW, dq), lambda i, j: (i * nblk + j, 0)),
            pl.BlockSpec((WINDOW, kv2), lambda i, j: (i * nblk + j, 0)),
            pl.BlockSpec((WINDOW, kv2), lambda i, j: (i * nblk + jnp.maximum(j - 1, 0), 0)),
            pl.BlockSpec(sink.shape, lambda i, j: (0, 0, 0)),
        ],
        out_specs=pl.BlockSpec((WINDOW, dq), lambda i, j: (i * nblk + j, 0)),
        out_shape=jax.ShapeDtypeStruct((n, dq), F32),
        compiler_params=_params(("parallel", "arbitrary")),
        name="attn_prefill",
    )(q, kv, kv, sink)


def _attn_decode(q, kv, cache_k, cache_v, sinks, b, t):
    n, dq = q.shape
    kv2 = kv.shape[-1]
    kvd = kv2 // 2
    nb = min(ATT_NB, b)
    rows = nb * t
    sink = _sink_rows(sinks, kvd // ATT_HEAD_DIM, t)
    return pl.pallas_call(
        functools.partial(_attn_decode_body, nb=nb, tq=t),
        grid=(b // nb,),
        in_specs=[
            pl.BlockSpec((rows, dq), lambda i: (i, 0)),
            pl.BlockSpec((rows, kv2), lambda i: (i, 0)),
            pl.BlockSpec((nb, WINDOW, kvd), lambda i: (i, 0, 0)),
            pl.BlockSpec((nb, WINDOW, kvd), lambda i: (i, 0, 0)),
            pl.BlockSpec(sink.shape, lambda i: (0, 0, 0)),
        ],
        out_specs=pl.BlockSpec((rows, dq), lambda i: (i, 0)),
        out_shape=jax.ShapeDtypeStruct((n, dq), F32),
        compiler_params=_params(("parallel",)),
        name="attn_decode",
    )(q, kv, cache_k.reshape(b, WINDOW, kvd), cache_v.reshape(b, WINDOW, kvd), sink)


def _moe_body(te_ref, tv_ref, tok_ref, dst_ref, h_hbm, gate_ref, wg_ref, wu_ref, wd_ref, out_hbm,
              xbuf, obuf, gsem, ssem, wg16, wu16, wd16):
    t = pl.program_id(0)
    last = pl.num_programs(0) - 1
    slot = t % 2

    def row_gather(tile, buf, r):
        tok = tok_ref[tile * MOE_TG + r]
        return pltpu.make_async_copy(h_hbm.at[pl.ds(tok, 1)], xbuf.at[buf, pl.ds(r, 1)], gsem.at[buf])

    def row_scatter(tile, r):
        dst = dst_ref[tile * MOE_TG + r]
        return pltpu.make_async_copy(obuf.at[pl.ds(r, 1)], out_hbm.at[pl.ds(dst, 1)], ssem.at[0])

    def for_rows(fn):
        def body(r, carry):
            fn(r)
            return carry
        lax.fori_loop(0, MOE_TG, body, 0, unroll=8)

    @pl.when(t == 0)
    def _():
        obuf[...] = jnp.zeros_like(obuf)
        spare = pltpu.make_async_copy(
            obuf, out_hbm.at[pl.ds(out_hbm.shape[0] - MOE_TG, MOE_TG)], ssem.at[0])
        spare.start()
        spare.wait()

    @pl.when((t == 0) & (tv_ref[0] > 0))
    def _():
        for_rows(lambda r: row_gather(0, 0, r).start())

    nxt = jnp.minimum(t + 1, last)

    @pl.when((t < last) & (tv_ref[nxt] > 0))
    def _():
        for_rows(lambda r: row_gather(nxt, 1 - slot, r).start())

    prev = jnp.maximum(t - 1, 0)

    @pl.when((t == 0) | (te_ref[t] != te_ref[prev]))
    def _():
        wg16[...] = wg_ref[...].astype(BF16)
        wu16[...] = wu_ref[...].astype(BF16)
        wd16[...] = wd_ref[...].astype(BF16)

    @pl.when(tv_ref[t] > 0)
    def _():
        for_rows(lambda r: row_gather(t, slot, r).wait())
        x = xbuf[slot].astype(BF16)
        hg = jnp.dot(x, wg16[...], preferred_element_type=F32)
        hu = jnp.dot(x, wu16[...], preferred_element_type=F32)
        act = (hg * jax.nn.sigmoid(hg)) * hu * gate_ref[...]
        o = jnp.dot(act.astype(BF16), wd16[...], preferred_element_type=F32)

        @pl.when(t > 0)
        def _():
            for_rows(lambda r: row_scatter(prev, r).wait())

        obuf[...] = o
        for_rows(lambda r: row_scatter(t, r).start())

        @pl.when(t == last)
        def _():
            for_rows(lambda r: row_scatter(t, r).wait())

    @pl.when((tv_ref[t] == 0) & (t > 0) & (tv_ref[prev] > 0))
    def _():
        for_rows(lambda r: row_scatter(prev, r).wait())


def _moe_plan(route, n_tiles):
    n = route.shape[0]
    e = route[:, :2].astype(jnp.int32).reshape(-1)
    w = route[:, 2:4].reshape(-1)
    onehot = (e[:, None] == jnp.arange(N_EXPERTS, dtype=jnp.int32)[None, :]).astype(jnp.int32)
    csum = jnp.cumsum(onehot, axis=0)
    rank = jnp.take_along_axis(csum, e[:, None], axis=1)[:, 0] - 1
    counts = csum[-1]
    padded = ((counts + MOE_TG - 1) // MOE_TG) * MOE_TG
    pend = jnp.cumsum(padded)
    dest = (pend - padded)[e] + rank
    p_rows = n_tiles * MOE_TG
    pair = jnp.arange(2 * n, dtype=jnp.int32)
    row_token = jnp.zeros((p_rows,), jnp.int32).at[dest].set(pair // 2)
    spare = 2 * n + jnp.arange(p_rows, dtype=jnp.int32) % MOE_TG
    row_dst = spare.at[dest].set((pair % 2) * n + pair // 2)
    row_gate = jnp.zeros((p_rows,), F32).at[dest].set(w)
    starts = jnp.arange(n_tiles, dtype=jnp.int32) * MOE_TG
    tile_valid = (starts < pend[-1]).astype(jnp.int32)
    tile_expert = jnp.minimum(jnp.searchsorted(pend, starts, side="right"), N_EXPERTS - 1)
    last_valid = jnp.max(jnp.where(tile_valid > 0, tile_expert, 0))
    tile_expert = jnp.where(tile_valid > 0, tile_expert, last_valid).astype(jnp.int32)
    return row_token, row_dst, row_gate, tile_expert, tile_valid


def _moe(h, route, w_gate, w_up, w_down, layer):
    n, d = h.shape
    f = w_gate.shape[-1]
    n_tiles = (2 * n) // MOE_TG + N_EXPERTS
    row_token, row_dst, row_gate, tile_expert, tile_valid = _moe_plan(route, n_tiles)
    return pl.pallas_call(
        _moe_body,
        grid_spec=pltpu.PrefetchScalarGridSpec(
            num_scalar_prefetch=4,
            grid=(n_tiles,),
            in_specs=[
                pl.BlockSpec(memory_space=pl.ANY),
                pl.BlockSpec((MOE_TG, 1), lambda t, te, tv, rt, rd: (t, 0)),
                pl.BlockSpec((None, None, d, f), lambda t, te, tv, rt, rd: (layer, te[t], 0, 0)),
                pl.BlockSpec((None, None, d, f), lambda t, te, tv, rt, rd: (layer, te[t], 0, 0)),
                pl.BlockSpec((None, None, f, d), lambda t, te, tv, rt, rd: (layer, te[t], 0, 0)),
            ],
            out_specs=pl.BlockSpec(memory_space=pl.ANY),
            scratch_shapes=[pltpu.VMEM((2, MOE_TG, d), F32), pltpu.VMEM((MOE_TG, d), F32),
                            pltpu.SemaphoreType.DMA((2,)), pltpu.SemaphoreType.DMA((1,)),
                            pltpu.VMEM((d, f), BF16), pltpu.VMEM((d, f), BF16),
                            pltpu.VMEM((f, d), BF16)],
        ),
        out_shape=jax.ShapeDtypeStruct((2 * n + MOE_TG, d), F32),
        compiler_params=_params(("arbitrary",)),
        name="moe_ffn",
    )(tile_expert, tile_valid, row_token, row_dst, h, row_gate.reshape(-1, 1), w_gate, w_up, w_down)


def _split_mod(mod, n_parts, b_first):
    d = mod.shape[-1] // n_parts
    parts = [mod[:, k * d:(k + 1) * d] for k in range(n_parts)]
    return [[p[:b_first, None, :] for p in parts], [p[b_first:, None, :] for p in parts]]


def kernel(x_prompt, x_sample, c_prompt, c_sample, state_wkv, state_shift, cache_k_win, cache_v_win, ada_w, ada_b, ln_g, ln_b, rw_mu, rw_w_rkv, rw_w0, rw_w1, rw_w2, rw_a0, rw_a1, rw_a2, rw_g1, rw_g2, rw_k_k, rw_k_a, rw_r_k, rw_lnx_w, rw_lnx_b, rw_wo, ada_kv_w, ada_kv_b, w_kv, w_q, attn_sinks, w_o_attn, router_w, router_bias, moe_w_gate, moe_w_up, moe_w_down):
    d = x_prompt.shape[-1]
    xs = [x_prompt, x_sample]
    shapes = [x.shape[:2] for x in xs]
    counts = [b * t for b, t in shapes]
    bp = shapes[0][0]

    c_all = jnp.concatenate([c_prompt, c_sample], axis=0)
    mods = {(l, s): _split_mod(_mm(c_all, ada_w, name="mm_ada", w_lead=(l, s), bias=ada_b[l, s],
                                   in_act="silu"), 3, bp)
            for l in range(DEPTH) for s in range(2)}
    mods_kv = _split_mod(_mm(c_all, ada_kv_w, name="mm_ada_kv", bias=ada_kv_b, in_act="silu"), 2, bp)

    rws = [router_w[:, e::EXPERTS_PER_GROUP] for e in range(EXPERTS_PER_GROUP)]
    rbs = [router_bias[e::EXPERTS_PER_GROUP].reshape(1, N_GROUPS) for e in range(EXPERTS_PER_GROUP)]
    router = (rws, rbs)

    n_all = counts[0] + counts[1]
    row0 = [0, counts[0]]

    def moe_layer(h_all, routes, layer):
        out = _moe(h_all, jnp.concatenate(routes, axis=0), moe_w_gate, moe_w_up, moe_w_down, layer)
        return [[(out, slot * n_all + row0[tr]) for slot in range(2)] for tr in range(2)]

    firsts = [jnp.zeros((bp, 1, d), F32), state_shift[0][:, None, :]]
    s0s = [None, _state_to_groups(state_wkv[0])]
    x1, h1, route1, states, hlasts = [], None, [], [], []
    for tr in range(2):
        b, t = shapes[tr]
        shift, scale, gate = mods[0, 0][tr]
        xmix, hlast = _premix(xs[tr], shift, scale, firsts[tr], rw_mu[0])
        r = _mm(xmix, rw_w_rkv, name="mm_r", x_lead=(0,), w_lead=(0, 0))
        k = _mm(xmix, rw_w_rkv, name="mm_k", x_lead=(2,), w_lead=(0, 1))
        v = _mm(xmix, rw_w_rkv, name="mm_v", x_lead=(3,), w_lead=(0, 2))
        lw = _mm(xmix, rw_w1, name="mm_w1", x_lead=(1,), w_lead=(0,), out_act="tanh", out_dtype=BF16)
        la = _mm(xmix, rw_a1, name="mm_a1", x_lead=(4,), w_lead=(0,), out_dtype=BF16)
        lg = _mm(xmix, rw_g1, name="mm_g1", x_lead=(5,), w_lead=(0,), out_act="sigmoid",
                 out_dtype=BF16)
        decay = _mm(lw, rw_w2, name="mm_w2", w_lead=(0,), bias=rw_w0[0], out_act="decay")
        a = _mm(la, rw_a2, name="mm_a2", w_lead=(0,), bias=rw_a0[0], out_act="sigmoid")
        g = _mm(lg, rw_g2, name="mm_g2", w_lead=(0,))
        y, st = _rwkv_scan(r, decay, k, v, a, rw_k_k[0], rw_k_a[0], s0s[tr], b, t)
        z = _rwkv_post(y, r, k, a, v, g, rw_lnx_w[0], rw_lnx_b[0], rw_r_k[0], rw_k_a[0])
        out = _mm(z, rw_wo, name="mm_wo", w_lead=(0,))
        sh, sc, _ = mods[0, 1][tr]
        xn, h1, route = _ln_mod(xs[tr], [(out, 0)], gate, ln_g[0, 0], ln_b[0, 0], [(sh, sc)], router,
                                (h1, n_all, row0[tr]))
        x1.append(xn), route1.append(route), states.append(st), hlasts.append(hlast)
    moe1 = moe_layer(h1, route1, 0)

    caches = [None, (cache_k_win, cache_v_win)]
    x3, h3, route3, kvs = [], None, [], []
    for tr in range(2):
        b, t = shapes[tr]
        shq, scq, gate_q = mods[1, 0][tr]
        shk, sck = mods_kv[tr]
        x2, hq, hkv = _ln_mod(x1[tr], moe1[tr], mods[0, 1][tr][2], ln_g[0, 1], ln_b[0, 1],
                              [(shq, scq), (shk, sck)])
        q = _mm(hq, w_q, name="mm_q", w_lead=(0,))
        kv = _mm(hkv, w_kv, name="mm_kv")
        if caches[tr] is None:
            o = _attn_prefill(q, kv, attn_sinks[0], b, t)
        else:
            o = _attn_decode(q, kv, caches[tr][0], caches[tr][1], attn_sinks[0], b, t)
        out = _mm(o, w_o_attn, name="mm_o", w_lead=(0,))
        sh, sc, _ = mods[1, 1][tr]
        xn, h3, route = _ln_mod(x2, [(out, 0)], gate_q, ln_g[1, 0], ln_b[1, 0], [(sh, sc)], router,
                                (h3, n_all, row0[tr]))
        x3.append(xn), route3.append(route), kvs.append(kv)
    moe3 = moe_layer(h3, route3, 1)
    ys = [_ln_mod(x3[tr], moe3[tr], mods[1, 1][tr][2], ln_g[1, 1], ln_b[1, 1], [])[0]
          for tr in range(2)]

    kvd = kvs[0].shape[-1] // 2
    kvh = kvd // ATT_HEAD_DIM
    (bp, tp), (bs, ts) = shapes
    kv_p = kvs[0].reshape(bp, tp, 2 * kvd)[:, tp - WINDOW:]
    p_k_win = kv_p[..., :kvd].reshape(bp, WINDOW, kvh, ATT_HEAD_DIM)
    p_v_win = kv_p[..., kvd:].reshape(bp, WINDOW, kvh, ATT_HEAD_DIM)
    kv_s = kvs[1].reshape(bs, ts, 2 * kvd)
    s_k_win = jnp.concatenate(
        [cache_k_win, kv_s[..., :kvd].reshape(bs, ts, kvh, ATT_HEAD_DIM)], axis=1)[:, -WINDOW:]
    s_v_win = jnp.concatenate(
        [cache_v_win, kv_s[..., kvd:].reshape(bs, ts, kvh, ATT_HEAD_DIM)], axis=1)[:, -WINDOW:]
    p_wkv = _state_from_groups(states[0])[None]
    s_wkv = _state_from_groups(states[1])[None]
    p_shift = hlasts[0].reshape(1, bp, d)
    s_shift = hlasts[1].reshape(1, bs, d)
    return (ys[0], ys[1], p_wkv, p_shift, p_k_win, p_v_win, s_wkv, s_shift, s_k_win, s_v_win)
```

```python
import functools

import jax
import jax.numpy as jnp
import numpy as np
from jax import lax
from jax.experimental import pallas as pl
from jax.experimental.pallas import tpu as pltpu

F32 = jnp.float32
BF16 = jnp.bfloat16

RW_HEAD = 64
ATT_HEAD_DIM = 64
GQA = 8
WINDOW = 128
N_EXPERTS = 32
EXPERTS_PER_GROUP = 4
N_GROUPS = N_EXPERTS // EXPERTS_PER_GROUP
DEPTH = 2
ALPHA = (2.0 * DEPTH) ** 0.25
LN_EPS = 1e-5
RW_LN_EPS = 64e-5
NEG_INF = -1e30

SUBLANES = 8
LANE_GROUP = 256
HEADS_PER_GROUP = LANE_GROUP // RW_HEAD
VMEM_LIMIT = 56 * 1024 * 1024

MM_TM = 1024
MM_TN = 512
EW_ROWS = 256
SCAN_TT = 64
SCAN_NB = 4
ATT_NB = 8
MOE_TG = 256


def _params(sem):
    return pltpu.CompilerParams(dimension_semantics=sem, vmem_limit_bytes=VMEM_LIMIT)


def _softplus(z):
    return jnp.maximum(z, 0.0) + jnp.log1p(jnp.exp(-jnp.abs(z)))


def _act(x, kind):
    if kind is None:
        return x
    if kind == "silu":
        return x * jax.nn.sigmoid(x)
    if kind == "tanh":
        return jnp.tanh(x)
    if kind == "sigmoid":
        return jax.nn.sigmoid(x)
    if kind == "decay":
        w = -_softplus(-x) - 0.5
        return jnp.exp(-jnp.exp(w))
    raise ValueError(kind)


def _mm_body(*refs, in_act, out_act, has_bias):
    x_ref, w_ref = refs[0], refs[1]
    b_ref = refs[2] if has_bias else None
    o_ref = refs[2 + has_bias]
    xs_ref = refs[3 + has_bias]

    @pl.when(pl.program_id(1) == 0)
    def _():
        xs_ref[...] = _act(x_ref[...].astype(F32), in_act).astype(BF16)

    acc = jnp.dot(xs_ref[...], w_ref[...].astype(BF16), preferred_element_type=F32)
    if has_bias:
        acc = acc + b_ref[...]
    o_ref[...] = _act(acc, out_act).astype(o_ref.dtype)


def _mm(x, w, *, name, x_lead=(), w_lead=(), bias=None, in_act=None, out_act=None, out_dtype=F32,
        tm=MM_TM, tn=MM_TN):
    m, k = x.shape[-2:]
    n = w.shape[-1]
    tm = min(tm, m)
    tn = min(tn, n)
    assert m % tm == 0 and n % tn == 0, (m, tm, n, tn)
    nx, nw = len(x_lead), len(w_lead)
    in_specs = [
        pl.BlockSpec((None,) * nx + (tm, k), lambda i, j: tuple(x_lead) + (i, 0)),
        pl.BlockSpec((None,) * nw + (k, tn), lambda i, j: tuple(w_lead) + (0, j)),
    ]
    args = [x, w]
    if bias is not None:
        in_specs.append(pl.BlockSpec((1, tn), lambda i, j: (0, j)))
        args.append(bias.reshape(1, n).astype(F32))
    body = functools.partial(_mm_body, in_act=in_act, out_act=out_act, has_bias=bias is not None)
    return pl.pallas_call(
        body,
        grid=(m // tm, n // tn),
        in_specs=in_specs,
        out_specs=pl.BlockSpec((tm, tn), lambda i, j: (i, j)),
        out_shape=jax.ShapeDtypeStruct((m, n), out_dtype),
        scratch_shapes=[pltpu.VMEM((tm, k), BF16)],
        compiler_params=_params(("parallel", "arbitrary")),
        name=name,
    )(*args)


def _geo(b, t):
    if t == SUBLANES:
        g = min(EW_ROWS // SUBLANES, b)
        return dict(groups=g, steps=b // g, mod_groups=g, steps_per_batch=1, mod_index=lambda i: i)
    assert t % EW_ROWS == 0
    spb = t // EW_ROWS
    return dict(groups=EW_ROWS // SUBLANES, steps=b * spb, mod_groups=1, steps_per_batch=spb,
                mod_index=lambda i: i // spb)


def _premix_body(x_ref, xprev_ref, first_ref, shift_ref, scale_ref, mu_ref, xmix_ref, hlast_ref, *,
                 steps_per_batch):
    i = pl.program_id(0)
    g, s, d = x_ref.shape
    gf = first_ref.shape[0]
    scale = scale_ref[...]
    shift = shift_ref[...]
    h3 = x_ref[...] * (1.0 + scale) + shift
    hlast_ref[...] = h3[g - gf:, s - 1:, :]
    h = h3.reshape(g * s, d)
    if steps_per_batch > 1:
        hprev_row = xprev_ref[:, s - 1:, :] * (1.0 + scale) + shift
        first = jnp.where(i % steps_per_batch == 0, first_ref[...], hprev_row)
        period = g * s
    else:
        first = first_ref[...]
        period = s
    first2 = jnp.broadcast_to(first, (g, s, d)).reshape(g * s, d)
    row = lax.broadcasted_iota(jnp.int32, (g * s, 1), 0)
    hp = jnp.where(row % period == 0, first2, pltpu.roll(h, 1, 0))
    xx = hp - h
    for m in range(xmix_ref.shape[0]):
        xmix_ref[m] = (h + xx * mu_ref[m]).astype(BF16)


def _premix(x, shift, scale, first, mu):
    b, t, d = x.shape
    n_mix = mu.shape[0]
    c = _geo(b, t)
    g, mg, midx = c["groups"], c["mod_groups"], c["mod_index"]
    rows = g * SUBLANES
    xg = x.reshape(-1, SUBLANES, d)
    per_batch = pl.BlockSpec((mg, 1, d), lambda i: (midx(i), 0, 0))
    return pl.pallas_call(
        functools.partial(_premix_body, steps_per_batch=c["steps_per_batch"]),
        grid=(c["steps"],),
        in_specs=[
            pl.BlockSpec((g, SUBLANES, d), lambda i: (i, 0, 0)),
            pl.BlockSpec((1, SUBLANES, d), lambda i: (jnp.maximum(i * g - 1, 0), 0, 0)),
            per_batch, per_batch, per_batch,
            pl.BlockSpec((n_mix, 1, d), lambda i: (0, 0, 0)),
        ],
        out_specs=[pl.BlockSpec((n_mix, rows, d), lambda i: (0, i, 0)), per_batch],
        out_shape=[jax.ShapeDtypeStruct((n_mix, b * t, d), BF16),
                   jax.ShapeDtypeStruct((b, 1, d), F32)],
        compiler_params=_params(("arbitrary",)),
        name="premix",
    )(xg, xg, first, shift, scale, mu.reshape(n_mix, 1, d))


def _split_bf16(x):
    hi = x.astype(BF16)
    return hi, (x - hi.astype(F32)).astype(BF16)


def _route(h, rw_ref, rb_ref):
    t = h.shape[0]
    h_hi, h_lo = _split_bf16(h)
    w_hi, w_lo = _split_bf16(rw_ref[...])
    logits = (jnp.dot(h_hi, w_hi, preferred_element_type=F32)
              + jnp.dot(h_lo, w_hi, preferred_element_type=F32)
              + jnp.dot(h_hi, w_lo, preferred_element_type=F32))
    s_all = jax.nn.sigmoid(logits)
    sb_all = s_all + rb_ref[...]
    s = [s_all[:, e * N_GROUPS:(e + 1) * N_GROUPS] for e in range(EXPERTS_PER_GROUP)]
    sb = [sb_all[:, e * N_GROUPS:(e + 1) * N_GROUPS] for e in range(EXPERTS_PER_GROUP)]
    hi01, lo01 = jnp.maximum(sb[0], sb[1]), jnp.minimum(sb[0], sb[1])
    hi23, lo23 = jnp.maximum(sb[2], sb[3]), jnp.minimum(sb[2], sb[3])
    top1 = jnp.maximum(hi01, hi23)
    top2 = jnp.maximum(jnp.minimum(hi01, hi23), jnp.maximum(lo01, lo23))
    gscore = top1 + top2
    lane = lax.broadcasted_iota(jnp.int32, (t, N_GROUPS), 1)
    gmax = jnp.max(gscore, axis=-1, keepdims=True)
    gsel = jnp.min(jnp.where(gscore == gmax, lane, N_GROUPS), axis=-1, keepdims=True)
    in_group = lane == gsel
    masked = [jnp.where(in_group, v, NEG_INF) for v in sb]
    idx = [lane * EXPERTS_PER_GROUP + e for e in range(EXPERTS_PER_GROUP)]

    def pick(vals):
        best = functools.reduce(jnp.maximum, [jnp.max(v, axis=-1, keepdims=True) for v in vals])
        return functools.reduce(jnp.minimum, [
            jnp.min(jnp.where(v == best, ix, N_EXPERTS), axis=-1, keepdims=True)
            for v, ix in zip(vals, idx)])

    def weight(sel):
        return functools.reduce(jnp.add, [
            jnp.sum(jnp.where(ix == sel, v, 0.0), axis=-1, keepdims=True) for v, ix in zip(s, idx)])

    e1 = pick(masked)
    e2 = pick([jnp.where(ix == e1, -jnp.inf, v) for v, ix in zip(masked, idx)])
    w1, w2 = weight(e1), weight(e2)
    wsum = w1 + w2
    return jnp.where(lane == 0, e1.astype(F32),
                     jnp.where(lane == 1, e2.astype(F32),
                               jnp.where(lane == 2, w1 / wsum, jnp.where(lane == 3, w2 / wsum, 0.0))))


def _ln_mod_body(*refs, n_planes, n_mods, with_route, steps):
    if with_route:
        @pl.when(pl.program_id(0) < steps)
        def _():
            _ln_mod_compute(*refs, n_planes=n_planes, n_mods=n_mods, with_route=True)

        @pl.when(pl.program_id(0) >= steps)
        def _():
            h_ref = refs[-2]
            h_ref[...] = jnp.zeros_like(h_ref)
    else:
        _ln_mod_compute(*refs, n_planes=n_planes, n_mods=n_mods, with_route=False)


def _ln_mod_compute(*refs, n_planes, n_mods, with_route):
    it = iter(refs)
    x_ref = next(it)
    o_refs = [next(it) for _ in range(n_planes)]
    gate_ref, lng_ref, lnb_ref = next(it), next(it), next(it)
    mod_refs = [(next(it), next(it)) for _ in range(n_mods)]
    rw_ref = next(it) if with_route else None
    rb_ref = next(it) if with_route else None
    if with_route:
        next(it)
    xnew_ref = next(it)
    h_refs = [next(it) for _ in range(n_mods)]
    route_ref = next(it) if with_route else None

    g, s, d = x_ref.shape

    out = o_refs[0][...]
    for o_ref in o_refs[1:]:
        out = out + o_ref[...]
    y = ALPHA * x_ref[...] + gate_ref[...] * out
    mu = jnp.mean(y, axis=-1, keepdims=True)
    yc = y - mu
    var = jnp.mean(yc * yc, axis=-1, keepdims=True)
    xn = yc * lax.rsqrt(var + LN_EPS) * lng_ref[...] + lnb_ref[...]
    xnew_ref[...] = xn
    for k, ((shift_ref, scale_ref), h_ref) in enumerate(zip(mod_refs, h_refs)):
        h = (xn * (1.0 + scale_ref[...]) + shift_ref[...]).reshape(g * s, d)
        h_ref[...] = h.astype(h_ref.dtype)
        if with_route:
            route_ref[...] = _route(h, rw_ref, rb_ref)


def _ln_mod(x, planes, gate, ln_g, ln_b, mods, router=None, moe_rows=None):
    b, t, d = x.shape
    n = b * t
    c = _geo(b, t)
    g, mg, midx = c["groups"], c["mod_groups"], c["mod_index"]
    rows = g * SUBLANES
    with_route = router is not None
    steps = c["steps"]
    tail_steps = 0
    if with_route and moe_rows[0] is None:
        assert moe_rows[2] == 0 and (moe_rows[1] - n) % rows == 0
        tail_steps = (moe_rows[1] - n) // rows
    cl = lambda i: jnp.minimum(i, steps - 1)
    tok = pl.BlockSpec((g, SUBLANES, d), lambda i: (cl(i), 0, 0))
    per_batch = pl.BlockSpec((mg, 1, d), lambda i: (midx(cl(i)), 0, 0))
    const = pl.BlockSpec((1, 1, d), lambda i: (0, 0, 0))
    in_specs = [tok]
    args = [x.reshape(-1, SUBLANES, d)]
    for arr, row0 in planes:
        assert row0 % rows == 0, (row0, rows)
        in_specs.append(pl.BlockSpec((g, SUBLANES, d), lambda i, off=row0 // rows: (off + cl(i), 0, 0)))
        args.append(arr.reshape(-1, SUBLANES, d))
    in_specs += [per_batch, const, const]
    args += [gate, ln_g.reshape(1, 1, d), ln_b.reshape(1, 1, d)]
    for shift, scale in mods:
        in_specs += [per_batch, per_batch]
        args += [shift, scale]
    out_shape = [jax.ShapeDtypeStruct((n // SUBLANES, SUBLANES, d), F32)]
    out_specs = [tok]
    aliases = {}
    if with_route:
        in_specs += [pl.BlockSpec(a.shape, lambda i: (0, 0)) for a in router]
        args += list(router)
        buf, total, row0 = moe_rows
        if buf is None:
            in_specs.append(pl.BlockSpec((1, 128), lambda i: (0, 0)))
            args.append(jnp.zeros((1, 128), F32))
        else:
            in_specs.append(pl.BlockSpec(memory_space=pl.ANY))
            args.append(buf)
            aliases = {len(args) - 1: 1}
        out_shape.append(jax.ShapeDtypeStruct((total, d), F32))
        out_specs.append(pl.BlockSpec((rows, d), lambda i, off=row0 // rows: (off + i, 0)))
        out_shape.append(jax.ShapeDtypeStruct((n, N_GROUPS), F32))
        out_specs.append(pl.BlockSpec((rows, N_GROUPS), lambda i: (cl(i), 0)))
    else:
        for _ in mods:
            out_shape.append(jax.ShapeDtypeStruct((n, d), BF16))
            out_specs.append(pl.BlockSpec((rows, d), lambda i: (i, 0)))
    outs = pl.pallas_call(
        functools.partial(_ln_mod_body, n_planes=len(planes), n_mods=len(mods), with_route=with_route,
                          steps=steps),
        grid=(steps + tail_steps,),
        in_specs=in_specs,
        out_specs=out_specs,
        out_shape=out_shape,
        input_output_aliases=aliases,
        compiler_params=_params(("arbitrary",)),
        name="ln_mod",
    )(*args)
    return [outs[0].reshape(b, t, d)] + list(outs[1:])


def _group_consts():
    lane = np.arange(LANE_GROUP)
    bd = (lane[:, None] // RW_HEAD == lane[None, :] // RW_HEAD).astype(np.float32)
    eye = (np.arange(RW_HEAD)[:, None] == lane[None, :] % RW_HEAD).astype(np.float32)
    return jnp.asarray(bd, BF16), jnp.asarray(eye, F32)


def _head_sum(x, bd):
    hi = x.astype(BF16)
    lo = (x - hi.astype(F32)).astype(BF16)
    return jnp.dot(hi, bd, preferred_element_type=F32) + jnp.dot(lo, bd, preferred_element_type=F32)


def _head_sum_rows(x, bd):
    d = x.shape[-1]
    return jnp.concatenate(
        [_head_sum(x[:, c:c + LANE_GROUP], bd) for c in range(0, d, LANE_GROUP)], axis=-1)


def _scan_body(*refs, has_s0):
    it = iter(refs)
    r_ref, d_ref, k_ref, v_ref, a_ref = (next(it) for _ in range(5))
    kk_ref, ka_ref, bd_ref, eye_ref = next(it), next(it), next(it), next(it)
    s0_ref = next(it) if has_s0 else None
    y_ref, st_ref = next(it), next(it)
    kn_ref, al_ref, km_ref, yw_ref, vk_ref, lhs_ref = (next(it) for _ in range(6))

    nb, tt, d = r_ref.shape
    ng = st_ref.shape[1]

    @pl.when(pl.program_id(1) == 0)
    def _():
        if has_s0:
            st_ref[...] = s0_ref[...]
        else:
            st_ref[...] = jnp.zeros_like(st_ref)

    bd = bd_ref[...]
    eye = eye_ref[...]
    eye16 = eye.astype(BF16)
    k = k_ref[...].reshape(nb * tt, d)
    a = a_ref[...].reshape(nb * tt, d)
    r = r_ref[...].reshape(nb * tt, d)
    kk = k * kk_ref[...]
    kn = kk * lax.rsqrt(_head_sum_rows(kk * kk, bd) + 1e-12)
    al = kn * a
    km = k * (1.0 + (a - 1.0) * ka_ref[...])
    kn_ref[...] = kn.reshape(nb, tt, d)
    al_ref[...] = al.reshape(nb, tt, d)
    km_ref[...] = km.reshape(nb, tt, d)
    yw = d_ref[...].reshape(nb * tt, d) * r - kn * _head_sum_rows(al * r, bd)
    yw_ref[...] = yw.reshape(nb, tt, d)
    vk_ref[...] = v_ref[...] * _head_sum_rows(km * r, bd).reshape(nb, tt, d)
    packed_rows = 2 * SUBLANES

    def step(t, carry):
        def row(ref, n, g):
            return ref[n, pl.ds(t, 1), pl.ds(g * LANE_GROUP, LANE_GROUP)]

        def row16(ref, n, g):
            one = jnp.broadcast_to(row(ref, n, g), (packed_rows, LANE_GROUP)).astype(BF16)
            return jnp.concatenate([one] * (RW_HEAD // packed_rows), axis=0)

        for n in range(nb):
            for g in range(ng):
                q = (n * ng + g) * 3 * RW_HEAD
                s16 = st_ref[n, g].astype(BF16)
                lhs_ref[pl.ds(q, RW_HEAD), :] = s16 * row16(kn_ref, n, g)
                lhs_ref[pl.ds(q + RW_HEAD, RW_HEAD), :] = s16 * row16(yw_ref, n, g)
                lhs_ref[pl.ds(q + 2 * RW_HEAD, RW_HEAD), :] = eye16 * row16(v_ref, n, g)
        res = jnp.dot(lhs_ref[...], bd, preferred_element_type=F32)
        for n in range(nb):
            for g in range(ng):
                q = (n * ng + g) * 3 * RW_HEAD
                skk = res[q:q + RW_HEAD]
                ysum = res[q + RW_HEAD:q + 2 * RW_HEAD]
                vcol = res[q + 2 * RW_HEAD:q + 3 * RW_HEAD]
                st_ref[n, g] = (st_ref[n, g] * row(d_ref, n, g) - skk * row(al_ref, n, g)
                                + vcol * row(km_ref, n, g))
                y_ref[n, pl.ds(t, 1), pl.ds(g * LANE_GROUP, LANE_GROUP)] = (
                    jnp.sum(ysum * eye, axis=0, keepdims=True) + row(vk_ref, n, g))

        return carry

    lax.fori_loop(0, tt, step, 0)


def _rwkv_scan(r, dcy, k, v, a, kk, ka, s0, b, t):
    n, d = r.shape
    ng = d // LANE_GROUP
    bd, eye = _group_consts()
    if t > SCAN_TT:
        nb, tt = b, SCAN_TT
    else:
        nb, tt = min(SCAN_NB, b), t
    tok = pl.BlockSpec((nb, tt, d), lambda i, c: (i, c, 0))
    const2 = lambda shape: pl.BlockSpec(shape, lambda i, c: (0, 0))
    state = pl.BlockSpec((nb, ng, RW_HEAD, LANE_GROUP), lambda i, c: (i, 0, 0, 0))
    in_specs = [tok] * 5 + [const2((1, d)), const2((1, d)), const2(bd.shape), const2(eye.shape)]
    args = [x.reshape(b, t, d) for x in (r, dcy, k, v, a)] + [kk.reshape(1, d), ka.reshape(1, d), bd, eye]
    if s0 is not None:
        in_specs.append(state)
        args.append(s0)
    rows = nb * ng * RW_HEAD
    y, st = pl.pallas_call(
        functools.partial(_scan_body, has_s0=s0 is not None),
        grid=(b // nb, t // tt),
        in_specs=in_specs,
        out_specs=[tok, state],
        out_shape=[jax.ShapeDtypeStruct((b, t, d), F32),
                   jax.ShapeDtypeStruct((b, ng, RW_HEAD, LANE_GROUP), F32)],
        scratch_shapes=[pltpu.VMEM((nb, tt, d), F32)] * 5 + [pltpu.VMEM((3 * rows, LANE_GROUP), BF16)],
        compiler_params=_params(("parallel", "arbitrary")),
        name="rwkv_scan",
    )(*args)
    return y.reshape(n, d), st


def _state_to_groups(s):
    b, h, nv, nk = s.shape
    s = s.reshape(b, h // HEADS_PER_GROUP, HEADS_PER_GROUP, nv, nk)
    return s.transpose(0, 1, 3, 2, 4).reshape(b, h // HEADS_PER_GROUP, nv, HEADS_PER_GROUP * nk)


def _state_from_groups(s):
    b, ng, nv, _ = s.shape
    s = s.reshape(b, ng, nv, HEADS_PER_GROUP, RW_HEAD)
    return s.transpose(0, 1, 3, 2, 4).reshape(b, ng * HEADS_PER_GROUP, nv, RW_HEAD)


def _rwkv_post_body(y_ref, r_ref, k_ref, a_ref, v_ref, g_ref, lnw_ref, lnb_ref, rk_ref, ka_ref,
                    bd_ref, z_ref):
    bd = bd_ref[...]
    y = y_ref[...]
    inv = 1.0 / RW_HEAD
    mean = _head_sum_rows(y, bd) * inv
    yc = y - mean
    var = _head_sum_rows(yc * yc, bd) * inv
    yn = yc * lax.rsqrt(var + RW_LN_EPS) * lnw_ref[...] + lnb_ref[...]
    a = a_ref[...]
    km = k_ref[...] * (1.0 + (a - 1.0) * ka_ref[...])
    bonus = _head_sum_rows(r_ref[...] * km * rk_ref[...], bd) * v_ref[...]
    z_ref[...] = ((yn + bonus) * g_ref[...]).astype(BF16)


def _rwkv_post(y, r, k, a, v, g, lnw, lnb, rk, ka):
    n, d = y.shape
    bd, _ = _group_consts()
    rows = min(EW_ROWS, n)
    tok = pl.BlockSpec((rows, d), lambda i: (i, 0))
    vec = pl.BlockSpec((1, d), lambda i: (0, 0))
    return pl.pallas_call(
        _rwkv_post_body,
        grid=(n // rows,),
        in_specs=[tok] * 6 + [vec] * 4 + [pl.BlockSpec(bd.shape, lambda i: (0, 0))],
        out_specs=tok,
        out_shape=jax.ShapeDtypeStruct((n, d), BF16),
        compiler_params=_params(("parallel",)),
        name="rwkv_post",
    )(y, r, k, a, v, g, lnw.reshape(1, d), lnb.reshape(1, d), rk.reshape(1, d), ka.reshape(1, d), bd)


def _attn_unit(q, k_prev, v_prev, k_cur, v_cur, sink_ref, prev_limit, o_ref, row0):
    tq = q.shape[0]
    kvh = k_cur.shape[-1] // ATT_HEAD_DIM
    m_rows = GQA * tq
    ri = lax.broadcasted_iota(jnp.int32, (m_rows, 1), 0)
    qi = ri % tq
    head_in_group = ri // tq
    jp = lax.broadcasted_iota(jnp.int32, (1, WINDOW), 1)
    jc = lax.broadcasted_iota(jnp.int32, (1, tq), 1)
    diff_p = WINDOW + qi - jp
    diff_c = qi - jc
    valid_p = diff_p < prev_limit
    valid_c = diff_c >= 0
    nt = (((1,), (1,)), ((), ()))
    for kh in range(kvh):
        heads = [kh * GQA + h for h in range(GQA)]
        qs = jnp.concatenate(
            [q[:, h * ATT_HEAD_DIM:(h + 1) * ATT_HEAD_DIM] for h in heads], axis=0).astype(BF16)
        sl = slice(kh * ATT_HEAD_DIM, (kh + 1) * ATT_HEAD_DIM)
        kp, vp = k_prev[:, sl].astype(BF16), v_prev[:, sl].astype(BF16)
        kc, vc = k_cur[:, sl].astype(BF16), v_cur[:, sl].astype(BF16)
        s_p = lax.dot_general(qs, kp, nt, preferred_element_type=F32) * (ATT_HEAD_DIM ** -0.5)
        s_c = lax.dot_general(qs, kc, nt, preferred_element_type=F32) * (ATT_HEAD_DIM ** -0.5)
        hidx = (kh * GQA + head_in_group).astype(F32)
        slope = jnp.exp2(-8.0 * (hidx + 1.0) / (kvh * GQA))
        s_p = jnp.where(valid_p, s_p - slope * diff_p.astype(F32), NEG_INF)
        s_c = jnp.where(valid_c, s_c - slope * diff_c.astype(F32), NEG_INF)
        sink = sink_ref[kh]
        m = jnp.maximum(jnp.maximum(jnp.max(s_p, axis=-1, keepdims=True),
                                    jnp.max(s_c, axis=-1, keepdims=True)), sink)
        p_p = jnp.exp(s_p - m)
        p_c = jnp.exp(s_c - m)
        den = (jnp.sum(p_p, axis=-1, keepdims=True) + jnp.sum(p_c, axis=-1, keepdims=True)
               + jnp.exp(sink - m))
        o = (jnp.dot((p_p / den).astype(BF16), vp, preferred_element_type=F32)
             + jnp.dot((p_c / den).astype(BF16), vc, preferred_element_type=F32))
        for hq, h in enumerate(heads):
            o_ref[pl.ds(row0, tq), pl.ds(h * ATT_HEAD_DIM, ATT_HEAD_DIM)] = o[hq * tq:(hq + 1) * tq]


def _attn_prefill_body(q_ref, kvc_ref, kvp_ref, sink_ref, o_ref):
    kvd = kvc_ref.shape[-1] // 2
    kvc, kvp = kvc_ref[...], kvp_ref[...]
    prev_limit = jnp.where(pl.program_id(1) > 0, WINDOW, 0)
    _attn_unit(q_ref[...], kvp[:, :kvd], kvp[:, kvd:], kvc[:, :kvd], kvc[:, kvd:], sink_ref,
               prev_limit, o_ref, 0)


def _attn_decode_body(q_ref, kvc_ref, kc_ref, vc_ref, sink_ref, o_ref, *, nb, tq):
    kvd = kvc_ref.shape[-1] // 2
    for n in range(nb):
        kvc = kvc_ref[pl.ds(n * tq, tq), :]
        _attn_unit(q_ref[pl.ds(n * tq, tq), :], kc_ref[n], vc_ref[n], kvc[:, :kvd], kvc[:, kvd:],
                   sink_ref, WINDOW, o_ref, n * tq)


def _sink_rows(sinks, kvh, tq):
    return jnp.repeat(sinks.reshape(kvh, GQA), tq, axis=1).reshape(kvh, GQA * tq, 1).astype(F32)


def _attn_prefill(q, kv, sinks, b, t):
    n, dq = q.shape
    kv2 = kv.shape[-1]
    nblk = t // WINDOW
    sink = _sink_rows(sinks, kv2 // 2 // ATT_HEAD_DIM, WINDOW)
    return pl.pallas_call(
        _attn_prefill_body,
        grid=(b, nblk),
        in_specs=[
            pl.BlockSpec((WINDOW, dq), lambda i, j: (i * nblk + j, 0)),
            pl.BlockSpec((WINDOW, kv2), lambda i, j: (i * nblk + j, 0)),
            pl.BlockSpec((WINDOW, kv2), lambda i, j: (i * nblk + jnp.maximum(j - 1, 0), 0)),
            pl.BlockSpec(sink.shape, lambda i, j: (0, 0, 0)),
        ],
        out_specs=pl.BlockSpec((WINDOW, dq), lambda i, j: (i * nblk + j, 0)),
        out_shape=jax.ShapeDtypeStruct((n, dq), F32),
        compiler_params=_params(("parallel", "arbitrary")),
        name="attn_prefill",
    )(q, kv, kv, sink)


def _attn_decode(q, kv, cache_k, cache_v, sinks, b, t):
    n, dq = q.shape
    kv2 = kv.shape[-1]
    kvd = kv2 // 2
    nb = min(ATT_NB, b)
    rows = nb * t
    sink = _sink_rows(sinks, kvd // ATT_HEAD_DIM, t)
    return pl.pallas_call(
        functools.partial(_attn_decode_body, nb=nb, tq=t),
        grid=(b // nb,),
        in_specs=[
            pl.BlockSpec((rows, dq), lambda i: (i, 0)),
            pl.BlockSpec((rows, kv2), lambda i: (i, 0)),
            pl.BlockSpec((nb, WINDOW, kvd), lambda i: (i, 0, 0)),
            pl.BlockSpec((nb, WINDOW, kvd), lambda i: (i, 0, 0)),
            pl.BlockSpec(sink.shape, lambda i: (0, 0, 0)),
        ],
        out_specs=pl.BlockSpec((rows, dq), lambda i: (i, 0)),
        out_shape=jax.ShapeDtypeStruct((n, dq), F32),
        compiler_params=_params(("parallel",)),
        name="attn_decode",
    )(q, kv, cache_k.reshape(b, WINDOW, kvd), cache_v.reshape(b, WINDOW, kvd), sink)


def _moe_body(te_ref, tv_ref, tok_ref, dst_ref, h_hbm, gate_ref, wg_ref, wu_ref, wd_ref, out_hbm,
              xbuf, obuf, gsem, ssem, wg16, wu16, wd16):
    t = pl.program_id(0)
    last = pl.num_programs(0) - 1
    slot = t % 2

    def row_gather(tile, buf, r):
        tok = tok_ref[tile * MOE_TG + r]
        return pltpu.make_async_copy(h_hbm.at[pl.ds(tok, 1)], xbuf.at[buf, pl.ds(r, 1)], gsem.at[buf])

    def row_scatter(tile, r):
        dst = dst_ref[tile * MOE_TG + r]
        return pltpu.make_async_copy(obuf.at[pl.ds(r, 1)], out_hbm.at[pl.ds(dst, 1)], ssem.at[0])

    def for_rows(fn):
        def body(r, carry):
            fn(r)
            return carry
        lax.fori_loop(0, MOE_TG, body, 0, unroll=8)

    @pl.when(t == 0)
    def _():
        obuf[...] = jnp.zeros_like(obuf)
        spare = pltpu.make_async_copy(
            obuf, out_hbm.at[pl.ds(out_hbm.shape[0] - MOE_TG, MOE_TG)], ssem.at[0])
        spare.start()
        spare.wait()

    @pl.when((t == 0) & (tv_ref[0] > 0))
    def _():
        for_rows(lambda r: row_gather(0, 0, r).start())

    nxt = jnp.minimum(t + 1, last)

    @pl.when((t < last) & (tv_ref[nxt] > 0))
    def _():
        for_rows(lambda r: row_gather(nxt, 1 - slot, r).start())

    prev = jnp.maximum(t - 1, 0)

    @pl.when((t == 0) | (te_ref[t] != te_ref[prev]))
    def _():
        wg16[...] = wg_ref[...].astype(BF16)
        wu16[...] = wu_ref[...].astype(BF16)
        wd16[...] = wd_ref[...].astype(BF16)

    @pl.when(tv_ref[t] > 0)
    def _():
        for_rows(lambda r: row_gather(t, slot, r).wait())
        x = xbuf[slot].astype(BF16)
        hg = jnp.dot(x, wg16[...], preferred_element_type=F32)
        hu = jnp.dot(x, wu16[...], preferred_element_type=F32)
        act = (hg * jax.nn.sigmoid(hg)) * hu * gate_ref[...]
        o = jnp.dot(act.astype(BF16), wd16[...], preferred_element_type=F32)

        @pl.when(t > 0)
        def _():
            for_rows(lambda r: row_scatter(prev, r).wait())

        obuf[...] = o
        for_rows(lambda r: row_scatter(t, r).start())

        @pl.when(t == last)
        def _():
            for_rows(lambda r: row_scatter(t, r).wait())

    @pl.when((tv_ref[t] == 0) & (t > 0) & (tv_ref[prev] > 0))
    def _():
        for_rows(lambda r: row_scatter(prev, r).wait())


def _moe_plan(route, n_tiles):
    n = route.shape[0]
    e = route[:, :2].astype(jnp.int32).reshape(-1)
    w = route[:, 2:4].reshape(-1)
    onehot = (e[:, None] == jnp.arange(N_EXPERTS, dtype=jnp.int32)[None, :]).astype(jnp.int32)
    csum = jnp.cumsum(onehot, axis=0)
    rank = jnp.take_along_axis(csum, e[:, None], axis=1)[:, 0] - 1
    counts = csum[-1]
    padded = ((counts + MOE_TG - 1) // MOE_TG) * MOE_TG
    pend = jnp.cumsum(padded)
    dest = (pend - padded)[e] + rank
    p_rows = n_tiles * MOE_TG
    vals = jnp.stack([jnp.arange(1, 2 * n + 1, dtype=jnp.int32), lax.bitcast_convert_type(w, jnp.int32)],
                     axis=1)
    placed = jnp.zeros((p_rows, 2), jnp.int32).at[dest].set(vals)
    has_pair = placed[:, 0] > 0
    pair = placed[:, 0] - 1
    row_token = jnp.where(has_pair, pair // 2, 0)
    spare = 2 * n + jnp.arange(p_rows, dtype=jnp.int32) % MOE_TG
    row_dst = jnp.where(has_pair, (pair % 2) * n + pair // 2, spare)
    row_gate = jnp.where(has_pair, lax.bitcast_convert_type(placed[:, 1], F32), 0.0)
    starts = jnp.arange(n_tiles, dtype=jnp.int32) * MOE_TG
    tile_valid = (starts < pend[-1]).astype(jnp.int32)
    tile_expert = jnp.minimum(jnp.searchsorted(pend, starts, side="right"), N_EXPERTS - 1)
    last_valid = jnp.max(jnp.where(tile_valid > 0, tile_expert, 0))
    tile_expert = jnp.where(tile_valid > 0, tile_expert, last_valid).astype(jnp.int32)
    return row_token, row_dst, row_gate, tile_expert, tile_valid


def _moe(h, route, w_gate, w_up, w_down, layer):
    n, d = h.shape
    row = (d,)
    f = w_gate.shape[-1]
    n_tiles = (2 * n) // MOE_TG + N_EXPERTS
    row_token, row_dst, row_gate, tile_expert, tile_valid = _moe_plan(route, n_tiles)
    return pl.pallas_call(
        _moe_body,
        grid_spec=pltpu.PrefetchScalarGridSpec(
            num_scalar_prefetch=4,
            grid=(n_tiles,),
            in_specs=[
                pl.BlockSpec(memory_space=pl.ANY),
                pl.BlockSpec((MOE_TG, 1), lambda t, te, tv, rt, rd: (t, 0)),
                pl.BlockSpec((None, None, d, f), lambda t, te, tv, rt, rd: (layer, te[t], 0, 0)),
                pl.BlockSpec((None, None, d, f), lambda t, te, tv, rt, rd: (layer, te[t], 0, 0)),
                pl.BlockSpec((None, None, f, d), lambda t, te, tv, rt, rd: (layer, te[t], 0, 0)),
            ],
            out_specs=pl.BlockSpec(memory_space=pl.ANY),
            scratch_shapes=[pltpu.VMEM((2, MOE_TG) + row, F32), pltpu.VMEM((MOE_TG,) + row, F32),
                            pltpu.SemaphoreType.DMA((2,)), pltpu.SemaphoreType.DMA((1,)),
                            pltpu.VMEM((d, f), BF16), pltpu.VMEM((d, f), BF16),
                            pltpu.VMEM((f, d), BF16)],
        ),
        out_shape=jax.ShapeDtypeStruct((2 * n + MOE_TG,) + row, F32),
        compiler_params=_params(("arbitrary",)),
        name="moe_ffn",
    )(tile_expert, tile_valid, row_token, row_dst, h, row_gate.reshape(-1, 1), w_gate, w_up, w_down)


def _split_mod(mod, n_parts, b_first):
    d = mod.shape[-1] // n_parts
    parts = [mod[:, k * d:(k + 1) * d] for k in range(n_parts)]
    return [[p[:b_first, None, :] for p in parts], [p[b_first:, None, :] for p in parts]]


def kernel(x_prompt, x_sample, c_prompt, c_sample, state_wkv, state_shift, cache_k_win, cache_v_win, ada_w, ada_b, ln_g, ln_b, rw_mu, rw_w_rkv, rw_w0, rw_w1, rw_w2, rw_a0, rw_a1, rw_a2, rw_g1, rw_g2, rw_k_k, rw_k_a, rw_r_k, rw_lnx_w, rw_lnx_b, rw_wo, ada_kv_w, ada_kv_b, w_kv, w_q, attn_sinks, w_o_attn, router_w, router_bias, moe_w_gate, moe_w_up, moe_w_down):
    d = x_prompt.shape[-1]
    xs = [x_prompt, x_sample]
    shapes = [x.shape[:2] for x in xs]
    counts = [b * t for b, t in shapes]
    bp = shapes[0][0]

    c_all = jnp.concatenate([c_prompt, c_sample], axis=0)
    mods = {(l, s): _split_mod(_mm(c_all, ada_w, name="mm_ada", w_lead=(l, s), bias=ada_b[l, s],
                                   in_act="silu"), 3, bp)
            for l in range(DEPTH) for s in range(2)}
    mods_kv = _split_mod(_mm(c_all, ada_kv_w, name="mm_ada_kv", bias=ada_kv_b, in_act="silu"), 2, bp)

    by_slot = lambda x: x.reshape(-1, N_GROUPS, EXPERTS_PER_GROUP).swapaxes(1, 2).reshape(-1, N_EXPERTS)
    router = (by_slot(router_w), by_slot(router_bias))

    n_all = counts[0] + counts[1]
    row0 = [0, counts[0]]

    def moe_layer(h_all, routes, layer):
        out = _moe(h_all, jnp.concatenate(routes, axis=0), moe_w_gate, moe_w_up, moe_w_down, layer)
        return [[(out, slot * n_all + row0[tr]) for slot in range(2)] for tr in range(2)]

    firsts = [jnp.zeros((bp, 1, d), F32), state_shift[0][:, None, :]]
    s0s = [None, _state_to_groups(state_wkv[0])]
    x1, h1, route1, states, hlasts = [], None, [], [], []
    for tr in range(2):
        b, t = shapes[tr]
        shift, scale, gate = mods[0, 0][tr]
        xmix, hlast = _premix(xs[tr], shift, scale, firsts[tr], rw_mu[0])
        r = _mm(xmix, rw_w_rkv, name="mm_r", x_lead=(0,), w_lead=(0, 0))
        k = _mm(xmix, rw_w_rkv, name="mm_k", x_lead=(2,), w_lead=(0, 1))
        v = _mm(xmix, rw_w_rkv, name="mm_v", x_lead=(3,), w_lead=(0, 2))
        lw = _mm(xmix, rw_w1, name="mm_w1", x_lead=(1,), w_lead=(0,), out_act="tanh", out_dtype=BF16)
        la = _mm(xmix, rw_a1, name="mm_a1", x_lead=(4,), w_lead=(0,), out_dtype=BF16)
        lg = _mm(xmix, rw_g1, name="mm_g1", x_lead=(5,), w_lead=(0,), out_act="sigmoid",
                 out_dtype=BF16)
        decay = _mm(lw, rw_w2, name="mm_w2", w_lead=(0,), bias=rw_w0[0], out_act="decay")
        a = _mm(la, rw_a2, name="mm_a2", w_lead=(0,), bias=rw_a0[0], out_act="sigmoid")
        g = _mm(lg, rw_g2, name="mm_g2", w_lead=(0,))
        y, st = _rwkv_scan(r, decay, k, v, a, rw_k_k[0], rw_k_a[0], s0s[tr], b, t)
        z = _rwkv_post(y, r, k, a, v, g, rw_lnx_w[0], rw_lnx_b[0], rw_r_k[0], rw_k_a[0])
        out = _mm(z, rw_wo, name="mm_wo", w_lead=(0,))
        sh, sc, _ = mods[0, 1][tr]
        xn, h1, route = _ln_mod(xs[tr], [(out, 0)], gate, ln_g[0, 0], ln_b[0, 0], [(sh, sc)], router,
                                (h1, n_all, row0[tr]))
        x1.append(xn), route1.append(route), states.append(st), hlasts.append(hlast)
    moe1 = moe_layer(h1, route1, 0)

    caches = [None, (cache_k_win, cache_v_win)]
    x3, h3, route3, kvs = [], None, [], []
    for tr in range(2):
        b, t = shapes[tr]
        shq, scq, gate_q = mods[1, 0][tr]
        shk, sck = mods_kv[tr]
        x2, hq, hkv = _ln_mod(x1[tr], moe1[tr], mods[0, 1][tr][2], ln_g[0, 1], ln_b[0, 1],
                              [(shq, scq), (shk, sck)])
        q = _mm(hq, w_q, name="mm_q", w_lead=(0,))
        kv = _mm(hkv, w_kv, name="mm_kv")
        if caches[tr] is None:
            o = _attn_prefill(q, kv, attn_sinks[0], b, t)
        else:
            o = _attn_decode(q, kv, caches[tr][0], caches[tr][1], attn_sinks[0], b, t)
        out = _mm(o, w_o_attn, name="mm_o", w_lead=(0,))
        sh, sc, _ = mods[1, 1][tr]
        xn, h3, route = _ln_mod(x2, [(out, 0)], gate_q, ln_g[1, 0], ln_b[1, 0], [(sh, sc)], router,
                                (h3, n_all, row0[tr]))
        x3.append(xn), route3.append(route), kvs.append(kv)
    moe3 = moe_layer(h3, route3, 1)
    ys = [_ln_mod(x3[tr], moe3[tr], mods[1, 1][tr][2], ln_g[1, 1], ln_b[1, 1], [])[0]
          for tr in range(2)]

    kvd = kvs[0].shape[-1] // 2
    kvh = kvd // ATT_HEAD_DIM
    (bp, tp), (bs, ts) = shapes
    kv_p = kvs[0].reshape(bp, tp, 2 * kvd)[:, tp - WINDOW:]
    p_k_win = kv_p[..., :kvd].reshape(bp, WINDOW, kvh, ATT_HEAD_DIM)
    p_v_win = kv_p[..., kvd:].reshape(bp, WINDOW, kvh, ATT_HEAD_DIM)
    kv_s = kvs[1].reshape(bs, ts, 2 * kvd)
    s_k_win = jnp.concatenate(
        [cache_k_win, kv_s[..., :kvd].reshape(bs, ts, kvh, ATT_HEAD_DIM)], axis=1)[:, -WINDOW:]
    s_v_win = jnp.concatenate(
        [cache_v_win, kv_s[..., kvd:].reshape(bs, ts, kvh, ATT_HEAD_DIM)], axis=1)[:, -WINDOW:]
    p_wkv = _state_from_groups(states[0])[None]
    s_wkv = _state_from_groups(states[1])[None]
    p_shift = hlasts[0].reshape(1, bp, d)
    s_shift = hlasts[1].reshape(1, bs, d)
    return (ys[0], ys[1], p_wkv, p_shift, p_k_win, p_v_win, s_wkv, s_shift, s_k_win, s_v_win)
```

```python
import functools

import jax
import jax.numpy as jnp
import numpy as np
from jax import lax
from jax.experimental import pallas as pl
from jax.experimental.pallas import tpu as pltpu

F32 = jnp.float32
BF16 = jnp.bfloat16

RW_HEAD = 64
ATT_HEAD_DIM = 64
GQA = 8
WINDOW = 128
N_EXPERTS = 32
EXPERTS_PER_GROUP = 4
N_GROUPS = N_EXPERTS // EXPERTS_PER_GROUP
DEPTH = 2
ALPHA = (2.0 * DEPTH) ** 0.25
LN_EPS = 1e-5
RW_LN_EPS = 64e-5
NEG_INF = -1e30
MIX_RKV = (0, 2, 3)
MIX_W, MIX_A, MIX_G = 1, 4, 5

SUBLANES = 8
LANE_GROUP = 256
HEADS_PER_GROUP = LANE_GROUP // RW_HEAD
VMEM_LIMIT = 56 * 1024 * 1024

MM_TM = 1024
MM_TM_BF16 = 2048
MM_TN = 512
EW_ROWS = 256
SCAN_TT = 64
SCAN_NB = 4
ATT_NB = 8
MOE_TG = 256


def _params(sem):
    return pltpu.CompilerParams(dimension_semantics=sem, vmem_limit_bytes=VMEM_LIMIT)


def _softplus(z):
    return jnp.maximum(z, 0.0) + jnp.log1p(jnp.exp(-jnp.abs(z)))


def _act(x, kind):
    if kind is None:
        return x
    if kind == "silu":
        return x * jax.nn.sigmoid(x)
    if kind == "tanh":
        return jnp.tanh(x)
    if kind == "sigmoid":
        return jax.nn.sigmoid(x)
    if kind == "decay":
        w = -_softplus(-x) - 0.5
        return jnp.exp(-jnp.exp(w))
    raise ValueError(kind)


def _mm_body(*refs, in_act, out_act, has_bias, stage_x):
    x_ref, w_ref = refs[0], refs[1]
    b_ref = refs[2] if has_bias else None
    o_ref = refs[2 + has_bias]
    if stage_x:
        xs_ref = refs[3 + has_bias]

        @pl.when(pl.program_id(1) == 0)
        def _():
            xs_ref[...] = _act(x_ref[...].astype(F32), in_act).astype(BF16)
    else:
        xs_ref = x_ref

    acc = jnp.dot(xs_ref[...], w_ref[...].astype(BF16), preferred_element_type=F32)
    if has_bias:
        acc = acc + b_ref[...]
    o_ref[...] = _act(acc, out_act).astype(o_ref.dtype)


def _mm(x, w, *, name, x_lead=(), w_lead=(), bias=None, in_act=None, out_act=None, out_dtype=F32,
        tm=MM_TM, tn=MM_TN):
    m, k = x.shape[-2:]
    n = w.shape[-1]
    tm = min(tm, m)
    tn = min(tn, n)
    assert m % tm == 0 and n % tn == 0, (m, tm, n, tn)
    nx, nw = len(x_lead), len(w_lead)
    in_specs = [
        pl.BlockSpec((None,) * nx + (tm, k), lambda i, j: tuple(x_lead) + (i, 0)),
        pl.BlockSpec((None,) * nw + (k, tn), lambda i, j: tuple(w_lead) + (0, j)),
    ]
    args = [x, w]
    if bias is not None:
        in_specs.append(pl.BlockSpec((1, tn), lambda i, j: (0, j)))
        args.append(bias.reshape(1, n).astype(F32))
    stage_x = x.dtype != BF16 or in_act is not None
    body = functools.partial(_mm_body, in_act=in_act, out_act=out_act, has_bias=bias is not None,
                             stage_x=stage_x)
    return pl.pallas_call(
        body,
        grid=(m // tm, n // tn),
        in_specs=in_specs,
        out_specs=pl.BlockSpec((tm, tn), lambda i, j: (i, j)),
        out_shape=jax.ShapeDtypeStruct((m, n), out_dtype),
        scratch_shapes=[pltpu.VMEM((tm, k), BF16)] if stage_x else [],
        compiler_params=_params(("parallel", "arbitrary")),
        name=name,
    )(*args)


def _geo(b, t):
    if t == SUBLANES:
        g = min(EW_ROWS // SUBLANES, b)
        return dict(groups=g, steps=b // g, mod_groups=g, steps_per_batch=1, mod_index=lambda i: i)
    assert t % EW_ROWS == 0
    spb = t // EW_ROWS
    return dict(groups=EW_ROWS // SUBLANES, steps=b * spb, mod_groups=1, steps_per_batch=spb,
                mod_index=lambda i: i // spb)


def _premix_body(x_ref, xprev_ref, first_ref, shift_ref, scale_ref, mu_ref, w1_ref, a1_ref, g1_ref,
                 xmix_ref, hlast_ref, lw_ref, la_ref, lg_ref, *, steps_per_batch):
    i = pl.program_id(0)
    g, s, d = x_ref.shape
    gf = first_ref.shape[0]
    scale = scale_ref[...]
    shift = shift_ref[...]
    h3 = x_ref[...] * (1.0 + scale) + shift
    hlast_ref[...] = h3[g - gf:, s - 1:, :]
    h = h3.reshape(g * s, d)
    if steps_per_batch > 1:
        hprev_row = xprev_ref[:, s - 1:, :] * (1.0 + scale) + shift
        first = jnp.where(i % steps_per_batch == 0, first_ref[...], hprev_row)
        period = g * s
    else:
        first = first_ref[...]
        period = s
    first2 = jnp.broadcast_to(first, (g, s, d)).reshape(g * s, d)
    row = lax.broadcasted_iota(jnp.int32, (g * s, 1), 0)
    hp = jnp.where(row % period == 0, first2, pltpu.roll(h, 1, 0))
    xx = hp - h
    mix = lambda m: (h + xx * mu_ref[m]).astype(BF16)
    for slot, m in enumerate(MIX_RKV):
        xmix_ref[slot] = mix(m)
    lora = lambda m, w_ref: jnp.dot(mix(m), w_ref[...].astype(BF16), preferred_element_type=F32)
    lw_ref[...] = jnp.tanh(lora(MIX_W, w1_ref))
    la_ref[...] = lora(MIX_A, a1_ref)
    lg_ref[...] = jax.nn.sigmoid(lora(MIX_G, g1_ref))


def _premix(x, shift, scale, first, mu, w1, a1, g1):
    b, t, d = x.shape
    n = b * t
    n_mix = mu.shape[0]
    c = _geo(b, t)
    g, mg, midx = c["groups"], c["mod_groups"], c["mod_index"]
    rows = g * SUBLANES
    xg = x.reshape(-1, SUBLANES, d)
    per_batch = pl.BlockSpec((mg, 1, d), lambda i: (midx(i), 0, 0))
    whole = lambda a: pl.BlockSpec(a.shape, lambda i: (0,) * a.ndim)
    lora_out = lambda w: (pl.BlockSpec((rows, w.shape[-1]), lambda i: (i, 0)),
                          jax.ShapeDtypeStruct((n, w.shape[-1]), F32))
    lora_specs, lora_shapes = zip(lora_out(w1), lora_out(a1), lora_out(g1))
    return pl.pallas_call(
        functools.partial(_premix_body, steps_per_batch=c["steps_per_batch"]),
        grid=(c["steps"],),
        in_specs=[
            pl.BlockSpec((g, SUBLANES, d), lambda i: (i, 0, 0)),
            pl.BlockSpec((1, SUBLANES, d), lambda i: (jnp.maximum(i * g - 1, 0), 0, 0)),
            per_batch, per_batch, per_batch,
            pl.BlockSpec((n_mix, 1, d), lambda i: (0, 0, 0)),
            whole(w1), whole(a1), whole(g1),
        ],
        out_specs=[pl.BlockSpec((len(MIX_RKV), rows, d), lambda i: (0, i, 0)), per_batch,
                   *lora_specs],
        out_shape=[jax.ShapeDtypeStruct((len(MIX_RKV), n, d), BF16),
                   jax.ShapeDtypeStruct((b, 1, d), F32), *lora_shapes],
        compiler_params=_params(("arbitrary",)),
        name="premix",
    )(xg, xg, first, shift, scale, mu.reshape(n_mix, 1, d), w1, a1, g1)


def _split_bf16(x):
    hi = x.astype(BF16)
    return hi, (x - hi.astype(F32)).astype(BF16)


def _route(h, rw_ref, rb_ref):
    t = h.shape[0]
    h_hi, h_lo = _split_bf16(h)
    w_hi, w_lo = _split_bf16(rw_ref[...])
    logits = (jnp.dot(h_hi, w_hi, preferred_element_type=F32)
              + jnp.dot(h_lo, w_hi, preferred_element_type=F32)
              + jnp.dot(h_hi, w_lo, preferred_element_type=F32))
    s_all = jax.nn.sigmoid(logits)
    sb_all = s_all + rb_ref[...]
    s = [s_all[:, e * N_GROUPS:(e + 1) * N_GROUPS] for e in range(EXPERTS_PER_GROUP)]
    sb = [sb_all[:, e * N_GROUPS:(e + 1) * N_GROUPS] for e in range(EXPERTS_PER_GROUP)]
    hi01, lo01 = jnp.maximum(sb[0], sb[1]), jnp.minimum(sb[0], sb[1])
    hi23, lo23 = jnp.maximum(sb[2], sb[3]), jnp.minimum(sb[2], sb[3])
    top1 = jnp.maximum(hi01, hi23)
    top2 = jnp.maximum(jnp.minimum(hi01, hi23), jnp.maximum(lo01, lo23))
    gscore = top1 + top2
    lane = lax.broadcasted_iota(jnp.int32, (t, N_GROUPS), 1)
    gmax = jnp.max(gscore, axis=-1, keepdims=True)
    gsel = jnp.min(jnp.where(gscore == gmax, lane, N_GROUPS), axis=-1, keepdims=True)
    in_group = lane == gsel
    masked = [jnp.where(in_group, v, NEG_INF) for v in sb]
    idx = [lane * EXPERTS_PER_GROUP + e for e in range(EXPERTS_PER_GROUP)]

    def pick(vals):
        best = functools.reduce(jnp.maximum, [jnp.max(v, axis=-1, keepdims=True) for v in vals])
        return functools.reduce(jnp.minimum, [
            jnp.min(jnp.where(v == best, ix, N_EXPERTS), axis=-1, keepdims=True)
            for v, ix in zip(vals, idx)])

    def weight(sel):
        return functools.reduce(jnp.add, [
            jnp.sum(jnp.where(ix == sel, v, 0.0), axis=-1, keepdims=True) for v, ix in zip(s, idx)])

    e1 = pick(masked)
    e2 = pick([jnp.where(ix == e1, -jnp.inf, v) for v, ix in zip(masked, idx)])
    w1, w2 = weight(e1), weight(e2)
    wsum = w1 + w2
    return jnp.where(lane == 0, e1.astype(F32),
                     jnp.where(lane == 1, e2.astype(F32),
                               jnp.where(lane == 2, w1 / wsum, jnp.where(lane == 3, w2 / wsum, 0.0))))


def _ln_mod_body(*refs, n_planes, n_mods, with_route, steps):
    if with_route:
        @pl.when(pl.program_id(0) < steps)
        def _():
            _ln_mod_compute(*refs, n_planes=n_planes, n_mods=n_mods, with_route=True)

        @pl.when(pl.program_id(0) >= steps)
        def _():
            h_ref = refs[-2]
            h_ref[...] = jnp.zeros_like(h_ref)
    else:
        _ln_mod_compute(*refs, n_planes=n_planes, n_mods=n_mods, with_route=False)


def _ln_mod_compute(*refs, n_planes, n_mods, with_route):
    it = iter(refs)
    x_ref = next(it)
    o_refs = [next(it) for _ in range(n_planes)]
    gate_ref, lng_ref, lnb_ref = next(it), next(it), next(it)
    mod_refs = [(next(it), next(it)) for _ in range(n_mods)]
    rw_ref = next(it) if with_route else None
    rb_ref = next(it) if with_route else None
    if with_route:
        next(it)
    xnew_ref = next(it)
    h_refs = [next(it) for _ in range(n_mods)]
    route_ref = next(it) if with_route else None

    g, s, d = x_ref.shape

    out = o_refs[0][...]
    for o_ref in o_refs[1:]:
        out = out + o_ref[...]
    y = ALPHA * x_ref[...] + gate_ref[...] * out
    mu = jnp.mean(y, axis=-1, keepdims=True)
    yc = y - mu
    var = jnp.mean(yc * yc, axis=-1, keepdims=True)
    xn = yc * lax.rsqrt(var + LN_EPS) * lng_ref[...] + lnb_ref[...]
    xnew_ref[...] = xn
    for k, ((shift_ref, scale_ref), h_ref) in enumerate(zip(mod_refs, h_refs)):
        h = (xn * (1.0 + scale_ref[...]) + shift_ref[...]).reshape(g * s, d)
        h_ref[...] = h.astype(h_ref.dtype)
        if with_route:
            route_ref[...] = _route(h, rw_ref, rb_ref)


def _ln_mod(x, planes, gate, ln_g, ln_b, mods, router=None, moe_rows=None):
    b, t, d = x.shape
    n = b * t
    c = _geo(b, t)
    g, mg, midx = c["groups"], c["mod_groups"], c["mod_index"]
    rows = g * SUBLANES
    with_route = router is not None
    steps = c["steps"]
    tail_steps = 0
    if with_route and moe_rows[0] is None:
        assert moe_rows[2] == 0 and (moe_rows[1] - n) % rows == 0
        tail_steps = (moe_rows[1] - n) // rows
    cl = lambda i: jnp.minimum(i, steps - 1)
    tok = pl.BlockSpec((g, SUBLANES, d), lambda i: (cl(i), 0, 0))
    per_batch = pl.BlockSpec((mg, 1, d), lambda i: (midx(cl(i)), 0, 0))
    const = pl.BlockSpec((1, 1, d), lambda i: (0, 0, 0))
    in_specs = [tok]
    args = [x.reshape(-1, SUBLANES, d)]
    for arr, row0 in planes:
        assert row0 % rows == 0, (row0, rows)
        in_specs.append(pl.BlockSpec((g, SUBLANES, d), lambda i, off=row0 // rows: (off + cl(i), 0, 0)))
        args.append(arr.reshape(-1, SUBLANES, d))
    in_specs += [per_batch, const, const]
    args += [gate, ln_g.reshape(1, 1, d), ln_b.reshape(1, 1, d)]
    for shift, scale in mods:
        in_specs += [per_batch, per_batch]
        args += [shift, scale]
    out_shape = [jax.ShapeDtypeStruct((n // SUBLANES, SUBLANES, d), F32)]
    out_specs = [tok]
    aliases = {}
    if with_route:
        in_specs += [pl.BlockSpec(a.shape, lambda i: (0, 0)) for a in router]
        args += list(router)
        buf, total, row0 = moe_rows
        if buf is None:
            in_specs.append(pl.BlockSpec((1, 128), lambda i: (0, 0)))
            args.append(jnp.zeros((1, 128), F32))
        else:
            in_specs.append(pl.BlockSpec(memory_space=pl.ANY))
            args.append(buf)
            aliases = {len(args) - 1: 1}
        out_shape.append(jax.ShapeDtypeStruct((total, d), F32))
        out_specs.append(pl.BlockSpec((rows, d), lambda i, off=row0 // rows: (off + i, 0)))
        out_shape.append(jax.ShapeDtypeStruct((n, N_GROUPS), F32))
        out_specs.append(pl.BlockSpec((rows, N_GROUPS), lambda i: (cl(i), 0)))
    else:
        for _ in mods:
            out_shape.append(jax.ShapeDtypeStruct((n, d), BF16))
            out_specs.append(pl.BlockSpec((rows, d), lambda i: (i, 0)))
    outs = pl.pallas_call(
        functools.partial(_ln_mod_body, n_planes=len(planes), n_mods=len(mods), with_route=with_route,
                          steps=steps),
        grid=(steps + tail_steps,),
        in_specs=in_specs,
        out_specs=out_specs,
        out_shape=out_shape,
        input_output_aliases=aliases,
        compiler_params=_params(("arbitrary",)),
        name="ln_mod",
    )(*args)
    return [outs[0].reshape(b, t, d)] + list(outs[1:])


def _group_consts():
    lane = np.arange(LANE_GROUP)
    bd = (lane[:, None] // RW_HEAD == lane[None, :] // RW_HEAD).astype(np.float32)
    eye = (np.arange(RW_HEAD)[:, None] == lane[None, :] % RW_HEAD).astype(np.float32)
    return jnp.asarray(bd, BF16), jnp.asarray(eye, F32)


def _head_sum(x, bd):
    hi = x.astype(BF16)
    lo = (x - hi.astype(F32)).astype(BF16)
    return jnp.dot(hi, bd, preferred_element_type=F32) + jnp.dot(lo, bd, preferred_element_type=F32)


def _head_sum_rows(x, bd):
    d = x.shape[-1]
    return jnp.concatenate(
        [_head_sum(x[:, c:c + LANE_GROUP], bd) for c in range(0, d, LANE_GROUP)], axis=-1)


def _decay_and_rate(lw, la, w2_ref, w0_ref, a2_ref, a0_ref):
    second = lambda x, w_ref: jnp.dot(x.astype(BF16), w_ref[...].astype(BF16), preferred_element_type=F32)
    return (_act(second(lw, w2_ref) + w0_ref[...], "decay"),
            jax.nn.sigmoid(second(la, a2_ref) + a0_ref[...]))


def _scan_body(*refs, has_s0):
    it = iter(refs)
    r_ref, k_ref, v_ref, lw_ref, la_ref = (next(it) for _ in range(5))
    w2_ref, w0_ref, a2_ref, a0_ref = (next(it) for _ in range(4))
    kk_ref, ka_ref, bd_ref, eye_ref = next(it), next(it), next(it), next(it)
    s0_ref = next(it) if has_s0 else None
    y_ref, st_ref = next(it), next(it)
    d_ref, kn_ref, al_ref, km_ref, yw_ref, vk_ref, lhs_ref = (next(it) for _ in range(7))

    nb, tt, d = r_ref.shape
    ng = st_ref.shape[1]

    @pl.when(pl.program_id(1) == 0)
    def _():
        if has_s0:
            st_ref[...] = s0_ref[...]
        else:
            st_ref[...] = jnp.zeros_like(st_ref)

    bd = bd_ref[...]
    eye = eye_ref[...]
    eye16 = eye.astype(BF16)
    k = k_ref[...].reshape(nb * tt, d)
    r = r_ref[...].reshape(nb * tt, d)
    dcy, a = _decay_and_rate(lw_ref[...].reshape(nb * tt, -1), la_ref[...].reshape(nb * tt, -1),
                             w2_ref, w0_ref, a2_ref, a0_ref)
    d_ref[...] = dcy.reshape(nb, tt, d)
    kk = k * kk_ref[...]
    kn = kk * lax.rsqrt(_head_sum_rows(kk * kk, bd) + 1e-12)
    al = kn * a
    km = k * (1.0 + (a - 1.0) * ka_ref[...])
    kn_ref[...] = kn.reshape(nb, tt, d)
    al_ref[...] = al.reshape(nb, tt, d)
    km_ref[...] = km.reshape(nb, tt, d)
    yw = dcy * r - kn * _head_sum_rows(al * r, bd)
    yw_ref[...] = yw.reshape(nb, tt, d)
    vk_ref[...] = v_ref[...] * _head_sum_rows(km * r, bd).reshape(nb, tt, d)
    packed_rows = 2 * SUBLANES

    def step(t, carry):
        def row(ref, n, g):
            return ref[n, pl.ds(t, 1), pl.ds(g * LANE_GROUP, LANE_GROUP)]

        def row16(ref, n, g):
            one = jnp.broadcast_to(row(ref, n, g), (packed_rows, LANE_GROUP)).astype(BF16)
            return jnp.concatenate([one] * (RW_HEAD // packed_rows), axis=0)

        for n in range(nb):
            for g in range(ng):
                q = (n * ng + g) * 3 * RW_HEAD
                s16 = st_ref[n, g].astype(BF16)
                lhs_ref[pl.ds(q, RW_HEAD), :] = s16 * row16(kn_ref, n, g)
                lhs_ref[pl.ds(q + RW_HEAD, RW_HEAD), :] = s16 * row16(yw_ref, n, g)
                lhs_ref[pl.ds(q + 2 * RW_HEAD, RW_HEAD), :] = eye16 * row16(v_ref, n, g)
        res = jnp.dot(lhs_ref[...], bd, preferred_element_type=F32)
        for n in range(nb):
            for g in range(ng):
                q = (n * ng + g) * 3 * RW_HEAD
                skk = res[q:q + RW_HEAD]
                ysum = res[q + RW_HEAD:q + 2 * RW_HEAD]
                vcol = res[q + 2 * RW_HEAD:q + 3 * RW_HEAD]
                st_ref[n, g] = (st_ref[n, g] * row(d_ref, n, g) - skk * row(al_ref, n, g)
                                + vcol * row(km_ref, n, g))
                y_ref[n, pl.ds(t, 1), pl.ds(g * LANE_GROUP, LANE_GROUP)] = (
                    jnp.sum(ysum * eye, axis=0, keepdims=True) + row(vk_ref, n, g))

        return carry

    lax.fori_loop(0, tt, step, 0)


def _rwkv_scan(r, k, v, lw, la, w2, w0, a2, a0, kk, ka, s0, b, t):
    n, d = r.shape
    ng = d // LANE_GROUP
    bd, eye = _group_consts()
    if t > SCAN_TT:
        nb, tt = b, SCAN_TT
    else:
        nb, tt = min(SCAN_NB, b), t
    tok = pl.BlockSpec((nb, tt, d), lambda i, c: (i, c, 0))
    low = lambda x: pl.BlockSpec((nb, tt, x.shape[-1]), lambda i, c: (i, c, 0))
    const2 = lambda shape: pl.BlockSpec(shape, lambda i, c: (0, 0))
    state = pl.BlockSpec((nb, ng, RW_HEAD, LANE_GROUP), lambda i, c: (i, 0, 0, 0))
    in_specs = [tok] * 3 + [low(lw), low(la), const2(w2.shape), const2((1, d)), const2(a2.shape),
                            const2((1, d)), const2((1, d)), const2((1, d)), const2(bd.shape),
                            const2(eye.shape)]
    args = [x.reshape(b, t, x.shape[-1]) for x in (r, k, v, lw, la)]
    args += [w2, w0.reshape(1, d), a2, a0.reshape(1, d), kk.reshape(1, d), ka.reshape(1, d), bd, eye]
    if s0 is not None:
        in_specs.append(state)
        args.append(s0)
    rows = nb * ng * RW_HEAD
    y, st = pl.pallas_call(
        functools.partial(_scan_body, has_s0=s0 is not None),
        grid=(b // nb, t // tt),
        in_specs=in_specs,
        out_specs=[tok, state],
        out_shape=[jax.ShapeDtypeStruct((b, t, d), F32),
                   jax.ShapeDtypeStruct((b, ng, RW_HEAD, LANE_GROUP), F32)],
        scratch_shapes=[pltpu.VMEM((nb, tt, d), F32)] * 6 + [pltpu.VMEM((3 * rows, LANE_GROUP), BF16)],
        compiler_params=_params(("parallel", "arbitrary")),
        name="rwkv_scan",
    )(*args)
    return y.reshape(n, d), st


def _state_to_groups(s):
    b, h, nv, nk = s.shape
    s = s.reshape(b, h // HEADS_PER_GROUP, HEADS_PER_GROUP, nv, nk)
    return s.transpose(0, 1, 3, 2, 4).reshape(b, h // HEADS_PER_GROUP, nv, HEADS_PER_GROUP * nk)


def _state_from_groups(s):
    b, ng, nv, _ = s.shape
    s = s.reshape(b, ng, nv, HEADS_PER_GROUP, RW_HEAD)
    return s.transpose(0, 1, 3, 2, 4).reshape(b, ng * HEADS_PER_GROUP, nv, RW_HEAD)


def _rwkv_post_body(y_ref, r_ref, k_ref, v_ref, la_ref, lg_ref, a2_ref, a0_ref, g2_ref, lnw_ref,
                    lnb_ref, rk_ref, ka_ref, bd_ref, z_ref):
    bd = bd_ref[...]
    y = y_ref[...]
    inv = 1.0 / RW_HEAD
    mean = _head_sum_rows(y, bd) * inv
    yc = y - mean
    var = _head_sum_rows(yc * yc, bd) * inv
    yn = yc * lax.rsqrt(var + RW_LN_EPS) * lnw_ref[...] + lnb_ref[...]
    second = lambda x, w_ref: jnp.dot(x.astype(BF16), w_ref[...].astype(BF16), preferred_element_type=F32)
    a = jax.nn.sigmoid(second(la_ref[...], a2_ref) + a0_ref[...])
    gate = second(lg_ref[...], g2_ref)
    km = k_ref[...] * (1.0 + (a - 1.0) * ka_ref[...])
    bonus = _head_sum_rows(r_ref[...] * km * rk_ref[...], bd) * v_ref[...]
    z_ref[...] = ((yn + bonus) * gate).astype(BF16)


def _rwkv_post(y, r, k, v, la, lg, a2, a0, g2, lnw, lnb, rk, ka):
    n, d = y.shape
    bd, _ = _group_consts()
    rows = min(EW_ROWS, n)
    tok = pl.BlockSpec((rows, d), lambda i: (i, 0))
    low = lambda x: pl.BlockSpec((rows, x.shape[-1]), lambda i: (i, 0))
    vec = pl.BlockSpec((1, d), lambda i: (0, 0))
    whole = lambda x: pl.BlockSpec(x.shape, lambda i: (0, 0))
    return pl.pallas_call(
        _rwkv_post_body,
        grid=(n // rows,),
        in_specs=[tok] * 4 + [low(la), low(lg), whole(a2), vec, whole(g2)] + [vec] * 4 + [whole(bd)],
        out_specs=tok,
        out_shape=jax.ShapeDtypeStruct((n, d), BF16),
        compiler_params=_params(("parallel",)),
        name="rwkv_post",
    )(y, r, k, v, la, lg, a2, a0.reshape(1, d), g2, lnw.reshape(1, d), lnb.reshape(1, d),
      rk.reshape(1, d), ka.reshape(1, d), bd)


def _attn_unit(q, k_prev, v_prev, k_cur, v_cur, sink_ref, prev_limit, o_ref, row0):
    tq = q.shape[0]
    kvh = k_cur.shape[-1] // ATT_HEAD_DIM
    n_heads = kvh * GQA
    qi = lax.broadcasted_iota(jnp.int32, (tq, 1), 0)
    jp = lax.broadcasted_iota(jnp.int32, (1, WINDOW), 1)
    jc = lax.broadcasted_iota(jnp.int32, (1, tq), 1)
    diff_p = WINDOW + qi - jp
    diff_c = qi - jc
    valid_p = diff_p < prev_limit
    valid_c = diff_c >= 0
    diff_pf, diff_cf = diff_p.astype(F32), diff_c.astype(F32)
    ones_p = (lax.broadcasted_iota(jnp.int32, (WINDOW, ATT_HEAD_DIM), 1) == 0).astype(F32)
    ones_c = ones_p[:tq]
    nt = (((1,), (1,)), ((), ()))
    scale = ATT_HEAD_DIM ** -0.5
    for kh in range(kvh):
        heads = [kh * GQA + h for h in range(GQA)]
        qs = (jnp.concatenate([q[:, h * ATT_HEAD_DIM:(h + 1) * ATT_HEAD_DIM] for h in heads], axis=0)
              * scale).astype(BF16)
        sl = slice(kh * ATT_HEAD_DIM, (kh + 1) * ATT_HEAD_DIM)
        kp, kc = k_prev[:, sl].astype(BF16), k_cur[:, sl].astype(BF16)
        vp = jnp.concatenate([v_prev[:, sl], ones_p], axis=-1).astype(BF16)
        vc = jnp.concatenate([v_cur[:, sl], ones_c], axis=-1).astype(BF16)
        s_p_all = lax.dot_general(qs, kp, nt, preferred_element_type=F32)
        s_c_all = lax.dot_general(qs, kc, nt, preferred_element_type=F32)
        p_p, p_c, m_all = [], [], []
        for hq, h in enumerate(heads):
            slope = 2.0 ** (-8.0 * (h + 1) / n_heads)
            s_p = jnp.where(valid_p, s_p_all[hq * tq:(hq + 1) * tq] - slope * diff_pf, NEG_INF)
            s_c = jnp.where(valid_c, s_c_all[hq * tq:(hq + 1) * tq] - slope * diff_cf, NEG_INF)
            if tq == WINDOW:
                m = jnp.max(jnp.maximum(s_p, s_c), axis=-1, keepdims=True)
            else:
                m = jnp.maximum(jnp.max(s_p, axis=-1, keepdims=True),
                                jnp.max(s_c, axis=-1, keepdims=True))
            m = jnp.maximum(m, sink_ref[h])
            p_p.append(jnp.exp(s_p - m))
            p_c.append(jnp.exp(s_c - m))
            m_all.append(m)
        o_all = (jnp.dot(jnp.concatenate(p_p, axis=0).astype(BF16), vp, preferred_element_type=F32)
                 + jnp.dot(jnp.concatenate(p_c, axis=0).astype(BF16), vc, preferred_element_type=F32))
        for hq, h in enumerate(heads):
            o_h = o_all[hq * tq:(hq + 1) * tq]
            den = o_h[:, ATT_HEAD_DIM:ATT_HEAD_DIM + 1] + jnp.exp(sink_ref[h] - m_all[hq])
            o_ref[pl.ds(row0, tq), pl.ds(h * ATT_HEAD_DIM, ATT_HEAD_DIM)] = o_h[:, :ATT_HEAD_DIM] / den


def _attn_prefill_body(q_ref, kvc_ref, kvp_ref, sink_ref, o_ref):
    kvd = kvc_ref.shape[-1] // 2
    kvc, kvp = kvc_ref[...], kvp_ref[...]
    prev_limit = jnp.where(pl.program_id(1) > 0, WINDOW, 0)
    _attn_unit(q_ref[...], kvp[:, :kvd], kvp[:, kvd:], kvc[:, :kvd], kvc[:, kvd:], sink_ref,
               prev_limit, o_ref, 0)


def _attn_decode_body(q_ref, kvc_ref, kc_ref, vc_ref, sink_ref, o_ref, *, nb, tq):
    kvd = kvc_ref.shape[-1] // 2
    for n in range(nb):
        kvc = kvc_ref[pl.ds(n * tq, tq), :]
        _attn_unit(q_ref[pl.ds(n * tq, tq), :], kc_ref[n], vc_ref[n], kvc[:, :kvd], kvc[:, kvd:],
                   sink_ref, WINDOW, o_ref, n * tq)


def _attn_prefill(q, kv, sinks, b, t):
    n, dq = q.shape
    kv2 = kv.shape[-1]
    nblk = t // WINDOW
    return pl.pallas_call(
        _attn_prefill_body,
        grid=(b, nblk),
        in_specs=[
            pl.BlockSpec((WINDOW, dq), lambda i, j: (i * nblk + j, 0)),
            pl.BlockSpec((WINDOW, kv2), lambda i, j: (i * nblk + j, 0)),
            pl.BlockSpec((WINDOW, kv2), lambda i, j: (i * nblk + jnp.maximum(j - 1, 0), 0)),
            pl.BlockSpec(memory_space=pltpu.SMEM),
        ],
        out_specs=pl.BlockSpec((WINDOW, dq), lambda i, j: (i * nblk + j, 0)),
        out_shape=jax.ShapeDtypeStruct((n, dq), F32),
        compiler_params=_params(("parallel", "arbitrary")),
        name="attn_prefill",
    )(q, kv, kv, sinks.astype(F32))


def _attn_decode(q, kv, cache_k, cache_v, sinks, b, t):
    n, dq = q.shape
    kv2 = kv.shape[-1]
    kvd = kv2 // 2
    nb = min(ATT_NB, b)
    rows = nb * t
    return pl.pallas_call(
        functools.partial(_attn_decode_body, nb=nb, tq=t),
        grid=(b // nb,),
        in_specs=[
            pl.BlockSpec((rows, dq), lambda i: (i, 0)),
            pl.BlockSpec((rows, kv2), lambda i: (i, 0)),
            pl.BlockSpec((nb, WINDOW, kvd), lambda i: (i, 0, 0)),
            pl.BlockSpec((nb, WINDOW, kvd), lambda i: (i, 0, 0)),
            pl.BlockSpec(memory_space=pltpu.SMEM),
        ],
        out_specs=pl.BlockSpec((rows, dq), lambda i: (i, 0)),
        out_shape=jax.ShapeDtypeStruct((n, dq), F32),
        compiler_params=_params(("parallel",)),
        name="attn_decode",
    )(q, kv, cache_k.reshape(b, WINDOW, kvd), cache_v.reshape(b, WINDOW, kvd), sinks.astype(F32))


def _moe_body(te_ref, tv_ref, tok_ref, dst_ref, h_hbm, gate_ref, wg_ref, wu_ref, wd_ref, out_hbm,
              xbuf, obuf, gsem, ssem, wg16, wu16, wd16):
    t = pl.program_id(0)
    last = pl.num_programs(0) - 1
    slot = t % 2

    def row_gather(tile, buf, r):
        tok = tok_ref[tile * MOE_TG + r]
        return pltpu.make_async_copy(h_hbm.at[pl.ds(tok, 1)], xbuf.at[buf, pl.ds(r, 1)], gsem.at[buf])

    def row_scatter(tile, r):
        dst = dst_ref[tile * MOE_TG + r]
        return pltpu.make_async_copy(obuf.at[pl.ds(r, 1)], out_hbm.at[pl.ds(dst, 1)], ssem.at[0])

    def for_rows(tile, fn):
        groups = lax.shift_right_logical(tv_ref[tile] + (SUBLANES - 1), 3)

        def body(i, carry):
            base = pl.multiple_of(i * SUBLANES, SUBLANES)
            for u in range(SUBLANES):
                fn(base + u)
            return carry
        lax.fori_loop(0, groups, body, 0)

    @pl.when(t == 0)
    def _():
        xbuf[...] = jnp.zeros_like(xbuf)
        obuf[...] = jnp.zeros_like(obuf)
        spare = pltpu.make_async_copy(
            obuf, out_hbm.at[pl.ds(out_hbm.shape[0] - MOE_TG, MOE_TG)], ssem.at[0])
        spare.start()
        spare.wait()
        for_rows(0, lambda r: row_gather(0, 0, r).start())

    nxt = jnp.minimum(t + 1, last)

    @pl.when(t < last)
    def _():
        for_rows(nxt, lambda r: row_gather(nxt, 1 - slot, r).start())

    prev = jnp.maximum(t - 1, 0)

    @pl.when((t == 0) | (te_ref[t] != te_ref[prev]))
    def _():
        wg16[...] = wg_ref[...].astype(BF16)
        wu16[...] = wu_ref[...].astype(BF16)
        wd16[...] = wd_ref[...].astype(BF16)

    @pl.when(tv_ref[t] > 0)
    def _():
        for_rows(t, lambda r: row_gather(t, slot, r).wait())
        x = xbuf[slot].astype(BF16)
        hg = jnp.dot(x, wg16[...], preferred_element_type=F32)
        hu = jnp.dot(x, wu16[...], preferred_element_type=F32)
        act = (hg * jax.nn.sigmoid(hg)) * hu * gate_ref[...]
        o = jnp.dot(act.astype(BF16), wd16[...], preferred_element_type=F32)

        @pl.when(t > 0)
        def _():
            for_rows(prev, lambda r: row_scatter(prev, r).wait())

        obuf[...] = o
        for_rows(t, lambda r: row_scatter(t, r).start())

        @pl.when(t == last)
        def _():
            for_rows(t, lambda r: row_scatter(t, r).wait())

    @pl.when((tv_ref[t] == 0) & (t > 0))
    def _():
        for_rows(prev, lambda r: row_scatter(prev, r).wait())


def _moe_plan(route, n_tiles):
    n = route.shape[0]
    e = route[:, :2].astype(jnp.int32).reshape(-1)
    w = route[:, 2:4].reshape(-1)
    onehot = (e[:, None] == jnp.arange(N_EXPERTS, dtype=jnp.int32)[None, :]).astype(jnp.int32)
    csum = jnp.cumsum(onehot, axis=0)
    rank = jnp.take_along_axis(csum, e[:, None], axis=1)[:, 0] - 1
    counts = csum[-1]
    padded = ((counts + MOE_TG - 1) // MOE_TG) * MOE_TG
    pend = jnp.cumsum(padded)
    dest = (pend - padded)[e] + rank
    p_rows = n_tiles * MOE_TG
    vals = jnp.stack([jnp.arange(1, 2 * n + 1, dtype=jnp.int32), lax.bitcast_convert_type(w, jnp.int32)],
                     axis=1)
    placed = jnp.zeros((p_rows, 2), jnp.int32).at[dest].set(vals)
    has_pair = placed[:, 0] > 0
    pair = placed[:, 0] - 1
    row_token = jnp.where(has_pair, pair // 2, 0)
    spare = 2 * n + jnp.arange(p_rows, dtype=jnp.int32) % MOE_TG
    row_dst = jnp.where(has_pair, (pair % 2) * n + pair // 2, spare)
    row_gate = jnp.where(has_pair, lax.bitcast_convert_type(placed[:, 1], F32), 0.0)
    starts = jnp.arange(n_tiles, dtype=jnp.int32) * MOE_TG
    tile_expert = jnp.minimum(jnp.searchsorted(pend, starts, side="right"), N_EXPERTS - 1)
    tile_rows = jnp.clip(counts[tile_expert] - (starts - (pend - padded)[tile_expert]), 0, MOE_TG)
    tile_rows = jnp.where(starts < pend[-1], tile_rows, 0).astype(jnp.int32)
    last_used = jnp.max(jnp.where(tile_rows > 0, tile_expert, 0))
    tile_expert = jnp.where(tile_rows > 0, tile_expert, last_used).astype(jnp.int32)
    return row_token, row_dst, row_gate, tile_expert, tile_rows


def _moe(h, route, w_gate, w_up, w_down, layer):
    n, d = h.shape
    row = (d,)
    f = w_gate.shape[-1]
    n_tiles = (2 * n) // MOE_TG + N_EXPERTS
    row_token, row_dst, row_gate, tile_expert, tile_rows = _moe_plan(route, n_tiles)
    return pl.pallas_call(
        _moe_body,
        grid_spec=pltpu.PrefetchScalarGridSpec(
            num_scalar_prefetch=4,
            grid=(n_tiles,),
            in_specs=[
                pl.BlockSpec(memory_space=pl.ANY),
                pl.BlockSpec((MOE_TG, 1), lambda t, te, tv, rt, rd: (t, 0)),
                pl.BlockSpec((None, None, d, f), lambda t, te, tv, rt, rd: (layer, te[t], 0, 0)),
                pl.BlockSpec((None, None, d, f), lambda t, te, tv, rt, rd: (layer, te[t], 0, 0)),
                pl.BlockSpec((None, None, f, d), lambda t, te, tv, rt, rd: (layer, te[t], 0, 0)),
            ],
            out_specs=pl.BlockSpec(memory_space=pl.ANY),
            scratch_shapes=[pltpu.VMEM((2, MOE_TG) + row, F32), pltpu.VMEM((MOE_TG,) + row, F32),
                            pltpu.SemaphoreType.DMA((2,)), pltpu.SemaphoreType.DMA((1,)),
                            pltpu.VMEM((d, f), BF16), pltpu.VMEM((d, f), BF16),
                            pltpu.VMEM((f, d), BF16)],
        ),
        out_shape=jax.ShapeDtypeStruct((2 * n + MOE_TG,) + row, F32),
        compiler_params=_params(("arbitrary",)),
        name="moe_ffn",
    )(tile_expert, tile_rows, row_token, row_dst, h, row_gate.reshape(-1, 1), w_gate, w_up, w_down)


def _split_mod(mod, n_parts, b_first):
    d = mod.shape[-1] // n_parts
    parts = [mod[:, k * d:(k + 1) * d] for k in range(n_parts)]
    return [[p[:b_first, None, :] for p in parts], [p[b_first:, None, :] for p in parts]]


def kernel(x_prompt, x_sample, c_prompt, c_sample, state_wkv, state_shift, cache_k_win, cache_v_win, ada_w, ada_b, ln_g, ln_b, rw_mu, rw_w_rkv, rw_w0, rw_w1, rw_w2, rw_a0, rw_a1, rw_a2, rw_g1, rw_g2, rw_k_k, rw_k_a, rw_r_k, rw_lnx_w, rw_lnx_b, rw_wo, ada_kv_w, ada_kv_b, w_kv, w_q, attn_sinks, w_o_attn, router_w, router_bias, moe_w_gate, moe_w_up, moe_w_down):
    d = x_prompt.shape[-1]
    xs = [x_prompt, x_sample]
    shapes = [x.shape[:2] for x in xs]
    counts = [b * t for b, t in shapes]
    bp = shapes[0][0]

    c_all = jnp.concatenate([c_prompt, c_sample], axis=0)
    mods = {(l, s): _split_mod(_mm(c_all, ada_w, name="mm_ada", w_lead=(l, s), bias=ada_b[l, s],
                                   in_act="silu"), 3, bp)
            for l in range(DEPTH) for s in range(2)}
    mods_kv = _split_mod(_mm(c_all, ada_kv_w, name="mm_ada_kv", bias=ada_kv_b, in_act="silu"), 2, bp)

    by_slot = lambda x: x.reshape(-1, N_GROUPS, EXPERTS_PER_GROUP).swapaxes(1, 2).reshape(-1, N_EXPERTS)
    router = (by_slot(router_w), by_slot(router_bias))

    n_all = counts[0] + counts[1]
    row0 = [0, counts[0]]

    def moe_layer(h_all, routes, layer):
        out = _moe(h_all, jnp.concatenate(routes, axis=0), moe_w_gate, moe_w_up, moe_w_down, layer)
        return [[(out, slot * n_all + row0[tr]) for slot in range(2)] for tr in range(2)]

    firsts = [jnp.zeros((bp, 1, d), F32), state_shift[0][:, None, :]]
    s0s = [None, _state_to_groups(state_wkv[0])]
    x1, h1, route1, states, hlasts = [], None, [], [], []
    for tr in range(2):
        b, t = shapes[tr]
        shift, scale, gate = mods[0, 0][tr]
        xmix, hlast, lw, la, lg = _premix(xs[tr], shift, scale, firsts[tr], rw_mu[0], rw_w1[0],
                                          rw_a1[0], rw_g1[0])
        r = _mm(xmix, rw_w_rkv, name="mm_r", x_lead=(0,), w_lead=(0, 0), tm=MM_TM_BF16)
        k = _mm(xmix, rw_w_rkv, name="mm_k", x_lead=(1,), w_lead=(0, 1), tm=MM_TM_BF16)
        v = _mm(xmix, rw_w_rkv, name="mm_v", x_lead=(2,), w_lead=(0, 2), tm=MM_TM_BF16)
        y, st = _rwkv_scan(r, k, v, lw, la, rw_w2[0], rw_w0[0], rw_a2[0], rw_a0[0], rw_k_k[0],
                           rw_k_a[0], s0s[tr], b, t)
        z = _rwkv_post(y, r, k, v, la, lg, rw_a2[0], rw_a0[0], rw_g2[0], rw_lnx_w[0], rw_lnx_b[0],
                       rw_r_k[0], rw_k_a[0])
        out = _mm(z, rw_wo, name="mm_wo", w_lead=(0,), tm=MM_TM_BF16)
        sh, sc, _ = mods[0, 1][tr]
        xn, h1, route = _ln_mod(xs[tr], [(out, 0)], gate, ln_g[0, 0], ln_b[0, 0], [(sh, sc)], router,
                                (h1, n_all, row0[tr]))
        x1.append(xn), route1.append(route), states.append(st), hlasts.append(hlast)
    moe1 = moe_layer(h1, route1, 0)

    caches = [None, (cache_k_win, cache_v_win)]
    x3, h3, route3, kvs = [], None, [], []
    for tr in range(2):
        b, t = shapes[tr]
        shq, scq, gate_q = mods[1, 0][tr]
        shk, sck = mods_kv[tr]
        x2, hq, hkv = _ln_mod(x1[tr], moe1[tr], mods[0, 1][tr][2], ln_g[0, 1], ln_b[0, 1],
                              [(shq, scq), (shk, sck)])
        q = _mm(hq, w_q, name="mm_q", w_lead=(0,), tm=MM_TM_BF16)
        kv = _mm(hkv, w_kv, name="mm_kv", tm=MM_TM_BF16)
        if caches[tr] is None:
            o = _attn_prefill(q, kv, attn_sinks[0], b, t)
        else:
            o = _attn_decode(q, kv, caches[tr][0], caches[tr][1], attn_sinks[0], b, t)
        out = _mm(o, w_o_attn, name="mm_o", w_lead=(0,))
        sh, sc, _ = mods[1, 1][tr]
        xn, h3, route = _ln_mod(x2, [(out, 0)], gate_q, ln_g[1, 0], ln_b[1, 0], [(sh, sc)], router,
                                (h3, n_all, row0[tr]))
        x3.append(xn), route3.append(route), kvs.append(kv)
    moe3 = moe_layer(h3, route3, 1)
    ys = [_ln_mod(x3[tr], moe3[tr], mods[1, 1][tr][2], ln_g[1, 1], ln_b[1, 1], [])[0]
          for tr in range(2)]

    kvd = kvs[0].shape[-1] // 2
    kvh = kvd // ATT_HEAD_DIM
    (bp, tp), (bs, ts) = shapes
    kv_p = kvs[0].reshape(bp, tp, 2 * kvd)[:, tp - WINDOW:]
    p_k_win = kv_p[..., :kvd].reshape(bp, WINDOW, kvh, ATT_HEAD_DIM)
    p_v_win = kv_p[..., kvd:].reshape(bp, WINDOW, kvh, ATT_HEAD_DIM)
    kv_s = kvs[1].reshape(bs, ts, 2 * kvd)
    s_k_win = jnp.concatenate(
        [cache_k_win, kv_s[..., :kvd].reshape(bs, ts, kvh, ATT_HEAD_DIM)], axis=1)[:, -WINDOW:]
    s_v_win = jnp.concatenate(
        [cache_v_win, kv_s[..., kvd:].reshape(bs, ts, kvh, ATT_HEAD_DIM)], axis=1)[:, -WINDOW:]
    p_wkv = _state_from_groups(states[0])[None]
    s_wkv = _state_from_groups(states[1])[None]
    p_shift = hlasts[0].reshape(1, bp, d)
    s_shift = hlasts[1].reshape(1, bs, d)
    return (ys[0], ys[1], p_wkv, p_shift, p_k_win, p_v_win, s_wkv, s_shift, s_k_win, s_v_win)
```

```python
import functools

import jax
import jax.numpy as jnp
import numpy as np
from jax import lax
from jax.experimental import pallas as pl
from jax.experimental.pallas import tpu as pltpu

F32 = jnp.float32
BF16 = jnp.bfloat16

RW_HEAD = 64
ATT_HEAD_DIM = 64
GQA = 8
WINDOW = 128
N_EXPERTS = 32
EXPERTS_PER_GROUP = 4
N_GROUPS = N_EXPERTS // EXPERTS_PER_GROUP
DEPTH = 2
ALPHA = (2.0 * DEPTH) ** 0.25
LN_EPS = 1e-5
RW_LN_EPS = 64e-5
NEG_INF = -1e30
MIX_RKV = (0, 2, 3)
MIX_W, MIX_A, MIX_G = 1, 4, 5

SUBLANES = 8
LANE_GROUP = 256
HEADS_PER_GROUP = LANE_GROUP // RW_HEAD
VMEM_LIMIT = 56 * 1024 * 1024

MM_TM = 1024
MM_TM_BF16 = 2048
MM_TN = 512
EW_ROWS = 256
SCAN_TT = 64
SCAN_NB = 4
SCAN_UNROLL = 8
ATT_NB = 8
MOE_TG = 256


def _params(sem):
    return pltpu.CompilerParams(dimension_semantics=sem, vmem_limit_bytes=VMEM_LIMIT)


def _softplus(z):
    return jnp.maximum(z, 0.0) + jnp.log1p(jnp.exp(-jnp.abs(z)))


def _act(x, kind):
    if kind is None:
        return x
    if kind == "silu":
        return x * jax.nn.sigmoid(x)
    if kind == "tanh":
        return jnp.tanh(x)
    if kind == "sigmoid":
        return jax.nn.sigmoid(x)
    if kind == "decay":
        w = -_softplus(-x) - 0.5
        return jnp.exp(-jnp.exp(w))
    raise ValueError(kind)


def _mm_body(*refs, in_act, out_act, has_bias, stage_x):
    x_ref, w_ref = refs[0], refs[1]
    b_ref = refs[2] if has_bias else None
    o_ref = refs[2 + has_bias]
    if stage_x:
        xs_ref = refs[3 + has_bias]

        @pl.when(pl.program_id(1) == 0)
        def _():
            xs_ref[...] = _act(x_ref[...].astype(F32), in_act).astype(BF16)
    else:
        xs_ref = x_ref

    acc = jnp.dot(xs_ref[...], w_ref[...].astype(BF16), preferred_element_type=F32)
    if has_bias:
        acc = acc + b_ref[...]
    o_ref[...] = _act(acc, out_act).astype(o_ref.dtype)


def _mm(x, w, *, name, x_lead=(), w_lead=(), bias=None, in_act=None, out_act=None, out_dtype=F32,
        tm=MM_TM, tn=MM_TN):
    m, k = x.shape[-2:]
    n = w.shape[-1]
    tm = min(tm, m)
    tn = min(tn, n)
    assert m % tm == 0 and n % tn == 0, (m, tm, n, tn)
    nx, nw = len(x_lead), len(w_lead)
    in_specs = [
        pl.BlockSpec((None,) * nx + (tm, k), lambda i, j: tuple(x_lead) + (i, 0)),
        pl.BlockSpec((None,) * nw + (k, tn), lambda i, j: tuple(w_lead) + (0, j)),
    ]
    args = [x, w]
    if bias is not None:
        in_specs.append(pl.BlockSpec((1, tn), lambda i, j: (0, j)))
        args.append(bias.reshape(1, n).astype(F32))
    stage_x = x.dtype != BF16 or in_act is not None
    body = functools.partial(_mm_body, in_act=in_act, out_act=out_act, has_bias=bias is not None,
                             stage_x=stage_x)
    return pl.pallas_call(
        body,
        grid=(m // tm, n // tn),
        in_specs=in_specs,
        out_specs=pl.BlockSpec((tm, tn), lambda i, j: (i, j)),
        out_shape=jax.ShapeDtypeStruct((m, n), out_dtype),
        scratch_shapes=[pltpu.VMEM((tm, k), BF16)] if stage_x else [],
        compiler_params=_params(("parallel", "arbitrary")),
        name=name,
    )(*args)


def _geo(b, t):
    if t == SUBLANES:
        g = min(EW_ROWS // SUBLANES, b)
        return dict(groups=g, steps=b // g, mod_groups=g, steps_per_batch=1, mod_index=lambda i: i)
    assert t % EW_ROWS == 0
    spb = t // EW_ROWS
    return dict(groups=EW_ROWS // SUBLANES, steps=b * spb, mod_groups=1, steps_per_batch=spb,
                mod_index=lambda i: i // spb)


def _mod_spec(mod, mg, d, block_of_step):
    _, part, row0 = mod
    assert row0 % mg == 0, (row0, mg)
    return pl.BlockSpec((mg, 1, d), lambda i: (row0 // mg + block_of_step(i), 0, part))


def _premix_body(x_ref, xprev_ref, first_ref, shift_ref, scale_ref, mu_ref, w1_ref, a1_ref, g1_ref,
                 xmix_ref, hlast_ref, lw_ref, la_ref, lg_ref, *, steps_per_batch):
    i = pl.program_id(0)
    g, s, d = x_ref.shape
    gf = first_ref.shape[0]
    scale = scale_ref[...]
    shift = shift_ref[...]
    h3 = x_ref[...] * (1.0 + scale) + shift
    hlast_ref[...] = h3[g - gf:, s - 1:, :]
    h = h3.reshape(g * s, d)
    if steps_per_batch > 1:
        hprev_row = xprev_ref[:, s - 1:, :] * (1.0 + scale) + shift
        first = jnp.where(i % steps_per_batch == 0, first_ref[...], hprev_row)
        period = g * s
    else:
        first = first_ref[...]
        period = s
    first2 = jnp.broadcast_to(first, (g, s, d)).reshape(g * s, d)
    row = lax.broadcasted_iota(jnp.int32, (g * s, 1), 0)
    hp = jnp.where(row % period == 0, first2, pltpu.roll(h, 1, 0))
    xx = hp - h
    mix = lambda m: (h + xx * mu_ref[m]).astype(BF16)
    for slot, m in enumerate(MIX_RKV):
        xmix_ref[slot] = mix(m)
    lora = lambda m, w_ref: jnp.dot(mix(m), w_ref[...].astype(BF16), preferred_element_type=F32)
    lw_ref[...] = jnp.tanh(lora(MIX_W, w1_ref))
    la_ref[...] = lora(MIX_A, a1_ref)
    lg_ref[...] = jax.nn.sigmoid(lora(MIX_G, g1_ref))


def _premix(x, shift, scale, first, mu, w1, a1, g1):
    b, t, d = x.shape
    n = b * t
    n_mix = mu.shape[0]
    c = _geo(b, t)
    g, mg, midx = c["groups"], c["mod_groups"], c["mod_index"]
    rows = g * SUBLANES
    xg = x.reshape(-1, SUBLANES, d)
    per_batch = pl.BlockSpec((mg, 1, d), lambda i: (midx(i), 0, 0))
    mod_spec = lambda mod: _mod_spec(mod, mg, d, midx)
    whole = lambda a: pl.BlockSpec(a.shape, lambda i: (0,) * a.ndim)
    lora_out = lambda w: (pl.BlockSpec((rows, w.shape[-1]), lambda i: (i, 0)),
                          jax.ShapeDtypeStruct((n, w.shape[-1]), F32))
    lora_specs, lora_shapes = zip(lora_out(w1), lora_out(a1), lora_out(g1))
    return pl.pallas_call(
        functools.partial(_premix_body, steps_per_batch=c["steps_per_batch"]),
        grid=(c["steps"],),
        in_specs=[
            pl.BlockSpec((g, SUBLANES, d), lambda i: (i, 0, 0)),
            pl.BlockSpec((1, SUBLANES, d), lambda i: (jnp.maximum(i * g - 1, 0), 0, 0)),
            per_batch, mod_spec(shift), mod_spec(scale),
            pl.BlockSpec((n_mix, 1, d), lambda i: (0, 0, 0)),
            whole(w1), whole(a1), whole(g1),
        ],
        out_specs=[pl.BlockSpec((len(MIX_RKV), rows, d), lambda i: (0, i, 0)), per_batch,
                   *lora_specs],
        out_shape=[jax.ShapeDtypeStruct((len(MIX_RKV), n, d), BF16),
                   jax.ShapeDtypeStruct((b, 1, d), F32), *lora_shapes],
        compiler_params=_params(("arbitrary",)),
        name="premix",
    )(xg, xg, first, shift[0], scale[0], mu.reshape(n_mix, 1, d), w1, a1, g1)


def _split_bf16(x):
    hi = x.astype(BF16)
    return hi, (x - hi.astype(F32)).astype(BF16)


def _route(h, rw_ref, rb_ref):
    t = h.shape[0]
    h_hi, h_lo = _split_bf16(h)
    w_hi, w_lo = _split_bf16(rw_ref[...])
    logits = (jnp.dot(h_hi, w_hi, preferred_element_type=F32)
              + jnp.dot(h_lo, w_hi, preferred_element_type=F32)
              + jnp.dot(h_hi, w_lo, preferred_element_type=F32))
    s_all = jax.nn.sigmoid(logits)
    sb_all = s_all + rb_ref[...]
    s = [s_all[:, e * N_GROUPS:(e + 1) * N_GROUPS] for e in range(EXPERTS_PER_GROUP)]
    sb = [sb_all[:, e * N_GROUPS:(e + 1) * N_GROUPS] for e in range(EXPERTS_PER_GROUP)]
    hi01, lo01 = jnp.maximum(sb[0], sb[1]), jnp.minimum(sb[0], sb[1])
    hi23, lo23 = jnp.maximum(sb[2], sb[3]), jnp.minimum(sb[2], sb[3])
    top1 = jnp.maximum(hi01, hi23)
    top2 = jnp.maximum(jnp.minimum(hi01, hi23), jnp.maximum(lo01, lo23))
    gscore = top1 + top2
    lane = lax.broadcasted_iota(jnp.int32, (t, N_GROUPS), 1)
    gmax = jnp.max(gscore, axis=-1, keepdims=True)
    gsel = jnp.min(jnp.where(gscore == gmax, lane, N_GROUPS), axis=-1, keepdims=True)
    in_group = lane == gsel
    masked = [jnp.where(in_group, v, NEG_INF) for v in sb]
    idx = [lane * EXPERTS_PER_GROUP + e for e in range(EXPERTS_PER_GROUP)]

    def pick(vals):
        best = functools.reduce(jnp.maximum, [jnp.max(v, axis=-1, keepdims=True) for v in vals])
        return functools.reduce(jnp.minimum, [
            jnp.min(jnp.where(v == best, ix, N_EXPERTS), axis=-1, keepdims=True)
            for v, ix in zip(vals, idx)])

    def weight(sel):
        return functools.reduce(jnp.add, [
            jnp.sum(jnp.where(ix == sel, v, 0.0), axis=-1, keepdims=True) for v, ix in zip(s, idx)])

    e1 = pick(masked)
    e2 = pick([jnp.where(ix == e1, -jnp.inf, v) for v, ix in zip(masked, idx)])
    w1, w2 = weight(e1), weight(e2)
    wsum = w1 + w2
    return jnp.where(lane == 0, e1.astype(F32),
                     jnp.where(lane == 1, e2.astype(F32),
                               jnp.where(lane == 2, w1 / wsum, jnp.where(lane == 3, w2 / wsum, 0.0))))


def _ln_mod_body(*refs, n_planes, n_mods, with_route, steps):
    if with_route:
        @pl.when(pl.program_id(0) < steps)
        def _():
            _ln_mod_compute(*refs, n_planes=n_planes, n_mods=n_mods, with_route=True)

        @pl.when(pl.program_id(0) >= steps)
        def _():
            h_ref = refs[-2]
            h_ref[...] = jnp.zeros_like(h_ref)
    else:
        _ln_mod_compute(*refs, n_planes=n_planes, n_mods=n_mods, with_route=False)


def _ln_mod_compute(*refs, n_planes, n_mods, with_route):
    it = iter(refs)
    x_ref = next(it)
    o_refs = [next(it) for _ in range(n_planes)]
    gate_ref, lng_ref, lnb_ref = next(it), next(it), next(it)
    mod_refs = [(next(it), next(it)) for _ in range(n_mods)]
    rw_ref = next(it) if with_route else None
    rb_ref = next(it) if with_route else None
    if with_route:
        next(it)
    xnew_ref = next(it)
    h_refs = [next(it) for _ in range(n_mods)]
    route_ref = next(it) if with_route else None

    g, s, d = x_ref.shape

    out = o_refs[0][...]
    for o_ref in o_refs[1:]:
        out = out + o_ref[...]
    y = ALPHA * x_ref[...] + gate_ref[...] * out
    mu = jnp.mean(y, axis=-1, keepdims=True)
    yc = y - mu
    var = jnp.mean(yc * yc, axis=-1, keepdims=True)
    xn = yc * lax.rsqrt(var + LN_EPS) * lng_ref[...] + lnb_ref[...]
    xnew_ref[...] = xn
    for k, ((shift_ref, scale_ref), h_ref) in enumerate(zip(mod_refs, h_refs)):
        h = (xn * (1.0 + scale_ref[...]) + shift_ref[...]).reshape(g * s, d)
        h_ref[...] = h.astype(h_ref.dtype)
        if with_route:
            route_ref[...] = _route(h, rw_ref, rb_ref)


def _ln_mod(x, planes, gate, ln_g, ln_b, mods, router=None, moe_rows=None):
    b, t, d = x.shape
    n = b * t
    c = _geo(b, t)
    g, mg, midx = c["groups"], c["mod_groups"], c["mod_index"]
    rows = g * SUBLANES
    with_route = router is not None
    steps = c["steps"]
    tail_steps = 0
    if with_route and moe_rows[0] is None:
        assert moe_rows[2] == 0 and (moe_rows[1] - n) % rows == 0
        tail_steps = (moe_rows[1] - n) // rows
    cl = lambda i: jnp.minimum(i, steps - 1)
    tok = pl.BlockSpec((g, SUBLANES, d), lambda i: (cl(i), 0, 0))
    mod_spec = lambda mod: _mod_spec(mod, mg, d, lambda i: midx(cl(i)))
    const = pl.BlockSpec((1, 1, d), lambda i: (0, 0, 0))
    in_specs = [tok]
    args = [x.reshape(-1, SUBLANES, d)]
    for arr, row0 in planes:
        assert row0 % rows == 0, (row0, rows)
        in_specs.append(pl.BlockSpec((g, SUBLANES, d), lambda i, off=row0 // rows: (off + cl(i), 0, 0)))
        args.append(arr.reshape(-1, SUBLANES, d))
    in_specs += [mod_spec(gate), const, const]
    args += [gate[0], ln_g.reshape(1, 1, d), ln_b.reshape(1, 1, d)]
    for shift, scale in mods:
        in_specs += [mod_spec(shift), mod_spec(scale)]
        args += [shift[0], scale[0]]
    out_shape = [jax.ShapeDtypeStruct((n // SUBLANES, SUBLANES, d), F32)]
    out_specs = [tok]
    aliases = {}
    if with_route:
        in_specs += [pl.BlockSpec(a.shape, lambda i: (0, 0)) for a in router]
        args += list(router)
        buf, total, row0 = moe_rows
        if buf is None:
            in_specs.append(pl.BlockSpec((1, 128), lambda i: (0, 0)))
            args.append(jnp.zeros((1, 128), F32))
        else:
            in_specs.append(pl.BlockSpec(memory_space=pl.ANY))
            args.append(buf)
            aliases = {len(args) - 1: 1}
        out_shape.append(jax.ShapeDtypeStruct((total, d), F32))
        out_specs.append(pl.BlockSpec((rows, d), lambda i, off=row0 // rows: (off + i, 0)))
        out_shape.append(jax.ShapeDtypeStruct((n, N_GROUPS), F32))
        out_specs.append(pl.BlockSpec((rows, N_GROUPS), lambda i: (cl(i), 0)))
    else:
        for _ in mods:
            out_shape.append(jax.ShapeDtypeStruct((n, d), BF16))
            out_specs.append(pl.BlockSpec((rows, d), lambda i: (i, 0)))
    outs = pl.pallas_call(
        functools.partial(_ln_mod_body, n_planes=len(planes), n_mods=len(mods), with_route=with_route,
                          steps=steps),
        grid=(steps + tail_steps,),
        in_specs=in_specs,
        out_specs=out_specs,
        out_shape=out_shape,
        input_output_aliases=aliases,
        compiler_params=_params(("arbitrary",)),
        name="ln_mod",
    )(*args)
    return [outs[0].reshape(b, t, d)] + list(outs[1:])


def _group_consts():
    lane = np.arange(LANE_GROUP)
    bd = (lane[:, None] // RW_HEAD == lane[None, :] // RW_HEAD).astype(np.float32)
    eye = (np.arange(RW_HEAD)[:, None] == lane[None, :] % RW_HEAD).astype(np.float32)
    return jnp.asarray(bd, BF16), jnp.asarray(eye, F32)


def _head_sum(x, bd):
    hi = x.astype(BF16)
    lo = (x - hi.astype(F32)).astype(BF16)
    return jnp.dot(hi, bd, preferred_element_type=F32) + jnp.dot(lo, bd, preferred_element_type=F32)


def _head_sum_rows(x, bd):
    d = x.shape[-1]
    return jnp.concatenate(
        [_head_sum(x[:, c:c + LANE_GROUP], bd) for c in range(0, d, LANE_GROUP)], axis=-1)


def _decay_and_rate(lw, la, w2_ref, w0_ref, a2_ref, a0_ref):
    second = lambda x, w_ref: jnp.dot(x.astype(BF16), w_ref[...].astype(BF16), preferred_element_type=F32)
    return (_act(second(lw, w2_ref) + w0_ref[...], "decay"),
            jax.nn.sigmoid(second(la, a2_ref) + a0_ref[...]))


def _scan_body(*refs, has_s0):
    it = iter(refs)
    r_ref, k_ref, v_ref, lw_ref, la_ref = (next(it) for _ in range(5))
    w2_ref, w0_ref, a2_ref, a0_ref = (next(it) for _ in range(4))
    kk_ref, ka_ref, bd_ref, eye_ref = next(it), next(it), next(it), next(it)
    s0_ref = next(it) if has_s0 else None
    y_ref, st_ref = next(it), next(it)
    d_ref, kn_ref, al_ref, km_ref, yw_ref, vk_ref, lhs_ref = (next(it) for _ in range(7))

    nb, tt, d = r_ref.shape
    ng = st_ref.shape[1]

    @pl.when(pl.program_id(1) == 0)
    def _():
        if has_s0:
            st_ref[...] = s0_ref[...]
        else:
            st_ref[...] = jnp.zeros_like(st_ref)

    bd = bd_ref[...]
    eye = eye_ref[...]
    eye16 = eye.astype(BF16)
    k = k_ref[...].reshape(nb * tt, d)
    r = r_ref[...].reshape(nb * tt, d)
    dcy, a = _decay_and_rate(lw_ref[...].reshape(nb * tt, -1), la_ref[...].reshape(nb * tt, -1),
                             w2_ref, w0_ref, a2_ref, a0_ref)
    d_ref[...] = dcy.reshape(nb, tt, d)
    kk = k * kk_ref[...]
    kn = kk * lax.rsqrt(_head_sum_rows(kk * kk, bd) + 1e-12)
    al = kn * a
    km = k * (1.0 + (a - 1.0) * ka_ref[...])
    kn_ref[...] = kn.reshape(nb, tt, d)
    al_ref[...] = al.reshape(nb, tt, d)
    km_ref[...] = km.reshape(nb, tt, d)
    yw = dcy * r - kn * _head_sum_rows(al * r, bd)
    yw_ref[...] = yw.reshape(nb, tt, d)
    vk_ref[...] = v_ref[...] * _head_sum_rows(km * r, bd).reshape(nb, tt, d)
    packed_rows = 2 * SUBLANES

    def batch_step(t, n, lhs):
        def row(ref, g):
            return ref[n, pl.ds(t, 1), pl.ds(g * LANE_GROUP, LANE_GROUP)]

        def row16(ref, g):
            one = jnp.broadcast_to(row(ref, g), (packed_rows, LANE_GROUP)).astype(BF16)
            return jnp.concatenate([one] * (RW_HEAD // packed_rows), axis=0)

        for g in range(ng):
            q = g * 3 * RW_HEAD
            s16 = st_ref[n, g].astype(BF16)
            lhs[pl.ds(q, RW_HEAD), :] = s16 * row16(kn_ref, g)
            lhs[pl.ds(q + RW_HEAD, RW_HEAD), :] = s16 * row16(yw_ref, g)
            lhs[pl.ds(q + 2 * RW_HEAD, RW_HEAD), :] = eye16 * row16(v_ref, g)
        res = jnp.dot(lhs[...], bd, preferred_element_type=F32)
        for g in range(ng):
            q = g * 3 * RW_HEAD
            skk = res[q:q + RW_HEAD]
            ysum = res[q + RW_HEAD:q + 2 * RW_HEAD]
            vcol = res[q + 2 * RW_HEAD:q + 3 * RW_HEAD]
            st_ref[n, g] = (st_ref[n, g] * row(d_ref, g) - skk * row(al_ref, g)
                            + vcol * row(km_ref, g))
            y_ref[n, pl.ds(t, 1), pl.ds(g * LANE_GROUP, LANE_GROUP)] = (
                jnp.sum(ysum * eye, axis=0, keepdims=True) + row(vk_ref, g))

    n_sub = lhs_ref.shape[0]

    def steps(i, carry):
        for u in range(n_sub):
            for n in range(nb):
                batch_step(i * n_sub + u, n, lhs_ref.at[u, n])
        return carry

    lax.fori_loop(0, tt // n_sub, steps, 0)


def _rwkv_scan(r, k, v, lw, la, w2, w0, a2, a0, kk, ka, s0, b, t):
    n, d = r.shape
    ng = d // LANE_GROUP
    bd, eye = _group_consts()
    if t > SCAN_TT:
        nb, tt = b, SCAN_TT
    else:
        nb, tt = min(SCAN_NB, b), t
    tok = pl.BlockSpec((nb, tt, d), lambda i, c: (i, c, 0))
    low = lambda x: pl.BlockSpec((nb, tt, x.shape[-1]), lambda i, c: (i, c, 0))
    const2 = lambda shape: pl.BlockSpec(shape, lambda i, c: (0, 0))
    state = pl.BlockSpec((nb, ng, RW_HEAD, LANE_GROUP), lambda i, c: (i, 0, 0, 0))
    in_specs = [tok] * 3 + [low(lw), low(la), const2(w2.shape), const2((1, d)), const2(a2.shape),
                            const2((1, d)), const2((1, d)), const2((1, d)), const2(bd.shape),
                            const2(eye.shape)]
    args = [x.reshape(b, t, x.shape[-1]) for x in (r, k, v, lw, la)]
    args += [w2, w0.reshape(1, d), a2, a0.reshape(1, d), kk.reshape(1, d), ka.reshape(1, d), bd, eye]
    if s0 is not None:
        in_specs.append(state)
        args.append(s0)
    rows = nb * ng * RW_HEAD
    y, st = pl.pallas_call(
        functools.partial(_scan_body, has_s0=s0 is not None),
        grid=(b // nb, t // tt),
        in_specs=in_specs,
        out_specs=[tok, state],
        out_shape=[jax.ShapeDtypeStruct((b, t, d), F32),
                   jax.ShapeDtypeStruct((b, ng, RW_HEAD, LANE_GROUP), F32)],
        scratch_shapes=[pltpu.VMEM((nb, tt, d), F32)] * 6
        + [pltpu.VMEM((SCAN_UNROLL, nb, 3 * ng * RW_HEAD, LANE_GROUP), BF16)],
        compiler_params=_params(("parallel", "arbitrary")),
        name="rwkv_scan",
    )(*args)
    return y.reshape(n, d), st


def _state_to_groups(s):
    b, h, nv, nk = s.shape
    s = s.reshape(b, h // HEADS_PER_GROUP, HEADS_PER_GROUP, nv, nk)
    return s.transpose(0, 1, 3, 2, 4).reshape(b, h // HEADS_PER_GROUP, nv, HEADS_PER_GROUP * nk)


def _state_from_groups(s):
    b, ng, nv, _ = s.shape
    s = s.reshape(b, ng, nv, HEADS_PER_GROUP, RW_HEAD)
    return s.transpose(0, 1, 3, 2, 4).reshape(b, ng * HEADS_PER_GROUP, nv, RW_HEAD)


def _rwkv_post_body(y_ref, r_ref, k_ref, v_ref, la_ref, lg_ref, a2_ref, a0_ref, g2_ref, lnw_ref,
                    lnb_ref, rk_ref, ka_ref, bd_ref, z_ref):
    bd = bd_ref[...]
    y = y_ref[...]
    inv = 1.0 / RW_HEAD
    mean = _head_sum_rows(y, bd) * inv
    yc = y - mean
    var = _head_sum_rows(yc * yc, bd) * inv
    yn = yc * lax.rsqrt(var + RW_LN_EPS) * lnw_ref[...] + lnb_ref[...]
    second = lambda x, w_ref: jnp.dot(x.astype(BF16), w_ref[...].astype(BF16), preferred_element_type=F32)
    a = jax.nn.sigmoid(second(la_ref[...], a2_ref) + a0_ref[...])
    gate = second(lg_ref[...], g2_ref)
    km = k_ref[...] * (1.0 + (a - 1.0) * ka_ref[...])
    bonus = _head_sum_rows(r_ref[...] * km * rk_ref[...], bd) * v_ref[...]
    z_ref[...] = ((yn + bonus) * gate).astype(BF16)


def _rwkv_post(y, r, k, v, la, lg, a2, a0, g2, lnw, lnb, rk, ka):
    n, d = y.shape
    bd, _ = _group_consts()
    rows = min(EW_ROWS, n)
    tok = pl.BlockSpec((rows, d), lambda i: (i, 0))
    low = lambda x: pl.BlockSpec((rows, x.shape[-1]), lambda i: (i, 0))
    vec = pl.BlockSpec((1, d), lambda i: (0, 0))
    whole = lambda x: pl.BlockSpec(x.shape, lambda i: (0, 0))
    return pl.pallas_call(
        _rwkv_post_body,
        grid=(n // rows,),
        in_specs=[tok] * 4 + [low(la), low(lg), whole(a2), vec, whole(g2)] + [vec] * 4 + [whole(bd)],
        out_specs=tok,
        out_shape=jax.ShapeDtypeStruct((n, d), BF16),
        compiler_params=_params(("parallel",)),
        name="rwkv_post",
    )(y, r, k, v, la, lg, a2, a0.reshape(1, d), g2, lnw.reshape(1, d), lnb.reshape(1, d),
      rk.reshape(1, d), ka.reshape(1, d), bd)


def _attn_unit(q, k_prev, v_prev, k_cur, v_cur, sink_ref, prev_limit, o_ref, row0):
    tq = q.shape[0]
    kvh = k_cur.shape[-1] // ATT_HEAD_DIM
    n_heads = kvh * GQA
    qi = lax.broadcasted_iota(jnp.int32, (tq, 1), 0)
    jp = lax.broadcasted_iota(jnp.int32, (1, WINDOW), 1)
    jc = lax.broadcasted_iota(jnp.int32, (1, tq), 1)
    diff_p = WINDOW + qi - jp
    diff_c = qi - jc
    valid_p = diff_p < prev_limit
    valid_c = diff_c >= 0
    diff_pf, diff_cf = diff_p.astype(F32), diff_c.astype(F32)
    ones_p = (lax.broadcasted_iota(jnp.int32, (WINDOW, ATT_HEAD_DIM), 1) == 0).astype(F32)
    ones_c = ones_p[:tq]
    nt = (((1,), (1,)), ((), ()))
    scale = ATT_HEAD_DIM ** -0.5
    for kh in range(kvh):
        heads = [kh * GQA + h for h in range(GQA)]
        qs = (jnp.concatenate([q[:, h * ATT_HEAD_DIM:(h + 1) * ATT_HEAD_DIM] for h in heads], axis=0)
              * scale).astype(BF16)
        sl = slice(kh * ATT_HEAD_DIM, (kh + 1) * ATT_HEAD_DIM)
        kp, kc = k_prev[:, sl].astype(BF16), k_cur[:, sl].astype(BF16)
        vp = jnp.concatenate([v_prev[:, sl], ones_p], axis=-1).astype(BF16)
        vc = jnp.concatenate([v_cur[:, sl], ones_c], axis=-1).astype(BF16)
        s_p_all = lax.dot_general(qs, kp, nt, preferred_element_type=F32)
        s_c_all = lax.dot_general(qs, kc, nt, preferred_element_type=F32)
        p_p, p_c, m_all = [], [], []
        for hq, h in enumerate(heads):
            slope = 2.0 ** (-8.0 * (h + 1) / n_heads)
            s_p = jnp.where(valid_p, s_p_all[hq * tq:(hq + 1) * tq] - slope * diff_pf, NEG_INF)
            s_c = jnp.where(valid_c, s_c_all[hq * tq:(hq + 1) * tq] - slope * diff_cf, NEG_INF)
            if tq == WINDOW:
                m = jnp.max(jnp.maximum(s_p, s_c), axis=-1, keepdims=True)
            else:
                m = jnp.maximum(jnp.max(s_p, axis=-1, keepdims=True),
                                jnp.max(s_c, axis=-1, keepdims=True))
            m = jnp.maximum(m, sink_ref[h])
            p_p.append(jnp.exp(s_p - m))
            p_c.append(jnp.exp(s_c - m))
            m_all.append(m)
        o_all = (jnp.dot(jnp.concatenate(p_p, axis=0).astype(BF16), vp, preferred_element_type=F32)
                 + jnp.dot(jnp.concatenate(p_c, axis=0).astype(BF16), vc, preferred_element_type=F32))
        for hq, h in enumerate(heads):
            o_h = o_all[hq * tq:(hq + 1) * tq]
            den = o_h[:, ATT_HEAD_DIM:ATT_HEAD_DIM + 1] + jnp.exp(sink_ref[h] - m_all[hq])
            o_ref[pl.ds(row0, tq), pl.ds(h * ATT_HEAD_DIM, ATT_HEAD_DIM)] = o_h[:, :ATT_HEAD_DIM] / den


def _attn_prefill_body(q_ref, kvc_ref, kvp_ref, sink_ref, o_ref):
    kvd = kvc_ref.shape[-1] // 2
    kvc, kvp = kvc_ref[...], kvp_ref[...]
    prev_limit = jnp.where(pl.program_id(1) > 0, WINDOW, 0)
    _attn_unit(q_ref[...], kvp[:, :kvd], kvp[:, kvd:], kvc[:, :kvd], kvc[:, kvd:], sink_ref,
               prev_limit, o_ref, 0)


def _attn_decode_body(q_ref, kvc_ref, kc_ref, vc_ref, sink_ref, o_ref, *, nb, tq):
    kvd = kvc_ref.shape[-1] // 2
    for n in range(nb):
        kvc = kvc_ref[pl.ds(n * tq, tq), :]
        _attn_unit(q_ref[pl.ds(n * tq, tq), :], kc_ref[n], vc_ref[n], kvc[:, :kvd], kvc[:, kvd:],
                   sink_ref, WINDOW, o_ref, n * tq)


def _attn_prefill(q, kv, sinks, b, t):
    n, dq = q.shape
    kv2 = kv.shape[-1]
    nblk = t // WINDOW
    return pl.pallas_call(
        _attn_prefill_body,
        grid=(b, nblk),
        in_specs=[
            pl.BlockSpec((WINDOW, dq), lambda i, j: (i * nblk + j, 0)),
            pl.BlockSpec((WINDOW, kv2), lambda i, j: (i * nblk + j, 0)),
            pl.BlockSpec((WINDOW, kv2), lambda i, j: (i * nblk + jnp.maximum(j - 1, 0), 0)),
            pl.BlockSpec(memory_space=pltpu.SMEM),
        ],
        out_specs=pl.BlockSpec((WINDOW, dq), lambda i, j: (i * nblk + j, 0)),
        out_shape=jax.ShapeDtypeStruct((n, dq), F32),
        compiler_params=_params(("parallel", "arbitrary")),
        name="attn_prefill",
    )(q, kv, kv, sinks.astype(F32))


def _attn_decode(q, kv, cache_k, cache_v, sinks, b, t):
    n, dq = q.shape
    kv2 = kv.shape[-1]
    kvd = kv2 // 2
    nb = min(ATT_NB, b)
    rows = nb * t
    return pl.pallas_call(
        functools.partial(_attn_decode_body, nb=nb, tq=t),
        grid=(b // nb,),
        in_specs=[
            pl.BlockSpec((rows, dq), lambda i: (i, 0)),
            pl.BlockSpec((rows, kv2), lambda i: (i, 0)),
            pl.BlockSpec((nb, WINDOW, kvd), lambda i: (i, 0, 0)),
            pl.BlockSpec((nb, WINDOW, kvd), lambda i: (i, 0, 0)),
            pl.BlockSpec(memory_space=pltpu.SMEM),
        ],
        out_specs=pl.BlockSpec((rows, dq), lambda i: (i, 0)),
        out_shape=jax.ShapeDtypeStruct((n, dq), F32),
        compiler_params=_params(("parallel",)),
        name="attn_decode",
    )(q, kv, cache_k.reshape(b, WINDOW, kvd), cache_v.reshape(b, WINDOW, kvd), sinks.astype(F32))


def _moe_body(te_ref, tv_ref, tok_ref, dst_ref, h_hbm, gate_ref, wg_ref, wu_ref, wd_ref, out_hbm,
              xbuf, obuf, gsem, ssem, wg16, wu16, wd16):
    t = pl.program_id(0)
    last = pl.num_programs(0) - 1
    slot = t % 2

    def split(row):
        return lax.shift_right_logical(row, 3), row & (SUBLANES - 1)

    def gather_of(buf):
        def copy(i, u, tok):
            hi, lo = split(tok)
            return pltpu.make_async_copy(h_hbm.at[hi, pl.ds(lo, 1)], xbuf.at[buf, i, pl.ds(u, 1)],
                                         gsem.at[buf])
        return tok_ref, copy

    def scatter_copy(i, u, dst):
        hi, lo = split(dst)
        return pltpu.make_async_copy(obuf.at[i, pl.ds(u, 1)], out_hbm.at[hi, pl.ds(lo, 1)], ssem.at[0])

    scatter = (dst_ref, scatter_copy)

    def for_rows(tile, rows, act):
        table, build = rows
        groups = lax.shift_right_logical(tv_ref[tile] + (SUBLANES - 1), 3)

        def body(i, carry):
            base = tile * MOE_TG + i * SUBLANES
            entries = [table[base + u] for u in range(SUBLANES)]
            for u in range(SUBLANES):
                act(build(i, u, entries[u]))
            return carry
        lax.fori_loop(0, groups, body, 0)

    start = lambda copy: copy.start()
    wait = lambda copy: copy.wait()

    groups_per_tile = MOE_TG // SUBLANES

    @pl.when(t == 0)
    def _():
        xbuf[...] = jnp.zeros_like(xbuf)
        obuf[...] = jnp.zeros_like(obuf)
        spare = pltpu.make_async_copy(
            obuf, out_hbm.at[pl.ds(out_hbm.shape[0] - groups_per_tile, groups_per_tile)], ssem.at[0])
        spare.start()
        spare.wait()
        for_rows(0, gather_of(0), start)

    nxt = jnp.minimum(t + 1, last)

    @pl.when(t < last)
    def _():
        for_rows(nxt, gather_of(1 - slot), start)

    prev = jnp.maximum(t - 1, 0)

    @pl.when((t == 0) | (te_ref[t] != te_ref[prev]))
    def _():
        wg16[...] = wg_ref[...].astype(BF16)
        wu16[...] = wu_ref[...].astype(BF16)
        wd16[...] = wd_ref[...].astype(BF16)

    @pl.when(tv_ref[t] > 0)
    def _():
        for_rows(t, gather_of(slot), wait)
        x = xbuf[slot].reshape(MOE_TG, -1).astype(BF16)
        hg = jnp.dot(x, wg16[...], preferred_element_type=F32)
        hu = jnp.dot(x, wu16[...], preferred_element_type=F32)
        act = (hg * jax.nn.sigmoid(hg)) * hu * gate_ref[...]
        o = jnp.dot(act.astype(BF16), wd16[...], preferred_element_type=F32)

        @pl.when(t > 0)
        def _():
            for_rows(prev, scatter, wait)

        obuf[...] = o.reshape(obuf.shape)
        for_rows(t, scatter, start)

        @pl.when(t == last)
        def _():
            for_rows(t, scatter, wait)

    @pl.when((tv_ref[t] == 0) & (t > 0))
    def _():
        for_rows(prev, scatter, wait)


def _moe_plan(route, n_tiles):
    n = route.shape[0]
    e = route[:, :2].astype(jnp.int32).reshape(-1)
    w = route[:, 2:4].reshape(-1)
    onehot = (e[:, None] == jnp.arange(N_EXPERTS, dtype=jnp.int32)[None, :]).astype(jnp.int32)
    csum = jnp.cumsum(onehot, axis=0)
    rank = jnp.take_along_axis(csum, e[:, None], axis=1)[:, 0] - 1
    counts = csum[-1]
    padded = ((counts + MOE_TG - 1) // MOE_TG) * MOE_TG
    pend = jnp.cumsum(padded)
    dest = (pend - padded)[e] + rank
    p_rows = n_tiles * MOE_TG
    vals = jnp.stack([jnp.arange(1, 2 * n + 1, dtype=jnp.int32), lax.bitcast_convert_type(w, jnp.int32)],
                     axis=1)
    placed = jnp.zeros((p_rows, 2), jnp.int32).at[dest].set(vals)
    has_pair = placed[:, 0] > 0
    pair = placed[:, 0] - 1
    row_token = jnp.where(has_pair, pair // 2, 0)
    spare = 2 * n + jnp.arange(p_rows, dtype=jnp.int32) % MOE_TG
    row_dst = jnp.where(has_pair, (pair % 2) * n + pair // 2, spare)
    row_gate = jnp.where(has_pair, lax.bitcast_convert_type(placed[:, 1], F32), 0.0)
    starts = jnp.arange(n_tiles, dtype=jnp.int32) * MOE_TG
    tile_expert = jnp.minimum(jnp.searchsorted(pend, starts, side="right"), N_EXPERTS - 1)
    tile_rows = jnp.clip(counts[tile_expert] - (starts - (pend - padded)[tile_expert]), 0, MOE_TG)
    tile_rows = jnp.where(starts < pend[-1], tile_rows, 0).astype(jnp.int32)
    last_used = jnp.max(jnp.where(tile_rows > 0, tile_expert, 0))
    tile_expert = jnp.where(tile_rows > 0, tile_expert, last_used).astype(jnp.int32)
    return row_token, row_dst, row_gate, tile_expert, tile_rows


def _moe(h, route, w_gate, w_up, w_down, layer):
    n, d = h.shape
    group = (SUBLANES, d)
    f = w_gate.shape[-1]
    n_tiles = (2 * n) // MOE_TG + N_EXPERTS
    n_out = 2 * n + MOE_TG
    assert n % SUBLANES == 0
    row_token, row_dst, row_gate, tile_expert, tile_rows = _moe_plan(route, n_tiles)
    out = pl.pallas_call(
        _moe_body,
        grid_spec=pltpu.PrefetchScalarGridSpec(
            num_scalar_prefetch=4,
            grid=(n_tiles,),
            in_specs=[
                pl.BlockSpec(memory_space=pl.ANY),
                pl.BlockSpec((MOE_TG, 1), lambda t, te, tv, rt, rd: (t, 0)),
                pl.BlockSpec((None, None, d, f), lambda t, te, tv, rt, rd: (layer, te[t], 0, 0)),
                pl.BlockSpec((None, None, d, f), lambda t, te, tv, rt, rd: (layer, te[t], 0, 0)),
                pl.BlockSpec((None, None, f, d), lambda t, te, tv, rt, rd: (layer, te[t], 0, 0)),
            ],
            out_specs=pl.BlockSpec(memory_space=pl.ANY),
            scratch_shapes=[pltpu.VMEM((2, MOE_TG // SUBLANES) + group, F32),
                            pltpu.VMEM((MOE_TG // SUBLANES,) + group, F32),
                            pltpu.SemaphoreType.DMA((2,)), pltpu.SemaphoreType.DMA((1,)),
                            pltpu.VMEM((d, f), BF16), pltpu.VMEM((d, f), BF16),
                            pltpu.VMEM((f, d), BF16)],
        ),
        out_shape=jax.ShapeDtypeStruct((n_out // SUBLANES,) + group, F32),
        compiler_params=_params(("arbitrary",)),
        name="moe_ffn",
    )(tile_expert, tile_rows, row_token, row_dst, h.reshape((n // SUBLANES,) + group),
      row_gate.reshape(-1, 1), w_gate, w_up, w_down)
    return out.reshape(n_out, d)


def _split_mod(mod, n_parts, trunk_rows):
    mod = mod[:, None, :]
    return [[(mod, k, row0) for k in range(n_parts)] for row0 in trunk_rows]


def kernel(x_prompt, x_sample, c_prompt, c_sample, state_wkv, state_shift, cache_k_win, cache_v_win, ada_w, ada_b, ln_g, ln_b, rw_mu, rw_w_rkv, rw_w0, rw_w1, rw_w2, rw_a0, rw_a1, rw_a2, rw_g1, rw_g2, rw_k_k, rw_k_a, rw_r_k, rw_lnx_w, rw_lnx_b, rw_wo, ada_kv_w, ada_kv_b, w_kv, w_q, attn_sinks, w_o_attn, router_w, router_bias, moe_w_gate, moe_w_up, moe_w_down):
    d = x_prompt.shape[-1]
    xs = [x_prompt, x_sample]
    shapes = [x.shape[:2] for x in xs]
    counts = [b * t for b, t in shapes]
    bp = shapes[0][0]

    c_all = jnp.concatenate([c_sample, c_prompt], axis=0)
    trunk_rows = [shapes[1][0], 0]
    mods = {(l, s): _split_mod(_mm(c_all, ada_w, name="mm_ada", w_lead=(l, s), bias=ada_b[l, s],
                                   in_act="silu"), 3, trunk_rows)
            for l in range(DEPTH) for s in range(2)}
    mods_kv = _split_mod(_mm(c_all, ada_kv_w, name="mm_ada_kv", bias=ada_kv_b, in_act="silu"), 2,
                         trunk_rows)

    by_slot = lambda x: x.reshape(-1, N_GROUPS, EXPERTS_PER_GROUP).swapaxes(1, 2).reshape(-1, N_EXPERTS)
    router = (by_slot(router_w), by_slot(router_bias))

    n_all = counts[0] + counts[1]
    row0 = [0, counts[0]]

    def moe_layer(h_all, routes, layer):
        out = _moe(h_all, jnp.concatenate(routes, axis=0), moe_w_gate, moe_w_up, moe_w_down, layer)
        return [[(out, slot * n_all + row0[tr]) for slot in range(2)] for tr in range(2)]

    firsts = [jnp.zeros((bp, 1, d), F32), state_shift[0][:, None, :]]
    s0s = [None, _state_to_groups(state_wkv[0])]
    x1, h1, route1, states, hlasts = [], None, [], [], []
    for tr in range(2):
        b, t = shapes[tr]
        shift, scale, gate = mods[0, 0][tr]
        xmix, hlast, lw, la, lg = _premix(xs[tr], shift, scale, firsts[tr], rw_mu[0], rw_w1[0],
                                          rw_a1[0], rw_g1[0])
        r = _mm(xmix, rw_w_rkv, name="mm_r", x_lead=(0,), w_lead=(0, 0), tm=MM_TM_BF16)
        k = _mm(xmix, rw_w_rkv, name="mm_k", x_lead=(1,), w_lead=(0, 1), tm=MM_TM_BF16)
        v = _mm(xmix, rw_w_rkv, name="mm_v", x_lead=(2,), w_lead=(0, 2), tm=MM_TM_BF16)
        y, st = _rwkv_scan(r, k, v, lw, la, rw_w2[0], rw_w0[0], rw_a2[0], rw_a0[0], rw_k_k[0],
                           rw_k_a[0], s0s[tr], b, t)
        z = _rwkv_post(y, r, k, v, la, lg, rw_a2[0], rw_a0[0], rw_g2[0], rw_lnx_w[0], rw_lnx_b[0],
                       rw_r_k[0], rw_k_a[0])
        out = _mm(z, rw_wo, name="mm_wo", w_lead=(0,), tm=MM_TM_BF16)
        sh, sc, _ = mods[0, 1][tr]
        xn, h1, route = _ln_mod(xs[tr], [(out, 0)], gate, ln_g[0, 0], ln_b[0, 0], [(sh, sc)], router,
                                (h1, n_all, row0[tr]))
        x1.append(xn), route1.append(route), states.append(st), hlasts.append(hlast)
    moe1 = moe_layer(h1, route1, 0)

    caches = [None, (cache_k_win, cache_v_win)]
    x3, h3, route3, kvs = [], None, [], []
    for tr in range(2):
        b, t = shapes[tr]
        shq, scq, gate_q = mods[1, 0][tr]
        shk, sck = mods_kv[tr]
        x2, hq, hkv = _ln_mod(x1[tr], moe1[tr], mods[0, 1][tr][2], ln_g[0, 1], ln_b[0, 1],
                              [(shq, scq), (shk, sck)])
        q = _mm(hq, w_q, name="mm_q", w_lead=(0,), tm=MM_TM_BF16)
        kv = _mm(hkv, w_kv, name="mm_kv", tm=MM_TM_BF16)
        if caches[tr] is None:
            o = _attn_prefill(q, kv, attn_sinks[0], b, t)
        else:
            o = _attn_decode(q, kv, caches[tr][0], caches[tr][1], attn_sinks[0], b, t)
        out = _mm(o, w_o_attn, name="mm_o", w_lead=(0,))
        sh, sc, _ = mods[1, 1][tr]
        xn, h3, route = _ln_mod(x2, [(out, 0)], gate_q, ln_g[1, 0], ln_b[1, 0], [(sh, sc)], router,
                                (h3, n_all, row0[tr]))
        x3.append(xn), route3.append(route), kvs.append(kv)
    moe3 = moe_layer(h3, route3, 1)
    ys = [_ln_mod(x3[tr], moe3[tr], mods[1, 1][tr][2], ln_g[1, 1], ln_b[1, 1], [])[0]
          for tr in range(2)]

    kvd = kvs[0].shape[-1] // 2
    kvh = kvd // ATT_HEAD_DIM
    (bp, tp), (bs, ts) = shapes
    kv_p = kvs[0].reshape(bp, tp, 2 * kvd)[:, tp - WINDOW:]
    p_k_win = kv_p[..., :kvd].reshape(bp, WINDOW, kvh, ATT_HEAD_DIM)
    p_v_win = kv_p[..., kvd:].reshape(bp, WINDOW, kvh, ATT_HEAD_DIM)
    kv_s = kvs[1].reshape(bs, ts, 2 * kvd)
    s_k_win = jnp.concatenate(
        [cache_k_win, kv_s[..., :kvd].reshape(bs, ts, kvh, ATT_HEAD_DIM)], axis=1)[:, -WINDOW:]
    s_v_win = jnp.concatenate(
        [cache_v_win, kv_s[..., kvd:].reshape(bs, ts, kvh, ATT_HEAD_DIM)], axis=1)[:, -WINDOW:]
    p_wkv = _state_from_groups(states[0])[None]
    s_wkv = _state_from_groups(states[1])[None]
    p_shift = hlasts[0].reshape(1, bp, d)
    s_shift = hlasts[1].reshape(1, bs, d)
    return (ys[0], ys[1], p_wkv, p_shift, p_k_win, p_v_win, s_wkv, s_shift, s_k_win, s_v_win)
```

```python
import functools

import jax
import jax.numpy as jnp
import numpy as np
from jax import lax
from jax.experimental import pallas as pl
from jax.experimental.pallas import tpu as pltpu

F32 = jnp.float32
BF16 = jnp.bfloat16

RW_HEAD = 64
ATT_HEAD_DIM = 64
GQA = 8
WINDOW = 128
N_EXPERTS = 32
EXPERTS_PER_GROUP = 4
N_GROUPS = N_EXPERTS // EXPERTS_PER_GROUP
DEPTH = 2
ALPHA = (2.0 * DEPTH) ** 0.25
LN_EPS = 1e-5
RW_LN_EPS = 64e-5
NEG_INF = -1e30
MIX_RKV = (0, 2, 3)
MIX_W, MIX_A, MIX_G = 1, 4, 5

SUBLANES = 8
LANE_GROUP = 256
HEADS_PER_GROUP = LANE_GROUP // RW_HEAD
VMEM_LIMIT = 56 * 1024 * 1024

MM_TM = 1024
MM_TM_BF16 = 2048
MM_TN = 512
EW_ROWS = 256
SCAN_TT = 64
SCAN_NB = 4
SCAN_UNROLL = 8
ATT_NB = 8
MOE_TG = 256


def _params(sem):
    return pltpu.CompilerParams(dimension_semantics=sem, vmem_limit_bytes=VMEM_LIMIT)


def _softplus(z):
    return jnp.maximum(z, 0.0) + jnp.log1p(jnp.exp(-jnp.abs(z)))


def _act(x, kind):
    if kind is None:
        return x
    if kind == "silu":
        return x * jax.nn.sigmoid(x)
    if kind == "tanh":
        return jnp.tanh(x)
    if kind == "sigmoid":
        return jax.nn.sigmoid(x)
    if kind == "decay":
        w = -_softplus(-x) - 0.5
        return jnp.exp(-jnp.exp(w))
    raise ValueError(kind)


def _mm_body(*refs, in_act, out_act, has_bias, stage_x):
    x_ref, w_ref = refs[0], refs[1]
    b_ref = refs[2] if has_bias else None
    o_ref = refs[2 + has_bias]
    if stage_x:
        xs_ref = refs[3 + has_bias]

        @pl.when(pl.program_id(1) == 0)
        def _():
            xs_ref[...] = _act(x_ref[...].astype(F32), in_act).astype(BF16)
    else:
        xs_ref = x_ref

    acc = jnp.dot(xs_ref[...], w_ref[...].astype(BF16), preferred_element_type=F32)
    if has_bias:
        acc = acc + b_ref[...]
    o_ref[...] = _act(acc, out_act).astype(o_ref.dtype)


def _mm(x, w, *, name, x_lead=(), w_lead=(), bias=None, in_act=None, out_act=None, out_dtype=F32,
        tm=MM_TM, tn=MM_TN):
    m, k = x.shape[-2:]
    n = w.shape[-1]
    tm = min(tm, m)
    tn = min(tn, n)
    assert m % tm == 0 and n % tn == 0, (m, tm, n, tn)
    nx, nw = len(x_lead), len(w_lead)
    in_specs = [
        pl.BlockSpec((None,) * nx + (tm, k), lambda i, j: tuple(x_lead) + (i, 0)),
        pl.BlockSpec((None,) * nw + (k, tn), lambda i, j: tuple(w_lead) + (0, j)),
    ]
    args = [x, w]
    if bias is not None:
        in_specs.append(pl.BlockSpec((1, tn), lambda i, j: (0, j)))
        args.append(bias.reshape(1, n).astype(F32))
    stage_x = x.dtype != BF16 or in_act is not None
    body = functools.partial(_mm_body, in_act=in_act, out_act=out_act, has_bias=bias is not None,
                             stage_x=stage_x)
    return pl.pallas_call(
        body,
        grid=(m // tm, n // tn),
        in_specs=in_specs,
        out_specs=pl.BlockSpec((tm, tn), lambda i, j: (i, j)),
        out_shape=jax.ShapeDtypeStruct((m, n), out_dtype),
        scratch_shapes=[pltpu.VMEM((tm, k), BF16)] if stage_x else [],
        compiler_params=_params(("parallel", "arbitrary")),
        name=name,
    )(*args)


def _geo(b, t):
    if t == SUBLANES:
        g = min(EW_ROWS // SUBLANES, b)
        return dict(groups=g, steps=b // g, mod_groups=g, steps_per_batch=1, mod_index=lambda i: i)
    assert t % EW_ROWS == 0
    spb = t // EW_ROWS
    return dict(groups=EW_ROWS // SUBLANES, steps=b * spb, mod_groups=1, steps_per_batch=spb,
                mod_index=lambda i: i // spb)


def _mod_spec(mod, mg, d, block_of_step):
    _, part, row0 = mod
    assert row0 % mg == 0, (row0, mg)
    return pl.BlockSpec((mg, 1, d), lambda i: (row0 // mg + block_of_step(i), 0, part))


def _premix_body(x_ref, xprev_ref, first_ref, shift_ref, scale_ref, mu_ref, w1_ref, a1_ref, g1_ref,
                 xmix_ref, hlast_ref, lw_ref, la_ref, lg_ref, *, steps_per_batch):
    i = pl.program_id(0)
    g, s, d = x_ref.shape
    gf = first_ref.shape[0]
    scale = scale_ref[...]
    shift = shift_ref[...]
    h3 = x_ref[...] * (1.0 + scale) + shift
    hlast_ref[...] = h3[g - gf:, s - 1:, :]
    h = h3.reshape(g * s, d)
    if steps_per_batch > 1:
        hprev_row = xprev_ref[:, s - 1:, :] * (1.0 + scale) + shift
        first = jnp.where(i % steps_per_batch == 0, first_ref[...], hprev_row)
        period = g * s
    else:
        first = first_ref[...]
        period = s
    first2 = jnp.broadcast_to(first, (g, s, d)).reshape(g * s, d)
    row = lax.broadcasted_iota(jnp.int32, (g * s, 1), 0)
    hp = jnp.where(row % period == 0, first2, pltpu.roll(h, 1, 0))
    xx = hp - h
    mix = lambda m: (h + xx * mu_ref[m]).astype(BF16)
    for slot, m in enumerate(MIX_RKV):
        xmix_ref[slot] = mix(m)
    lora = lambda m, w_ref: jnp.dot(mix(m), w_ref[...].astype(BF16), preferred_element_type=F32)
    lw_ref[...] = jnp.tanh(lora(MIX_W, w1_ref))
    la_ref[...] = lora(MIX_A, a1_ref)
    lg_ref[...] = jax.nn.sigmoid(lora(MIX_G, g1_ref))


def _premix(x, shift, scale, first, mu, w1, a1, g1):
    b, t, d = x.shape
    n = b * t
    n_mix = mu.shape[0]
    c = _geo(b, t)
    g, mg, midx = c["groups"], c["mod_groups"], c["mod_index"]
    rows = g * SUBLANES
    xg = x.reshape(-1, SUBLANES, d)
    per_batch = pl.BlockSpec((mg, 1, d), lambda i: (midx(i), 0, 0))
    mod_spec = lambda mod: _mod_spec(mod, mg, d, midx)
    whole = lambda a: pl.BlockSpec(a.shape, lambda i: (0,) * a.ndim)
    lora_out = lambda w: (pl.BlockSpec((rows, w.shape[-1]), lambda i: (i, 0)),
                          jax.ShapeDtypeStruct((n, w.shape[-1]), F32))
    lora_specs, lora_shapes = zip(lora_out(w1), lora_out(a1), lora_out(g1))
    return pl.pallas_call(
        functools.partial(_premix_body, steps_per_batch=c["steps_per_batch"]),
        grid=(c["steps"],),
        in_specs=[
            pl.BlockSpec((g, SUBLANES, d), lambda i: (i, 0, 0)),
            pl.BlockSpec((1, SUBLANES, d), lambda i: (jnp.maximum(i * g - 1, 0), 0, 0)),
            per_batch, mod_spec(shift), mod_spec(scale),
            pl.BlockSpec((n_mix, 1, d), lambda i: (0, 0, 0)),
            whole(w1), whole(a1), whole(g1),
        ],
        out_specs=[pl.BlockSpec((len(MIX_RKV), rows, d), lambda i: (0, i, 0)), per_batch,
                   *lora_specs],
        out_shape=[jax.ShapeDtypeStruct((len(MIX_RKV), n, d), BF16),
                   jax.ShapeDtypeStruct((b, 1, d), F32), *lora_shapes],
        compiler_params=_params(("arbitrary",)),
        name="premix",
    )(xg, xg, first, shift[0], scale[0], mu.reshape(n_mix, 1, d), w1, a1, g1)


def _split_bf16(x):
    hi = x.astype(BF16)
    return hi, (x - hi.astype(F32)).astype(BF16)


def _route(h, rw_ref, rb_ref):
    t = h.shape[0]
    h_hi, h_lo = _split_bf16(h)
    w_hi, w_lo = _split_bf16(rw_ref[...])
    logits = (jnp.dot(h_hi, w_hi, preferred_element_type=F32)
              + jnp.dot(h_lo, w_hi, preferred_element_type=F32)
              + jnp.dot(h_hi, w_lo, preferred_element_type=F32))
    s_all = jax.nn.sigmoid(logits)
    sb_all = s_all + rb_ref[...]
    s = [s_all[:, e * N_GROUPS:(e + 1) * N_GROUPS] for e in range(EXPERTS_PER_GROUP)]
    sb = [sb_all[:, e * N_GROUPS:(e + 1) * N_GROUPS] for e in range(EXPERTS_PER_GROUP)]
    hi01, lo01 = jnp.maximum(sb[0], sb[1]), jnp.minimum(sb[0], sb[1])
    hi23, lo23 = jnp.maximum(sb[2], sb[3]), jnp.minimum(sb[2], sb[3])
    top1 = jnp.maximum(hi01, hi23)
    top2 = jnp.maximum(jnp.minimum(hi01, hi23), jnp.maximum(lo01, lo23))
    gscore = top1 + top2
    lane = lax.broadcasted_iota(jnp.int32, (t, N_GROUPS), 1)
    gmax = jnp.max(gscore, axis=-1, keepdims=True)
    gsel = jnp.min(jnp.where(gscore == gmax, lane, N_GROUPS), axis=-1, keepdims=True)
    in_group = lane == gsel
    masked = [jnp.where(in_group, v, NEG_INF) for v in sb]
    idx = [lane * EXPERTS_PER_GROUP + e for e in range(EXPERTS_PER_GROUP)]

    def pick(vals):
        best = functools.reduce(jnp.maximum, [jnp.max(v, axis=-1, keepdims=True) for v in vals])
        return functools.reduce(jnp.minimum, [
            jnp.min(jnp.where(v == best, ix, N_EXPERTS), axis=-1, keepdims=True)
            for v, ix in zip(vals, idx)])

    def weight(sel):
        return functools.reduce(jnp.add, [
            jnp.sum(jnp.where(ix == sel, v, 0.0), axis=-1, keepdims=True) for v, ix in zip(s, idx)])

    e1 = pick(masked)
    e2 = pick([jnp.where(ix == e1, -jnp.inf, v) for v, ix in zip(masked, idx)])
    w1, w2 = weight(e1), weight(e2)
    wsum = w1 + w2
    return jnp.where(lane == 0, e1.astype(F32),
                     jnp.where(lane == 1, e2.astype(F32),
                               jnp.where(lane == 2, w1 / wsum, jnp.where(lane == 3, w2 / wsum, 0.0))))


def _ln_mod_body(*refs, n_planes, n_mods, with_route, steps):
    if with_route:
        @pl.when(pl.program_id(0) < steps)
        def _():
            _ln_mod_compute(*refs, n_planes=n_planes, n_mods=n_mods, with_route=True)

        @pl.when(pl.program_id(0) >= steps)
        def _():
            h_ref = refs[-2]
            h_ref[...] = jnp.zeros_like(h_ref)
    else:
        _ln_mod_compute(*refs, n_planes=n_planes, n_mods=n_mods, with_route=False)


def _ln_mod_compute(*refs, n_planes, n_mods, with_route):
    it = iter(refs)
    x_ref = next(it)
    o_refs = [next(it) for _ in range(n_planes)]
    gate_ref, lng_ref, lnb_ref = next(it), next(it), next(it)
    mod_refs = [(next(it), next(it)) for _ in range(n_mods)]
    rw_ref = next(it) if with_route else None
    rb_ref = next(it) if with_route else None
    if with_route:
        next(it)
    xnew_ref = next(it)
    h_refs = [next(it) for _ in range(n_mods)]
    route_ref = next(it) if with_route else None

    g, s, d = x_ref.shape

    out = o_refs[0][...]
    for o_ref in o_refs[1:]:
        out = out + o_ref[...]
    y = ALPHA * x_ref[...] + gate_ref[...] * out
    mu = jnp.mean(y, axis=-1, keepdims=True)
    yc = y - mu
    var = jnp.mean(yc * yc, axis=-1, keepdims=True)
    xn = yc * lax.rsqrt(var + LN_EPS) * lng_ref[...] + lnb_ref[...]
    xnew_ref[...] = xn
    for k, ((shift_ref, scale_ref), h_ref) in enumerate(zip(mod_refs, h_refs)):
        h = (xn * (1.0 + scale_ref[...]) + shift_ref[...]).reshape(g * s, d)
        h_ref[...] = h.astype(h_ref.dtype)
        if with_route:
            route_ref[...] = _route(h, rw_ref, rb_ref)


def _ln_mod(x, planes, gate, ln_g, ln_b, mods, router=None, moe_rows=None):
    b, t, d = x.shape
    n = b * t
    c = _geo(b, t)
    g, mg, midx = c["groups"], c["mod_groups"], c["mod_index"]
    rows = g * SUBLANES
    with_route = router is not None
    steps = c["steps"]
    tail_steps = 0
    if with_route and moe_rows[0] is None:
        assert moe_rows[2] == 0 and (moe_rows[1] - n) % rows == 0
        tail_steps = (moe_rows[1] - n) // rows
    cl = lambda i: jnp.minimum(i, steps - 1)
    tok = pl.BlockSpec((g, SUBLANES, d), lambda i: (cl(i), 0, 0))
    mod_spec = lambda mod: _mod_spec(mod, mg, d, lambda i: midx(cl(i)))
    const = pl.BlockSpec((1, 1, d), lambda i: (0, 0, 0))
    in_specs = [tok]
    args = [x.reshape(-1, SUBLANES, d)]
    for arr, row0 in planes:
        assert row0 % rows == 0, (row0, rows)
        in_specs.append(pl.BlockSpec((g, SUBLANES, d), lambda i, off=row0 // rows: (off + cl(i), 0, 0)))
        args.append(arr.reshape(-1, SUBLANES, d))
    in_specs += [mod_spec(gate), const, const]
    args += [gate[0], ln_g.reshape(1, 1, d), ln_b.reshape(1, 1, d)]
    for shift, scale in mods:
        in_specs += [mod_spec(shift), mod_spec(scale)]
        args += [shift[0], scale[0]]
    out_shape = [jax.ShapeDtypeStruct((n // SUBLANES, SUBLANES, d), F32)]
    out_specs = [tok]
    aliases = {}
    if with_route:
        in_specs += [pl.BlockSpec(a.shape, lambda i: (0, 0)) for a in router]
        args += list(router)
        buf, total, row0 = moe_rows
        if buf is None:
            in_specs.append(pl.BlockSpec((1, 128), lambda i: (0, 0)))
            args.append(jnp.zeros((1, 128), F32))
        else:
            in_specs.append(pl.BlockSpec(memory_space=pl.ANY))
            args.append(buf)
            aliases = {len(args) - 1: 1}
        out_shape.append(jax.ShapeDtypeStruct((total, d), F32))
        out_specs.append(pl.BlockSpec((rows, d), lambda i, off=row0 // rows: (off + i, 0)))
        out_shape.append(jax.ShapeDtypeStruct((n, N_GROUPS), F32))
        out_specs.append(pl.BlockSpec((rows, N_GROUPS), lambda i: (cl(i), 0)))
    else:
        for _ in mods:
            out_shape.append(jax.ShapeDtypeStruct((n, d), BF16))
            out_specs.append(pl.BlockSpec((rows, d), lambda i: (i, 0)))
    outs = pl.pallas_call(
        functools.partial(_ln_mod_body, n_planes=len(planes), n_mods=len(mods), with_route=with_route,
                          steps=steps),
        grid=(steps + tail_steps,),
        in_specs=in_specs,
        out_specs=out_specs,
        out_shape=out_shape,
        input_output_aliases=aliases,
        compiler_params=_params(("arbitrary",)),
        name="ln_mod",
    )(*args)
    return [outs[0].reshape(b, t, d)] + list(outs[1:])


def _group_consts():
    lane = np.arange(LANE_GROUP)
    bd = (lane[:, None] // RW_HEAD == lane[None, :] // RW_HEAD).astype(np.float32)
    eye = (np.arange(RW_HEAD)[:, None] == lane[None, :] % RW_HEAD).astype(np.float32)
    return jnp.asarray(bd, BF16), jnp.asarray(eye, F32)


def _head_sum(x, bd):
    hi = x.astype(BF16)
    lo = (x - hi.astype(F32)).astype(BF16)
    return jnp.dot(hi, bd, preferred_element_type=F32) + jnp.dot(lo, bd, preferred_element_type=F32)


def _head_sum_rows(x, bd):
    d = x.shape[-1]
    return jnp.concatenate(
        [_head_sum(x[:, c:c + LANE_GROUP], bd) for c in range(0, d, LANE_GROUP)], axis=-1)


def _decay_and_rate(lw, la, w2_ref, w0_ref, a2_ref, a0_ref):
    second = lambda x, w_ref: jnp.dot(x.astype(BF16), w_ref[...].astype(BF16), preferred_element_type=F32)
    return (_act(second(lw, w2_ref) + w0_ref[...], "decay"),
            jax.nn.sigmoid(second(la, a2_ref) + a0_ref[...]))


def _scan_body(*refs, has_s0):
    it = iter(refs)
    r_ref, k_ref, v_ref, lw_ref, la_ref = (next(it) for _ in range(5))
    w2_ref, w0_ref, a2_ref, a0_ref = (next(it) for _ in range(4))
    kk_ref, ka_ref, bd_ref, eye_ref = next(it), next(it), next(it), next(it)
    s0_ref = next(it) if has_s0 else None
    y_ref, s_out_ref = next(it), next(it)
    st_ref, d_ref, kn_ref, al_ref, km_ref, yw_ref, vk_ref, lhs_ref = (next(it) for _ in range(8))

    nb, tt, d = r_ref.shape
    ng = st_ref.shape[1]

    def head_lanes(m):
        return pl.ds(m * RW_HEAD, RW_HEAD)

    @pl.when(pl.program_id(1) == 0)
    def _():
        if has_s0:
            for n in range(nb):
                for g in range(ng):
                    for m in range(HEADS_PER_GROUP):
                        st_ref[n, g, :, head_lanes(m)] = s0_ref[n, g * HEADS_PER_GROUP + m]
        else:
            st_ref[...] = jnp.zeros_like(st_ref)

    bd = bd_ref[...]
    eye = eye_ref[...]
    eye16 = eye.astype(BF16)
    k = k_ref[...].reshape(nb * tt, d)
    r = r_ref[...].reshape(nb * tt, d)
    dcy, a = _decay_and_rate(lw_ref[...].reshape(nb * tt, -1), la_ref[...].reshape(nb * tt, -1),
                             w2_ref, w0_ref, a2_ref, a0_ref)
    d_ref[...] = dcy.reshape(nb, tt, d)
    kk = k * kk_ref[...]
    kn = kk * lax.rsqrt(_head_sum_rows(kk * kk, bd) + 1e-12)
    al = kn * a
    km = k * (1.0 + (a - 1.0) * ka_ref[...])
    kn_ref[...] = kn.reshape(nb, tt, d)
    al_ref[...] = al.reshape(nb, tt, d)
    km_ref[...] = km.reshape(nb, tt, d)
    yw = dcy * r - kn * _head_sum_rows(al * r, bd)
    yw_ref[...] = yw.reshape(nb, tt, d)
    vk_ref[...] = v_ref[...] * _head_sum_rows(km * r, bd).reshape(nb, tt, d)
    packed_rows = 2 * SUBLANES

    def batch_step(t, n, lhs):
        def row(ref, g):
            return ref[n, pl.ds(t, 1), pl.ds(g * LANE_GROUP, LANE_GROUP)]

        def row16(ref, g):
            one = jnp.broadcast_to(row(ref, g), (packed_rows, LANE_GROUP)).astype(BF16)
            return jnp.concatenate([one] * (RW_HEAD // packed_rows), axis=0)

        for g in range(ng):
            q = g * 3 * RW_HEAD
            s16 = st_ref[n, g].astype(BF16)
            lhs[pl.ds(q, RW_HEAD), :] = s16 * row16(kn_ref, g)
            lhs[pl.ds(q + RW_HEAD, RW_HEAD), :] = s16 * row16(yw_ref, g)
            lhs[pl.ds(q + 2 * RW_HEAD, RW_HEAD), :] = eye16 * row16(v_ref, g)
        res = jnp.dot(lhs[...], bd, preferred_element_type=F32)
        for g in range(ng):
            q = g * 3 * RW_HEAD
            skk = res[q:q + RW_HEAD]
            ysum = res[q + RW_HEAD:q + 2 * RW_HEAD]
            vcol = res[q + 2 * RW_HEAD:q + 3 * RW_HEAD]
            st_ref[n, g] = (st_ref[n, g] * row(d_ref, g) - skk * row(al_ref, g)
                            + vcol * row(km_ref, g))
            y_ref[n, pl.ds(t, 1), pl.ds(g * LANE_GROUP, LANE_GROUP)] = (
                jnp.sum(ysum * eye, axis=0, keepdims=True) + row(vk_ref, g))

    n_sub = lhs_ref.shape[0]

    def steps(i, carry):
        for u in range(n_sub):
            for n in range(nb):
                batch_step(i * n_sub + u, n, lhs_ref.at[u, n])
        return carry

    lax.fori_loop(0, tt // n_sub, steps, 0)

    @pl.when(pl.program_id(1) == pl.num_programs(1) - 1)
    def _():
        for n in range(nb):
            for g in range(ng):
                for m in range(HEADS_PER_GROUP):
                    s_out_ref[n, g * HEADS_PER_GROUP + m] = st_ref[n, g, :, head_lanes(m)]


def _rwkv_scan(r, k, v, lw, la, w2, w0, a2, a0, kk, ka, s0, b, t):
    n, d = r.shape
    ng = d // LANE_GROUP
    bd, eye = _group_consts()
    if t > SCAN_TT:
        nb, tt = b, SCAN_TT
    else:
        nb, tt = min(SCAN_NB, b), t
    tok = pl.BlockSpec((nb, tt, d), lambda i, c: (i, c, 0))
    low = lambda x: pl.BlockSpec((nb, tt, x.shape[-1]), lambda i, c: (i, c, 0))
    const2 = lambda shape: pl.BlockSpec(shape, lambda i, c: (0, 0))
    n_heads = d // RW_HEAD
    state = pl.BlockSpec((nb, n_heads, RW_HEAD, RW_HEAD), lambda i, c: (i, 0, 0, 0))
    in_specs = [tok] * 3 + [low(lw), low(la), const2(w2.shape), const2((1, d)), const2(a2.shape),
                            const2((1, d)), const2((1, d)), const2((1, d)), const2(bd.shape),
                            const2(eye.shape)]
    args = [x.reshape(b, t, x.shape[-1]) for x in (r, k, v, lw, la)]
    args += [w2, w0.reshape(1, d), a2, a0.reshape(1, d), kk.reshape(1, d), ka.reshape(1, d), bd, eye]
    if s0 is not None:
        in_specs.append(state)
        args.append(s0)
    rows = nb * ng * RW_HEAD
    y, st = pl.pallas_call(
        functools.partial(_scan_body, has_s0=s0 is not None),
        grid=(b // nb, t // tt),
        in_specs=in_specs,
        out_specs=[tok, state],
        out_shape=[jax.ShapeDtypeStruct((b, t, d), F32),
                   jax.ShapeDtypeStruct((b, n_heads, RW_HEAD, RW_HEAD), F32)],
        scratch_shapes=[pltpu.VMEM((nb, ng, RW_HEAD, LANE_GROUP), F32)]
        + [pltpu.VMEM((nb, tt, d), F32)] * 6
        + [pltpu.VMEM((SCAN_UNROLL, nb, 3 * ng * RW_HEAD, LANE_GROUP), BF16)],
        compiler_params=_params(("parallel", "arbitrary")),
        name="rwkv_scan",
    )(*args)
    return y.reshape(n, d), st


def _rwkv_post_body(y_ref, r_ref, k_ref, v_ref, la_ref, lg_ref, a2_ref, a0_ref, g2_ref, lnw_ref,
                    lnb_ref, rk_ref, ka_ref, bd_ref, z_ref):
    bd = bd_ref[...]
    y = y_ref[...]
    inv = 1.0 / RW_HEAD
    mean = _head_sum_rows(y, bd) * inv
    yc = y - mean
    var = _head_sum_rows(yc * yc, bd) * inv
    yn = yc * lax.rsqrt(var + RW_LN_EPS) * lnw_ref[...] + lnb_ref[...]
    second = lambda x, w_ref: jnp.dot(x.astype(BF16), w_ref[...].astype(BF16), preferred_element_type=F32)
    a = jax.nn.sigmoid(second(la_ref[...], a2_ref) + a0_ref[...])
    gate = second(lg_ref[...], g2_ref)
    km = k_ref[...] * (1.0 + (a - 1.0) * ka_ref[...])
    bonus = _head_sum_rows(r_ref[...] * km * rk_ref[...], bd) * v_ref[...]
    z_ref[...] = ((yn + bonus) * gate).astype(BF16)


def _rwkv_post(y, r, k, v, la, lg, a2, a0, g2, lnw, lnb, rk, ka):
    n, d = y.shape
    bd, _ = _group_consts()
    rows = min(EW_ROWS, n)
    tok = pl.BlockSpec((rows, d), lambda i: (i, 0))
    low = lambda x: pl.BlockSpec((rows, x.shape[-1]), lambda i: (i, 0))
    vec = pl.BlockSpec((1, d), lambda i: (0, 0))
    whole = lambda x: pl.BlockSpec(x.shape, lambda i: (0, 0))
    return pl.pallas_call(
        _rwkv_post_body,
        grid=(n // rows,),
        in_specs=[tok] * 4 + [low(la), low(lg), whole(a2), vec, whole(g2)] + [vec] * 4 + [whole(bd)],
        out_specs=tok,
        out_shape=jax.ShapeDtypeStruct((n, d), BF16),
        compiler_params=_params(("parallel",)),
        name="rwkv_post",
    )(y, r, k, v, la, lg, a2, a0.reshape(1, d), g2, lnw.reshape(1, d), lnb.reshape(1, d),
      rk.reshape(1, d), ka.reshape(1, d), bd)


def _attn_unit(q, k_prev, v_prev, k_cur, v_cur, sink_ref, prev_limit, o_ref, row0):
    tq = q.shape[0]
    kvh = k_cur.shape[-1] // ATT_HEAD_DIM
    n_heads = kvh * GQA
    qi = lax.broadcasted_iota(jnp.int32, (tq, 1), 0)
    jp = lax.broadcasted_iota(jnp.int32, (1, WINDOW), 1)
    jc = lax.broadcasted_iota(jnp.int32, (1, tq), 1)
    diff_p = WINDOW + qi - jp
    diff_c = qi - jc
    valid_p = diff_p < prev_limit
    valid_c = diff_c >= 0
    diff_pf, diff_cf = diff_p.astype(F32), diff_c.astype(F32)
    ones_p = (lax.broadcasted_iota(jnp.int32, (WINDOW, ATT_HEAD_DIM), 1) == 0).astype(F32)
    ones_c = ones_p[:tq]
    nt = (((1,), (1,)), ((), ()))
    scale = ATT_HEAD_DIM ** -0.5
    for kh in range(kvh):
        heads = [kh * GQA + h for h in range(GQA)]
        qs = (jnp.concatenate([q[:, h * ATT_HEAD_DIM:(h + 1) * ATT_HEAD_DIM] for h in heads], axis=0)
              * scale).astype(BF16)
        sl = slice(kh * ATT_HEAD_DIM, (kh + 1) * ATT_HEAD_DIM)
        kp, kc = k_prev[:, sl].astype(BF16), k_cur[:, sl].astype(BF16)
        vp = jnp.concatenate([v_prev[:, sl], ones_p], axis=-1).astype(BF16)
        vc = jnp.concatenate([v_cur[:, sl], ones_c], axis=-1).astype(BF16)
        s_p_all = lax.dot_general(qs, kp, nt, preferred_element_type=F32)
        s_c_all = lax.dot_general(qs, kc, nt, preferred_element_type=F32)
        p_p, p_c, m_all = [], [], []
        for hq, h in enumerate(heads):
            slope = 2.0 ** (-8.0 * (h + 1) / n_heads)
            s_p = jnp.where(valid_p, s_p_all[hq * tq:(hq + 1) * tq] - slope * diff_pf, NEG_INF)
            s_c = jnp.where(valid_c, s_c_all[hq * tq:(hq + 1) * tq] - slope * diff_cf, NEG_INF)
            if tq == WINDOW:
                m = jnp.max(jnp.maximum(s_p, s_c), axis=-1, keepdims=True)
            else:
                m = jnp.maximum(jnp.max(s_p, axis=-1, keepdims=True),
                                jnp.max(s_c, axis=-1, keepdims=True))
            m = jnp.maximum(m, sink_ref[h])
            p_p.append(jnp.exp(s_p - m))
            p_c.append(jnp.exp(s_c - m))
            m_all.append(m)
        o_all = (jnp.dot(jnp.concatenate(p_p, axis=0).astype(BF16), vp, preferred_element_type=F32)
                 + jnp.dot(jnp.concatenate(p_c, axis=0).astype(BF16), vc, preferred_element_type=F32))
        for hq, h in enumerate(heads):
            o_h = o_all[hq * tq:(hq + 1) * tq]
            den = o_h[:, ATT_HEAD_DIM:ATT_HEAD_DIM + 1] + jnp.exp(sink_ref[h] - m_all[hq])
            o_ref[pl.ds(row0, tq), pl.ds(h * ATT_HEAD_DIM, ATT_HEAD_DIM)] = (
                o_h[:, :ATT_HEAD_DIM] / den).astype(o_ref.dtype)


def _attn_prefill_body(q_ref, kvc_ref, kvp_ref, sink_ref, o_ref):
    kvd = kvc_ref.shape[-1] // 2
    kvc, kvp = kvc_ref[...], kvp_ref[...]
    prev_limit = jnp.where(pl.program_id(1) > 0, WINDOW, 0)
    _attn_unit(q_ref[...], kvp[:, :kvd], kvp[:, kvd:], kvc[:, :kvd], kvc[:, kvd:], sink_ref,
               prev_limit, o_ref, 0)


def _attn_decode_body(q_ref, kvc_ref, kc_ref, vc_ref, sink_ref, o_ref, *, nb, tq):
    kvd = kvc_ref.shape[-1] // 2
    kvh = kvd // ATT_HEAD_DIM
    n_heads = kvh * GQA
    rows = n_heads * tq
    head = lax.broadcasted_iota(jnp.int32, (rows, 1), 0) // tq
    qi = lax.broadcasted_iota(jnp.int32, (rows, 1), 0) % tq
    slope = jnp.exp2(-8.0 * (head.astype(F32) + 1.0) / n_heads)
    sink = sink_ref[...]

    def key_axis(n_keys):
        c = lax.broadcasted_iota(jnp.int32, (1, kvh * n_keys), 1)
        return c // n_keys, c % n_keys

    kh_p, jp = key_axis(WINDOW)
    kh_c, jc = key_axis(tq)
    own_p = kh_p == head // GQA
    own_c = kh_c == head // GQA
    diff_p = WINDOW + qi - jp
    diff_c = qi - jc
    valid_p = jnp.where(own_p, diff_p, WINDOW) < WINDOW
    valid_c = jnp.where(own_c, diff_c, -1) >= 0
    bias_p = jnp.where(valid_p, -slope * diff_p.astype(F32), NEG_INF)
    bias_c = jnp.where(valid_c, -slope * diff_c.astype(F32), NEG_INF)
    ones = (lax.broadcasted_iota(jnp.int32, (WINDOW, ATT_HEAD_DIM), 1) == 0).astype(F32)
    nt = (((1,), (1,)), ((), ()))
    scale = ATT_HEAD_DIM ** -0.5
    head_cols = lambda x, h: x[:, h * ATT_HEAD_DIM:(h + 1) * ATT_HEAD_DIM]

    def stack_keys(x):
        return jnp.concatenate([head_cols(x, kh) for kh in range(kvh)], axis=0).astype(BF16)

    def stack_values(x):
        one = ones[:x.shape[0]]
        return jnp.concatenate(
            [jnp.concatenate([head_cols(x, kh), one], axis=-1) for kh in range(kvh)], axis=0).astype(BF16)

    for n in range(nb):
        q = q_ref[pl.ds(n * tq, tq), :]
        kvc = kvc_ref[pl.ds(n * tq, tq), :]
        qs = (jnp.concatenate([head_cols(q, h) for h in range(n_heads)], axis=0) * scale).astype(BF16)
        s_p = lax.dot_general(qs, stack_keys(kc_ref[n]), nt, preferred_element_type=F32) + bias_p
        s_c = lax.dot_general(qs, stack_keys(kvc[:, :kvd]), nt, preferred_element_type=F32) + bias_c
        m = jnp.maximum(jnp.maximum(jnp.max(s_p, axis=-1, keepdims=True),
                                    jnp.max(s_c, axis=-1, keepdims=True)), sink)
        o_all = (jnp.dot(jnp.exp(s_p - m).astype(BF16), stack_values(vc_ref[n]),
                         preferred_element_type=F32)
                 + jnp.dot(jnp.exp(s_c - m).astype(BF16), stack_values(kvc[:, kvd:]),
                           preferred_element_type=F32))
        den = o_all[:, ATT_HEAD_DIM:ATT_HEAD_DIM + 1] + jnp.exp(sink - m)
        o = o_all[:, :ATT_HEAD_DIM] / den
        for h in range(n_heads):
            o_ref[pl.ds(n * tq, tq), pl.ds(h * ATT_HEAD_DIM, ATT_HEAD_DIM)] = o[h * tq:(h + 1) * tq]


def _attn_prefill(q, kv, sinks, b, t):
    n, dq = q.shape
    kv2 = kv.shape[-1]
    nblk = t // WINDOW
    return pl.pallas_call(
        _attn_prefill_body,
        grid=(b, nblk),
        in_specs=[
            pl.BlockSpec((WINDOW, dq), lambda i, j: (i * nblk + j, 0)),
            pl.BlockSpec((WINDOW, kv2), lambda i, j: (i * nblk + j, 0)),
            pl.BlockSpec((WINDOW, kv2), lambda i, j: (i * nblk + jnp.maximum(j - 1, 0), 0)),
            pl.BlockSpec(memory_space=pltpu.SMEM),
        ],
        out_specs=pl.BlockSpec((WINDOW, dq), lambda i, j: (i * nblk + j, 0)),
        out_shape=jax.ShapeDtypeStruct((n, dq), BF16),
        compiler_params=_params(("parallel", "arbitrary")),
        name="attn_prefill",
    )(q, kv, kv, sinks.astype(F32))


def _attn_decode(q, kv, cache_k, cache_v, sinks, b, t):
    n, dq = q.shape
    kv2 = kv.shape[-1]
    kvd = kv2 // 2
    nb = min(ATT_NB, b)
    rows = nb * t
    return pl.pallas_call(
        functools.partial(_attn_decode_body, nb=nb, tq=t),
        grid=(b // nb,),
        in_specs=[
            pl.BlockSpec((rows, dq), lambda i: (i, 0)),
            pl.BlockSpec((rows, kv2), lambda i: (i, 0)),
            pl.BlockSpec((nb, WINDOW, kvd), lambda i: (i, 0, 0)),
            pl.BlockSpec((nb, WINDOW, kvd), lambda i: (i, 0, 0)),
            pl.BlockSpec((sinks.shape[0] * t, 1), lambda i: (0, 0)),
        ],
        out_specs=pl.BlockSpec((rows, dq), lambda i: (i, 0)),
        out_shape=jax.ShapeDtypeStruct((n, dq), F32),
        compiler_params=_params(("parallel",)),
        name="attn_decode",
    )(q, kv, cache_k.reshape(b, WINDOW, kvd), cache_v.reshape(b, WINDOW, kvd),
      jnp.repeat(sinks.astype(F32), t)[:, None])


def _moe_body(te_ref, tv_ref, tok_ref, dst_ref, h_hbm, gate_ref, wg_ref, wu_ref, wd_ref, out_hbm,
              xbuf, obuf, gsem, ssem, wg16, wu16, wd16):
    t = pl.program_id(0)
    last = pl.num_programs(0) - 1
    slot = t % 2

    def split(row):
        return lax.shift_right_logical(row, 3), row & (SUBLANES - 1)

    def gather_of(buf):
        def copy(i, u, tok):
            hi, lo = split(tok)
            return pltpu.make_async_copy(h_hbm.at[hi, pl.ds(lo, 1)], xbuf.at[buf, i, pl.ds(u, 1)],
                                         gsem.at[buf])
        return tok_ref, copy

    def scatter_copy(i, u, dst):
        hi, lo = split(dst)
        return pltpu.make_async_copy(obuf.at[i, pl.ds(u, 1)], out_hbm.at[hi, pl.ds(lo, 1)], ssem.at[0])

    scatter = (dst_ref, scatter_copy)

    def for_rows(tile, rows, act):
        table, build = rows
        groups = lax.shift_right_logical(tv_ref[tile] + (SUBLANES - 1), 3)

        def body(i, carry):
            base = tile * MOE_TG + i * SUBLANES
            entries = [table[base + u] for u in range(SUBLANES)]
            for u in range(SUBLANES):
                act(build(i, u, entries[u]))
            return carry
        lax.fori_loop(0, groups, body, 0)

    start = lambda copy: copy.start()
    wait = lambda copy: copy.wait()

    groups_per_tile = MOE_TG // SUBLANES

    @pl.when(t == 0)
    def _():
        xbuf[...] = jnp.zeros_like(xbuf)
        obuf[...] = jnp.zeros_like(obuf)
        spare = pltpu.make_async_copy(
            obuf, out_hbm.at[pl.ds(out_hbm.shape[0] - groups_per_tile, groups_per_tile)], ssem.at[0])
        spare.start()
        spare.wait()
        for_rows(0, gather_of(0), start)

    nxt = jnp.minimum(t + 1, last)

    @pl.when(t < last)
    def _():
        for_rows(nxt, gather_of(1 - slot), start)

    prev = jnp.maximum(t - 1, 0)

    @pl.when((t == 0) | (te_ref[t] != te_ref[prev]))
    def _():
        wg16[...] = wg_ref[...].astype(BF16)
        wu16[...] = wu_ref[...].astype(BF16)
        wd16[...] = wd_ref[...].astype(BF16)

    @pl.when(tv_ref[t] > 0)
    def _():
        for_rows(t, gather_of(slot), wait)
        x = xbuf[slot].reshape(MOE_TG, -1).astype(BF16)
        hg = jnp.dot(x, wg16[...], preferred_element_type=F32)
        hu = jnp.dot(x, wu16[...], preferred_element_type=F32)
        act = (hg * jax.nn.sigmoid(hg)) * hu * gate_ref[...]
        o = jnp.dot(act.astype(BF16), wd16[...], preferred_element_type=F32)

        @pl.when(t > 0)
        def _():
            for_rows(prev, scatter, wait)

        obuf[...] = o.reshape(obuf.shape)
        for_rows(t, scatter, start)

        @pl.when(t == last)
        def _():
            for_rows(t, scatter, wait)

    @pl.when((tv_ref[t] == 0) & (t > 0))
    def _():
        for_rows(prev, scatter, wait)


def _moe_plan(route, n_tiles):
    n = route.shape[0]
    e = route[:, :2].astype(jnp.int32).reshape(-1)
    w = route[:, 2:4].reshape(-1)
    onehot = (e[:, None] == jnp.arange(N_EXPERTS, dtype=jnp.int32)[None, :]).astype(jnp.int32)
    csum = jnp.cumsum(onehot, axis=0)
    rank = jnp.take_along_axis(csum, e[:, None], axis=1)[:, 0] - 1
    counts = csum[-1]
    padded = ((counts + MOE_TG - 1) // MOE_TG) * MOE_TG
    pend = jnp.cumsum(padded)
    dest = (pend - padded)[e] + rank
    p_rows = n_tiles * MOE_TG
    vals = jnp.stack([jnp.arange(1, 2 * n + 1, dtype=jnp.int32), lax.bitcast_convert_type(w, jnp.int32)],
                     axis=1)
    placed = jnp.zeros((p_rows, 2), jnp.int32).at[dest].set(vals)
    has_pair = placed[:, 0] > 0
    pair = placed[:, 0] - 1
    row_token = jnp.where(has_pair, pair // 2, 0)
    spare = 2 * n + jnp.arange(p_rows, dtype=jnp.int32) % MOE_TG
    row_dst = jnp.where(has_pair, (pair % 2) * n + pair // 2, spare)
    row_gate = jnp.where(has_pair, lax.bitcast_convert_type(placed[:, 1], F32), 0.0)
    starts = jnp.arange(n_tiles, dtype=jnp.int32) * MOE_TG
    tile_expert = jnp.minimum(jnp.searchsorted(pend, starts, side="right"), N_EXPERTS - 1)
    tile_rows = jnp.clip(counts[tile_expert] - (starts - (pend - padded)[tile_expert]), 0, MOE_TG)
    tile_rows = jnp.where(starts < pend[-1], tile_rows, 0).astype(jnp.int32)
    last_used = jnp.max(jnp.where(tile_rows > 0, tile_expert, 0))
    tile_expert = jnp.where(tile_rows > 0, tile_expert, last_used).astype(jnp.int32)
    return row_token, row_dst, row_gate, tile_expert, tile_rows


def _moe(h, route, w_gate, w_up, w_down, layer):
    n, d = h.shape
    group = (SUBLANES, d)
    f = w_gate.shape[-1]
    n_tiles = (2 * n) // MOE_TG + N_EXPERTS
    n_out = 2 * n + MOE_TG
    assert n % SUBLANES == 0
    row_token, row_dst, row_gate, tile_expert, tile_rows = _moe_plan(route, n_tiles)
    out = pl.pallas_call(
        _moe_body,
        grid_spec=pltpu.PrefetchScalarGridSpec(
            num_scalar_prefetch=4,
            grid=(n_tiles,),
            in_specs=[
                pl.BlockSpec(memory_space=pl.ANY),
                pl.BlockSpec((MOE_TG, 1), lambda t, te, tv, rt, rd: (t, 0)),
                pl.BlockSpec((None, None, d, f), lambda t, te, tv, rt, rd: (layer, te[t], 0, 0)),
                pl.BlockSpec((None, None, d, f), lambda t, te, tv, rt, rd: (layer, te[t], 0, 0)),
                pl.BlockSpec((None, None, f, d), lambda t, te, tv, rt, rd: (layer, te[t], 0, 0)),
            ],
            out_specs=pl.BlockSpec(memory_space=pl.ANY),
            scratch_shapes=[pltpu.VMEM((2, MOE_TG // SUBLANES) + group, F32),
                            pltpu.VMEM((MOE_TG // SUBLANES,) + group, F32),
                            pltpu.SemaphoreType.DMA((2,)), pltpu.SemaphoreType.DMA((1,)),
                            pltpu.VMEM((d, f), BF16), pltpu.VMEM((d, f), BF16),
                            pltpu.VMEM((f, d), BF16)],
        ),
        out_shape=jax.ShapeDtypeStruct((n_out // SUBLANES,) + group, F32),
        compiler_params=_params(("arbitrary",)),
        name="moe_ffn",
    )(tile_expert, tile_rows, row_token, row_dst, h.reshape((n // SUBLANES,) + group),
      row_gate.reshape(-1, 1), w_gate, w_up, w_down)
    return out.reshape(n_out, d)


def _split_mod(mod, n_parts, trunk_rows):
    mod = mod[:, None, :]
    return [[(mod, k, row0) for k in range(n_parts)] for row0 in trunk_rows]


def kernel(x_prompt, x_sample, c_prompt, c_sample, state_wkv, state_shift, cache_k_win, cache_v_win, ada_w, ada_b, ln_g, ln_b, rw_mu, rw_w_rkv, rw_w0, rw_w1, rw_w2, rw_a0, rw_a1, rw_a2, rw_g1, rw_g2, rw_k_k, rw_k_a, rw_r_k, rw_lnx_w, rw_lnx_b, rw_wo, ada_kv_w, ada_kv_b, w_kv, w_q, attn_sinks, w_o_attn, router_w, router_bias, moe_w_gate, moe_w_up, moe_w_down):
    d = x_prompt.shape[-1]
    xs = [x_prompt, x_sample]
    shapes = [x.shape[:2] for x in xs]
    counts = [b * t for b, t in shapes]
    bp = shapes[0][0]

    c_all = jnp.concatenate([c_sample, c_prompt], axis=0)
    trunk_rows = [shapes[1][0], 0]
    mods = {(l, s): _split_mod(_mm(c_all, ada_w, name="mm_ada", w_lead=(l, s), bias=ada_b[l, s],
                                   in_act="silu"), 3, trunk_rows)
            for l in range(DEPTH) for s in range(2)}
    mods_kv = _split_mod(_mm(c_all, ada_kv_w, name="mm_ada_kv", bias=ada_kv_b, in_act="silu"), 2,
                         trunk_rows)

    by_slot = lambda x: x.reshape(-1, N_GROUPS, EXPERTS_PER_GROUP).swapaxes(1, 2).reshape(-1, N_EXPERTS)
    router = (by_slot(router_w), by_slot(router_bias))

    n_all = counts[0] + counts[1]
    row0 = [0, counts[0]]

    def moe_layer(h_all, routes, layer):
        out = _moe(h_all, jnp.concatenate(routes, axis=0), moe_w_gate, moe_w_up, moe_w_down, layer)
        return [[(out, slot * n_all + row0[tr]) for slot in range(2)] for tr in range(2)]

    firsts = [jnp.zeros((bp, 1, d), F32), state_shift[0][:, None, :]]
    s0s = [None, state_wkv[0]]
    x1, h1, route1, states, hlasts = [], None, [], [], []
    for tr in range(2):
        b, t = shapes[tr]
        shift, scale, gate = mods[0, 0][tr]
        xmix, hlast, lw, la, lg = _premix(xs[tr], shift, scale, firsts[tr], rw_mu[0], rw_w1[0],
                                          rw_a1[0], rw_g1[0])
        r = _mm(xmix, rw_w_rkv, name="mm_r", x_lead=(0,), w_lead=(0, 0), tm=MM_TM_BF16)
        k = _mm(xmix, rw_w_rkv, name="mm_k", x_lead=(1,), w_lead=(0, 1), tm=MM_TM_BF16)
        v = _mm(xmix, rw_w_rkv, name="mm_v", x_lead=(2,), w_lead=(0, 2), tm=MM_TM_BF16)
        y, st = _rwkv_scan(r, k, v, lw, la, rw_w2[0], rw_w0[0], rw_a2[0], rw_a0[0], rw_k_k[0],
                           rw_k_a[0], s0s[tr], b, t)
        z = _rwkv_post(y, r, k, v, la, lg, rw_a2[0], rw_a0[0], rw_g2[0], rw_lnx_w[0], rw_lnx_b[0],
                       rw_r_k[0], rw_k_a[0])
        out = _mm(z, rw_wo, name="mm_wo", w_lead=(0,), tm=MM_TM_BF16)
        sh, sc, _ = mods[0, 1][tr]
        xn, h1, route = _ln_mod(xs[tr], [(out, 0)], gate, ln_g[0, 0], ln_b[0, 0], [(sh, sc)], router,
                                (h1, n_all, row0[tr]))
        x1.append(xn), route1.append(route), states.append(st), hlasts.append(hlast)
    moe1 = moe_layer(h1, route1, 0)

    caches = [None, (cache_k_win, cache_v_win)]
    x3, h3, route3, kvs = [], None, [], []
    for tr in range(2):
        b, t = shapes[tr]
        shq, scq, gate_q = mods[1, 0][tr]
        shk, sck = mods_kv[tr]
        x2, hq, hkv = _ln_mod(x1[tr], moe1[tr], mods[0, 1][tr][2], ln_g[0, 1], ln_b[0, 1],
                              [(shq, scq), (shk, sck)])
        q = _mm(hq, w_q, name="mm_q", w_lead=(0,), tm=MM_TM_BF16)
        kv = _mm(hkv, w_kv, name="mm_kv", tm=MM_TM_BF16)
        if caches[tr] is None:
            o = _attn_prefill(q, kv, attn_sinks[0], b, t)
        else:
            o = _attn_decode(q, kv, caches[tr][0], caches[tr][1], attn_sinks[0], b, t)
        out = _mm(o, w_o_attn, name="mm_o", w_lead=(0,), tm=MM_TM_BF16 if o.dtype == BF16 else MM_TM)
        sh, sc, _ = mods[1, 1][tr]
        xn, h3, route = _ln_mod(x2, [(out, 0)], gate_q, ln_g[1, 0], ln_b[1, 0], [(sh, sc)], router,
                                (h3, n_all, row0[tr]))
        x3.append(xn), route3.append(route), kvs.append(kv)
    moe3 = moe_layer(h3, route3, 1)
    ys = [_ln_mod(x3[tr], moe3[tr], mods[1, 1][tr][2], ln_g[1, 1], ln_b[1, 1], [])[0]
          for tr in range(2)]

    kvd = kvs[0].shape[-1] // 2
    kvh = kvd // ATT_HEAD_DIM
    (bp, tp), (bs, ts) = shapes
    kv_p = kvs[0].reshape(bp, tp, 2 * kvd)[:, tp - WINDOW:]
    p_k_win = kv_p[..., :kvd].reshape(bp, WINDOW, kvh, ATT_HEAD_DIM)
    p_v_win = kv_p[..., kvd:].reshape(bp, WINDOW, kvh, ATT_HEAD_DIM)
    kv_s = kvs[1].reshape(bs, ts, 2 * kvd)
    s_k_win = jnp.concatenate(
        [cache_k_win, kv_s[..., :kvd].reshape(bs, ts, kvh, ATT_HEAD_DIM)], axis=1)[:, -WINDOW:]
    s_v_win = jnp.concatenate(
        [cache_v_win, kv_s[..., kvd:].reshape(bs, ts, kvh, ATT_HEAD_DIM)], axis=1)[:, -WINDOW:]
    p_wkv = states[0][None]
    s_wkv = states[1][None]
    p_shift = hlasts[0].reshape(1, bp, d)
    s_shift = hlasts[1].reshape(1, bs, d)
    return (ys[0], ys[1], p_wkv, p_shift, p_k_win, p_v_win, s_wkv, s_shift, s_k_win, s_v_win)
```

```python
import functools

import jax
import jax.numpy as jnp
import numpy as np
from jax import lax
from jax.experimental import pallas as pl
from jax.experimental.pallas import tpu as pltpu

F32 = jnp.float32
BF16 = jnp.bfloat16

RW_HEAD = 64
ATT_HEAD_DIM = 64
GQA = 8
WINDOW = 128
N_EXPERTS = 32
EXPERTS_PER_GROUP = 4
N_GROUPS = N_EXPERTS // EXPERTS_PER_GROUP
DEPTH = 2
ALPHA = (2.0 * DEPTH) ** 0.25
LN_EPS = 1e-5
RW_LN_EPS = 64e-5
NEG_INF = -1e30
MIX_RKV = (0, 2, 3)
MIX_W, MIX_A, MIX_G = 1, 4, 5

SUBLANES = 8
LANE_GROUP = 256
HEADS_PER_GROUP = LANE_GROUP // RW_HEAD
VMEM_LIMIT = 56 * 1024 * 1024

MM_TM = 1024
MM_TM_BF16 = 2048
MM_TN = 512
EW_ROWS = 256
SCAN_TT = 64
SCAN_NB = 4
SCAN_UNROLL = 8
ATT_NB = 8
MOE_TG = 256


def _params(sem):
    return pltpu.CompilerParams(dimension_semantics=sem, vmem_limit_bytes=VMEM_LIMIT)


def _softplus(z):
    return jnp.maximum(z, 0.0) + jnp.log1p(jnp.exp(-jnp.abs(z)))


def _act(x, kind):
    if kind is None:
        return x
    if kind == "silu":
        return x * jax.nn.sigmoid(x)
    if kind == "tanh":
        return jnp.tanh(x)
    if kind == "sigmoid":
        return jax.nn.sigmoid(x)
    if kind == "decay":
        w = -_softplus(-x) - 0.5
        return jnp.exp(-jnp.exp(w))
    raise ValueError(kind)


def _mm_body(*refs, in_act, out_act, has_bias, stage_x):
    x_ref, w_ref = refs[0], refs[1]
    b_ref = refs[2] if has_bias else None
    o_ref = refs[2 + has_bias]
    if stage_x:
        xs_ref = refs[3 + has_bias]

        @pl.when(pl.program_id(1) == 0)
        def _():
            xs_ref[...] = _act(x_ref[...].astype(F32), in_act).astype(BF16)
    else:
        xs_ref = x_ref

    acc = jnp.dot(xs_ref[...], w_ref[...].astype(BF16), preferred_element_type=F32)
    if has_bias:
        acc = acc + b_ref[...]
    o_ref[...] = _act(acc, out_act).astype(o_ref.dtype)


def _mm(x, w, *, name, x_lead=(), w_lead=(), bias=None, in_act=None, out_act=None, out_dtype=F32,
        tm=MM_TM, tn=MM_TN):
    m, k = x.shape[-2:]
    n = w.shape[-1]
    tm = min(tm, m)
    tn = min(tn, n)
    assert m % tm == 0 and n % tn == 0, (m, tm, n, tn)
    nx, nw = len(x_lead), len(w_lead)
    in_specs = [
        pl.BlockSpec((None,) * nx + (tm, k), lambda i, j: tuple(x_lead) + (i, 0)),
        pl.BlockSpec((None,) * nw + (k, tn), lambda i, j: tuple(w_lead) + (0, j)),
    ]
    args = [x, w]
    if bias is not None:
        in_specs.append(pl.BlockSpec((1, tn), lambda i, j: (0, j)))
        args.append(bias.reshape(1, n).astype(F32))
    stage_x = x.dtype != BF16 or in_act is not None
    body = functools.partial(_mm_body, in_act=in_act, out_act=out_act, has_bias=bias is not None,
                             stage_x=stage_x)
    return pl.pallas_call(
        body,
        grid=(m // tm, n // tn),
        in_specs=in_specs,
        out_specs=pl.BlockSpec((tm, tn), lambda i, j: (i, j)),
        out_shape=jax.ShapeDtypeStruct((m, n), out_dtype),
        scratch_shapes=[pltpu.VMEM((tm, k), BF16)] if stage_x else [],
        compiler_params=_params(("parallel", "arbitrary")),
        name=name,
    )(*args)


def _geo(b, t):
    if t == SUBLANES:
        g = min(EW_ROWS // SUBLANES, b)
        return dict(groups=g, steps=b // g, mod_groups=g, steps_per_batch=1, mod_index=lambda i: i)
    assert t % EW_ROWS == 0
    spb = t // EW_ROWS
    return dict(groups=EW_ROWS // SUBLANES, steps=b * spb, mod_groups=1, steps_per_batch=spb,
                mod_index=lambda i: i // spb)


def _mod_spec(mod, mg, d, block_of_step):
    _, part, row0 = mod
    assert row0 % mg == 0, (row0, mg)
    return pl.BlockSpec((mg, 1, d), lambda i: (row0 // mg + block_of_step(i), 0, part))


def _premix_body(x_ref, xprev_ref, first_ref, shift_ref, scale_ref, mu_ref, w1_ref, a1_ref, g1_ref,
                 xmix_ref, hlast_ref, lw_ref, la_ref, lg_ref, *, steps_per_batch):
    i = pl.program_id(0)
    g, s, d = x_ref.shape
    gf = first_ref.shape[0]
    scale = scale_ref[...]
    shift = shift_ref[...]
    h3 = x_ref[...] * (1.0 + scale) + shift
    hlast_ref[...] = h3[g - gf:, s - 1:, :]
    h = h3.reshape(g * s, d)
    if steps_per_batch > 1:
        hprev_row = xprev_ref[:, s - 1:, :] * (1.0 + scale) + shift
        first = jnp.where(i % steps_per_batch == 0, first_ref[...], hprev_row)
        period = g * s
    else:
        first = first_ref[...]
        period = s
    first2 = jnp.broadcast_to(first, (g, s, d)).reshape(g * s, d)
    row = lax.broadcasted_iota(jnp.int32, (g * s, 1), 0)
    hp = jnp.where(row % period == 0, first2, pltpu.roll(h, 1, 0))
    xx = hp - h
    mix = lambda m: (h + xx * mu_ref[m]).astype(BF16)
    for slot, m in enumerate(MIX_RKV):
        xmix_ref[slot] = mix(m)
    lora = lambda m, w_ref: jnp.dot(mix(m), w_ref[...].astype(BF16), preferred_element_type=F32)
    lw_ref[...] = jnp.tanh(lora(MIX_W, w1_ref))
    la_ref[...] = lora(MIX_A, a1_ref)
    lg_ref[...] = jax.nn.sigmoid(lora(MIX_G, g1_ref))


def _premix(x, shift, scale, first, mu, w1, a1, g1):
    b, t, d = x.shape
    n = b * t
    n_mix = mu.shape[0]
    c = _geo(b, t)
    g, mg, midx = c["groups"], c["mod_groups"], c["mod_index"]
    rows = g * SUBLANES
    xg = x.reshape(-1, SUBLANES, d)
    per_batch = pl.BlockSpec((mg, 1, d), lambda i: (midx(i), 0, 0))
    mod_spec = lambda mod: _mod_spec(mod, mg, d, midx)
    whole = lambda a: pl.BlockSpec(a.shape, lambda i: (0,) * a.ndim)
    lora_out = lambda w: (pl.BlockSpec((rows, w.shape[-1]), lambda i: (i, 0)),
                          jax.ShapeDtypeStruct((n, w.shape[-1]), F32))
    lora_specs, lora_shapes = zip(lora_out(w1), lora_out(a1), lora_out(g1))
    return pl.pallas_call(
        functools.partial(_premix_body, steps_per_batch=c["steps_per_batch"]),
        grid=(c["steps"],),
        in_specs=[
            pl.BlockSpec((g, SUBLANES, d), lambda i: (i, 0, 0)),
            pl.BlockSpec((1, SUBLANES, d), lambda i: (jnp.maximum(i * g - 1, 0), 0, 0)),
            per_batch, mod_spec(shift), mod_spec(scale),
            pl.BlockSpec((n_mix, 1, d), lambda i: (0, 0, 0)),
            whole(w1), whole(a1), whole(g1),
        ],
        out_specs=[pl.BlockSpec((len(MIX_RKV), rows, d), lambda i: (0, i, 0)), per_batch,
                   *lora_specs],
        out_shape=[jax.ShapeDtypeStruct((len(MIX_RKV), n, d), BF16),
                   jax.ShapeDtypeStruct((b, 1, d), F32), *lora_shapes],
        compiler_params=_params(("arbitrary",)),
        name="premix",
    )(xg, xg, first, shift[0], scale[0], mu.reshape(n_mix, 1, d), w1, a1, g1)


def _split_bf16(x):
    hi = x.astype(BF16)
    return hi, (x - hi.astype(F32)).astype(BF16)


def _route(h, rw_ref, rb_ref, count_ref):
    t = h.shape[0]
    h_hi, h_lo = _split_bf16(h)
    w_hi, w_lo = _split_bf16(rw_ref[...])
    logits = (jnp.dot(h_hi, w_hi, preferred_element_type=F32)
              + jnp.dot(h_lo, w_hi, preferred_element_type=F32)
              + jnp.dot(h_hi, w_lo, preferred_element_type=F32))
    s_all = jax.nn.sigmoid(logits)
    sb_all = s_all + rb_ref[...]
    s = [s_all[:, e * N_GROUPS:(e + 1) * N_GROUPS] for e in range(EXPERTS_PER_GROUP)]
    sb = [sb_all[:, e * N_GROUPS:(e + 1) * N_GROUPS] for e in range(EXPERTS_PER_GROUP)]
    hi01, lo01 = jnp.maximum(sb[0], sb[1]), jnp.minimum(sb[0], sb[1])
    hi23, lo23 = jnp.maximum(sb[2], sb[3]), jnp.minimum(sb[2], sb[3])
    top1 = jnp.maximum(hi01, hi23)
    top2 = jnp.maximum(jnp.minimum(hi01, hi23), jnp.maximum(lo01, lo23))
    gscore = top1 + top2
    lane = lax.broadcasted_iota(jnp.int32, (t, N_GROUPS), 1)
    gmax = jnp.max(gscore, axis=-1, keepdims=True)
    gsel = jnp.min(jnp.where(gscore == gmax, lane, N_GROUPS), axis=-1, keepdims=True)
    in_group = lane == gsel
    masked = [jnp.where(in_group, v, NEG_INF) for v in sb]
    idx = [lane * EXPERTS_PER_GROUP + e for e in range(EXPERTS_PER_GROUP)]

    def pick(vals):
        best = functools.reduce(jnp.maximum, [jnp.max(v, axis=-1, keepdims=True) for v in vals])
        return functools.reduce(jnp.minimum, [
            jnp.min(jnp.where(v == best, ix, N_EXPERTS), axis=-1, keepdims=True)
            for v, ix in zip(vals, idx)])

    def weight(sel):
        return functools.reduce(jnp.add, [
            jnp.sum(jnp.where(ix == sel, v, 0.0), axis=-1, keepdims=True) for v, ix in zip(s, idx)])

    e1 = pick(masked)
    e2 = pick([jnp.where(ix == e1, -jnp.inf, v) for v, ix in zip(masked, idx)])
    w1, w2 = weight(e1), weight(e2)
    wsum = w1 + w2
    expert = lax.broadcasted_iota(jnp.int32, (t, N_EXPERTS), 1)
    chosen = jnp.where(expert == e1, 1.0, 0.0) + jnp.where(expert == e2, 1.0, 0.0)
    ri = lax.broadcasted_iota(jnp.int32, (t, t), 0)
    ci = lax.broadcasted_iota(jnp.int32, (t, t), 1)
    earlier = jnp.where(ri > ci, 1.0, 0.0).astype(BF16)
    before = count_ref[...] + jnp.dot(earlier, chosen.astype(BF16), preferred_element_type=F32)
    rank = lambda e: jnp.sum(jnp.where(expert == e, before, 0.0), axis=-1, keepdims=True)
    r1, r2 = rank(e1), rank(e2)
    count_ref[...] = count_ref[...] + jnp.sum(chosen, axis=0, keepdims=True)
    cols = [e1.astype(F32), e2.astype(F32), w1 / wsum, w2 / wsum, r1, r2]
    out = jnp.zeros((t, N_GROUPS), F32)
    for c, col in enumerate(cols):
        out = jnp.where(lane == c, col, out)
    return out


def _ln_mod_body(*refs, n_planes, n_mods, with_route, steps):
    if with_route:
        h_ref, count_ref = refs[-3], refs[-1]

        @pl.when(pl.program_id(0) == 0)
        def _():
            count_ref[...] = jnp.zeros_like(count_ref)

        @pl.when(pl.program_id(0) < steps)
        def _():
            _ln_mod_compute(*refs, n_planes=n_planes, n_mods=n_mods, with_route=True)

        @pl.when(pl.program_id(0) >= steps)
        def _():
            h_ref[...] = jnp.zeros_like(h_ref)
    else:
        _ln_mod_compute(*refs, n_planes=n_planes, n_mods=n_mods, with_route=False)


def _ln_mod_compute(*refs, n_planes, n_mods, with_route):
    it = iter(refs)
    x_ref = next(it)
    o_refs = [next(it) for _ in range(n_planes)]
    gate_ref, lng_ref, lnb_ref = next(it), next(it), next(it)
    mod_refs = [(next(it), next(it)) for _ in range(n_mods)]
    rw_ref = next(it) if with_route else None
    rb_ref = next(it) if with_route else None
    if with_route:
        next(it)
    xnew_ref = next(it)
    h_refs = [next(it) for _ in range(n_mods)]
    route_ref = next(it) if with_route else None
    count_ref = next(it) if with_route else None

    g, s, d = x_ref.shape

    out = o_refs[0][...]
    for o_ref in o_refs[1:]:
        out = out + o_ref[...]
    y = ALPHA * x_ref[...] + gate_ref[...] * out
    mu = jnp.mean(y, axis=-1, keepdims=True)
    yc = y - mu
    var = jnp.mean(yc * yc, axis=-1, keepdims=True)
    xn = yc * lax.rsqrt(var + LN_EPS) * lng_ref[...] + lnb_ref[...]
    xnew_ref[...] = xn
    for k, ((shift_ref, scale_ref), h_ref) in enumerate(zip(mod_refs, h_refs)):
        h = (xn * (1.0 + scale_ref[...]) + shift_ref[...]).reshape(g * s, d)
        h_ref[...] = h.astype(h_ref.dtype)
        if with_route:
            route_ref[...] = _route(h, rw_ref, rb_ref, count_ref)


def _ln_mod(x, planes, gate, ln_g, ln_b, mods, router=None, moe_rows=None):
    b, t, d = x.shape
    n = b * t
    c = _geo(b, t)
    g, mg, midx = c["groups"], c["mod_groups"], c["mod_index"]
    rows = g * SUBLANES
    with_route = router is not None
    steps = c["steps"]
    tail_steps = 0
    if with_route and moe_rows[0] is None:
        assert moe_rows[2] == 0 and (moe_rows[1] - n) % rows == 0
        tail_steps = (moe_rows[1] - n) // rows
    cl = lambda i: jnp.minimum(i, steps - 1)
    tok = pl.BlockSpec((g, SUBLANES, d), lambda i: (cl(i), 0, 0))
    mod_spec = lambda mod: _mod_spec(mod, mg, d, lambda i: midx(cl(i)))
    const = pl.BlockSpec((1, 1, d), lambda i: (0, 0, 0))
    in_specs = [tok]
    args = [x.reshape(-1, SUBLANES, d)]
    for arr, row0 in planes:
        assert row0 % rows == 0, (row0, rows)
        in_specs.append(pl.BlockSpec((g, SUBLANES, d), lambda i, off=row0 // rows: (off + cl(i), 0, 0)))
        args.append(arr.reshape(-1, SUBLANES, d))
    in_specs += [mod_spec(gate), const, const]
    args += [gate[0], ln_g.reshape(1, 1, d), ln_b.reshape(1, 1, d)]
    for shift, scale in mods:
        in_specs += [mod_spec(shift), mod_spec(scale)]
        args += [shift[0], scale[0]]
    out_shape = [jax.ShapeDtypeStruct((n // SUBLANES, SUBLANES, d), F32)]
    out_specs = [tok]
    aliases = {}
    if with_route:
        in_specs += [pl.BlockSpec(a.shape, lambda i: (0, 0)) for a in router]
        args += list(router)
        buf, total, row0 = moe_rows
        if buf is None:
            in_specs.append(pl.BlockSpec((1, 128), lambda i: (0, 0)))
            args.append(jnp.zeros((1, 128), F32))
        else:
            in_specs.append(pl.BlockSpec(memory_space=pl.ANY))
            args.append(buf)
            aliases = {len(args) - 1: 1}
        out_shape.append(jax.ShapeDtypeStruct((total, d), F32))
        out_specs.append(pl.BlockSpec((rows, d), lambda i, off=row0 // rows: (off + i, 0)))
        out_shape.append(jax.ShapeDtypeStruct((n, N_GROUPS), F32))
        out_specs.append(pl.BlockSpec((rows, N_GROUPS), lambda i: (cl(i), 0)))
        out_shape.append(jax.ShapeDtypeStruct((1, N_EXPERTS), F32))
        out_specs.append(pl.BlockSpec((1, N_EXPERTS), lambda i: (0, 0)))
    else:
        for _ in mods:
            out_shape.append(jax.ShapeDtypeStruct((n, d), BF16))
            out_specs.append(pl.BlockSpec((rows, d), lambda i: (i, 0)))
    outs = pl.pallas_call(
        functools.partial(_ln_mod_body, n_planes=len(planes), n_mods=len(mods), with_route=with_route,
                          steps=steps),
        grid=(steps + tail_steps,),
        in_specs=in_specs,
        out_specs=out_specs,
        out_shape=out_shape,
        input_output_aliases=aliases,
        compiler_params=_params(("arbitrary",)),
        name="ln_mod",
    )(*args)
    return [outs[0].reshape(b, t, d)] + list(outs[1:])


def _group_consts():
    lane = np.arange(LANE_GROUP)
    bd = (lane[:, None] // RW_HEAD == lane[None, :] // RW_HEAD).astype(np.float32)
    eye = (np.arange(RW_HEAD)[:, None] == lane[None, :] % RW_HEAD).astype(np.float32)
    return jnp.asarray(bd, BF16), jnp.asarray(eye, F32)


def _head_sum(x, bd):
    hi = x.astype(BF16)
    lo = (x - hi.astype(F32)).astype(BF16)
    return jnp.dot(hi, bd, preferred_element_type=F32) + jnp.dot(lo, bd, preferred_element_type=F32)


def _head_sum_rows(x, bd):
    d = x.shape[-1]
    return jnp.concatenate(
        [_head_sum(x[:, c:c + LANE_GROUP], bd) for c in range(0, d, LANE_GROUP)], axis=-1)


def _decay_and_rate(lw, la, w2_ref, w0_ref, a2_ref, a0_ref):
    second = lambda x, w_ref: jnp.dot(x.astype(BF16), w_ref[...].astype(BF16), preferred_element_type=F32)
    return (_act(second(lw, w2_ref) + w0_ref[...], "decay"),
            jax.nn.sigmoid(second(la, a2_ref) + a0_ref[...]))


def _scan_body(*refs, has_s0):
    it = iter(refs)
    r_ref, k_ref, v_ref, lw_ref, la_ref = (next(it) for _ in range(5))
    w2_ref, w0_ref, a2_ref, a0_ref = (next(it) for _ in range(4))
    kk_ref, ka_ref, bd_ref, eye_ref = next(it), next(it), next(it), next(it)
    s0_ref = next(it) if has_s0 else None
    y_ref, s_out_ref = next(it), next(it)
    st_ref, d_ref, kn_ref, al_ref, km_ref, yw_ref, vk_ref, lhs_ref = (next(it) for _ in range(8))

    nb, tt, d = r_ref.shape
    ng = st_ref.shape[1]

    def head_lanes(m):
        return pl.ds(m * RW_HEAD, RW_HEAD)

    @pl.when(pl.program_id(1) == 0)
    def _():
        if has_s0:
            for n in range(nb):
                for g in range(ng):
                    for m in range(HEADS_PER_GROUP):
                        st_ref[n, g, :, head_lanes(m)] = s0_ref[n, g * HEADS_PER_GROUP + m]
        else:
            st_ref[...] = jnp.zeros_like(st_ref)

    bd = bd_ref[...]
    eye = eye_ref[...]
    eye16 = eye.astype(BF16)
    k = k_ref[...].reshape(nb * tt, d)
    r = r_ref[...].reshape(nb * tt, d)
    dcy, a = _decay_and_rate(lw_ref[...].reshape(nb * tt, -1), la_ref[...].reshape(nb * tt, -1),
                             w2_ref, w0_ref, a2_ref, a0_ref)
    d_ref[...] = dcy.reshape(nb, tt, d)
    kk = k * kk_ref[...]
    kn = kk * lax.rsqrt(_head_sum_rows(kk * kk, bd) + 1e-12)
    al = kn * a
    km = k * (1.0 + (a - 1.0) * ka_ref[...])
    kn_ref[...] = kn.reshape(nb, tt, d)
    al_ref[...] = al.reshape(nb, tt, d)
    km_ref[...] = km.reshape(nb, tt, d)
    yw = dcy * r - kn * _head_sum_rows(al * r, bd)
    yw_ref[...] = yw.reshape(nb, tt, d)
    vk_ref[...] = v_ref[...] * _head_sum_rows(km * r, bd).reshape(nb, tt, d)
    packed_rows = 2 * SUBLANES

    def batch_step(t, n, lhs):
        def row(ref, g):
            return ref[n, pl.ds(t, 1), pl.ds(g * LANE_GROUP, LANE_GROUP)]

        def row16(ref, g):
            one = jnp.broadcast_to(row(ref, g), (packed_rows, LANE_GROUP)).astype(BF16)
            return jnp.concatenate([one] * (RW_HEAD // packed_rows), axis=0)

        for g in range(ng):
            q = g * 3 * RW_HEAD
            s16 = st_ref[n, g].astype(BF16)
            lhs[pl.ds(q, RW_HEAD), :] = s16 * row16(kn_ref, g)
            lhs[pl.ds(q + RW_HEAD, RW_HEAD), :] = s16 * row16(yw_ref, g)
            lhs[pl.ds(q + 2 * RW_HEAD, RW_HEAD), :] = eye16 * row16(v_ref, g)
        res = jnp.dot(lhs[...], bd, preferred_element_type=F32)
        for g in range(ng):
            q = g * 3 * RW_HEAD
            skk = res[q:q + RW_HEAD]
            ysum = res[q + RW_HEAD:q + 2 * RW_HEAD]
            vcol = res[q + 2 * RW_HEAD:q + 3 * RW_HEAD]
            st_ref[n, g] = (st_ref[n, g] * row(d_ref, g) - skk * row(al_ref, g)
                            + vcol * row(km_ref, g))
            y_ref[n, pl.ds(t, 1), pl.ds(g * LANE_GROUP, LANE_GROUP)] = (
                jnp.sum(ysum * eye, axis=0, keepdims=True) + row(vk_ref, g))

    n_sub = lhs_ref.shape[0]

    def steps(i, carry):
        for u in range(n_sub):
            for n in range(nb):
                batch_step(i * n_sub + u, n, lhs_ref.at[u, n])
        return carry

    lax.fori_loop(0, tt // n_sub, steps, 0)

    @pl.when(pl.program_id(1) == pl.num_programs(1) - 1)
    def _():
        for n in range(nb):
            for g in range(ng):
                for m in range(HEADS_PER_GROUP):
                    s_out_ref[n, g * HEADS_PER_GROUP + m] = st_ref[n, g, :, head_lanes(m)]


def _rwkv_scan(r, k, v, lw, la, w2, w0, a2, a0, kk, ka, s0, b, t):
    n, d = r.shape
    ng = d // LANE_GROUP
    bd, eye = _group_consts()
    if t > SCAN_TT:
        nb, tt = b, SCAN_TT
    else:
        nb, tt = min(SCAN_NB, b), t
    tok = pl.BlockSpec((nb, tt, d), lambda i, c: (i, c, 0))
    low = lambda x: pl.BlockSpec((nb, tt, x.shape[-1]), lambda i, c: (i, c, 0))
    const2 = lambda shape: pl.BlockSpec(shape, lambda i, c: (0, 0))
    n_heads = d // RW_HEAD
    state = pl.BlockSpec((nb, n_heads, RW_HEAD, RW_HEAD), lambda i, c: (i, 0, 0, 0))
    in_specs = [tok] * 3 + [low(lw), low(la), const2(w2.shape), const2((1, d)), const2(a2.shape),
                            const2((1, d)), const2((1, d)), const2((1, d)), const2(bd.shape),
                            const2(eye.shape)]
    args = [x.reshape(b, t, x.shape[-1]) for x in (r, k, v, lw, la)]
    args += [w2, w0.reshape(1, d), a2, a0.reshape(1, d), kk.reshape(1, d), ka.reshape(1, d), bd, eye]
    if s0 is not None:
        in_specs.append(state)
        args.append(s0)
    rows = nb * ng * RW_HEAD
    y, st = pl.pallas_call(
        functools.partial(_scan_body, has_s0=s0 is not None),
        grid=(b // nb, t // tt),
        in_specs=in_specs,
        out_specs=[tok, state],
        out_shape=[jax.ShapeDtypeStruct((b, t, d), F32),
                   jax.ShapeDtypeStruct((b, n_heads, RW_HEAD, RW_HEAD), F32)],
        scratch_shapes=[pltpu.VMEM((nb, ng, RW_HEAD, LANE_GROUP), F32)]
        + [pltpu.VMEM((nb, tt, d), F32)] * 6
        + [pltpu.VMEM((SCAN_UNROLL, nb, 3 * ng * RW_HEAD, LANE_GROUP), BF16)],
        compiler_params=_params(("parallel", "arbitrary")),
        name="rwkv_scan",
    )(*args)
    return y.reshape(n, d), st


def _rwkv_post_body(y_ref, r_ref, k_ref, v_ref, la_ref, lg_ref, a2_ref, a0_ref, g2_ref, lnw_ref,
                    lnb_ref, rk_ref, ka_ref, bd_ref, z_ref):
    bd = bd_ref[...]
    y = y_ref[...]
    inv = 1.0 / RW_HEAD
    mean = _head_sum_rows(y, bd) * inv
    yc = y - mean
    var = _head_sum_rows(yc * yc, bd) * inv
    yn = yc * lax.rsqrt(var + RW_LN_EPS) * lnw_ref[...] + lnb_ref[...]
    second = lambda x, w_ref: jnp.dot(x.astype(BF16), w_ref[...].astype(BF16), preferred_element_type=F32)
    a = jax.nn.sigmoid(second(la_ref[...], a2_ref) + a0_ref[...])
    gate = second(lg_ref[...], g2_ref)
    km = k_ref[...] * (1.0 + (a - 1.0) * ka_ref[...])
    bonus = _head_sum_rows(r_ref[...] * km * rk_ref[...], bd) * v_ref[...]
    z_ref[...] = ((yn + bonus) * gate).astype(BF16)


def _rwkv_post(y, r, k, v, la, lg, a2, a0, g2, lnw, lnb, rk, ka):
    n, d = y.shape
    bd, _ = _group_consts()
    rows = min(EW_ROWS, n)
    tok = pl.BlockSpec((rows, d), lambda i: (i, 0))
    low = lambda x: pl.BlockSpec((rows, x.shape[-1]), lambda i: (i, 0))
    vec = pl.BlockSpec((1, d), lambda i: (0, 0))
    whole = lambda x: pl.BlockSpec(x.shape, lambda i: (0, 0))
    return pl.pallas_call(
        _rwkv_post_body,
        grid=(n // rows,),
        in_specs=[tok] * 4 + [low(la), low(lg), whole(a2), vec, whole(g2)] + [vec] * 4 + [whole(bd)],
        out_specs=tok,
        out_shape=jax.ShapeDtypeStruct((n, d), BF16),
        compiler_params=_params(("parallel",)),
        name="rwkv_post",
    )(y, r, k, v, la, lg, a2, a0.reshape(1, d), g2, lnw.reshape(1, d), lnb.reshape(1, d),
      rk.reshape(1, d), ka.reshape(1, d), bd)


def _attn_unit(q, k_prev, v_prev, k_cur, v_cur, sink_ref, prev_limit, o_ref, row0):
    tq = q.shape[0]
    kvh = k_cur.shape[-1] // ATT_HEAD_DIM
    n_heads = kvh * GQA
    qi = lax.broadcasted_iota(jnp.int32, (tq, 1), 0)
    jp = lax.broadcasted_iota(jnp.int32, (1, WINDOW), 1)
    jc = lax.broadcasted_iota(jnp.int32, (1, tq), 1)
    diff_p = WINDOW + qi - jp
    diff_c = qi - jc
    valid_p = diff_p < prev_limit
    valid_c = diff_c >= 0
    diff_pf, diff_cf = diff_p.astype(F32), diff_c.astype(F32)
    ones_p = (lax.broadcasted_iota(jnp.int32, (WINDOW, ATT_HEAD_DIM), 1) == 0).astype(F32)
    ones_c = ones_p[:tq]
    nt = (((1,), (1,)), ((), ()))
    scale = ATT_HEAD_DIM ** -0.5
    for kh in range(kvh):
        heads = [kh * GQA + h for h in range(GQA)]
        qs = (jnp.concatenate([q[:, h * ATT_HEAD_DIM:(h + 1) * ATT_HEAD_DIM] for h in heads], axis=0)
              * scale).astype(BF16)
        sl = slice(kh * ATT_HEAD_DIM, (kh + 1) * ATT_HEAD_DIM)
        kp, kc = k_prev[:, sl].astype(BF16), k_cur[:, sl].astype(BF16)
        vp = jnp.concatenate([v_prev[:, sl], ones_p], axis=-1).astype(BF16)
        vc = jnp.concatenate([v_cur[:, sl], ones_c], axis=-1).astype(BF16)
        s_p_all = lax.dot_general(qs, kp, nt, preferred_element_type=F32)
        s_c_all = lax.dot_general(qs, kc, nt, preferred_element_type=F32)
        p_p, p_c, m_all = [], [], []
        for hq, h in enumerate(heads):
            slope = 2.0 ** (-8.0 * (h + 1) / n_heads)
            s_p = jnp.where(valid_p, s_p_all[hq * tq:(hq + 1) * tq] - slope * diff_pf, NEG_INF)
            s_c = jnp.where(valid_c, s_c_all[hq * tq:(hq + 1) * tq] - slope * diff_cf, NEG_INF)
            if tq == WINDOW:
                m = jnp.max(jnp.maximum(s_p, s_c), axis=-1, keepdims=True)
            else:
                m = jnp.maximum(jnp.max(s_p, axis=-1, keepdims=True),
                                jnp.max(s_c, axis=-1, keepdims=True))
            m = jnp.maximum(m, sink_ref[h])
            p_p.append(jnp.exp(s_p - m))
            p_c.append(jnp.exp(s_c - m))
            m_all.append(m)
        o_all = (jnp.dot(jnp.concatenate(p_p, axis=0).astype(BF16), vp, preferred_element_type=F32)
                 + jnp.dot(jnp.concatenate(p_c, axis=0).astype(BF16), vc, preferred_element_type=F32))
        for hq, h in enumerate(heads):
            o_h = o_all[hq * tq:(hq + 1) * tq]
            den = o_h[:, ATT_HEAD_DIM:ATT_HEAD_DIM + 1] + jnp.exp(sink_ref[h] - m_all[hq])
            o_ref[pl.ds(row0, tq), pl.ds(h * ATT_HEAD_DIM, ATT_HEAD_DIM)] = (
                o_h[:, :ATT_HEAD_DIM] / den).astype(o_ref.dtype)


def _attn_prefill_body(q_ref, kvc_ref, kvp_ref, sink_ref, o_ref):
    kvd = kvc_ref.shape[-1] // 2
    kvc, kvp = kvc_ref[...], kvp_ref[...]
    prev_limit = jnp.where(pl.program_id(1) > 0, WINDOW, 0)
    _attn_unit(q_ref[...], kvp[:, :kvd], kvp[:, kvd:], kvc[:, :kvd], kvc[:, kvd:], sink_ref,
               prev_limit, o_ref, 0)


def _attn_decode_body(q_ref, kvc_ref, kc_ref, vc_ref, sink_ref, o_ref, *, nb, tq):
    kvd = kvc_ref.shape[-1] // 2
    kvh = kvd // ATT_HEAD_DIM
    n_heads = kvh * GQA
    rows = n_heads * tq
    head = lax.broadcasted_iota(jnp.int32, (rows, 1), 0) // tq
    qi = lax.broadcasted_iota(jnp.int32, (rows, 1), 0) % tq
    slope = jnp.exp2(-8.0 * (head.astype(F32) + 1.0) / n_heads)
    sink = sink_ref[...]

    def key_axis(n_keys):
        c = lax.broadcasted_iota(jnp.int32, (1, kvh * n_keys), 1)
        return c // n_keys, c % n_keys

    kh_p, jp = key_axis(WINDOW)
    kh_c, jc = key_axis(tq)
    own_p = kh_p == head // GQA
    own_c = kh_c == head // GQA
    diff_p = WINDOW + qi - jp
    diff_c = qi - jc
    valid_p = jnp.where(own_p, diff_p, WINDOW) < WINDOW
    valid_c = jnp.where(own_c, diff_c, -1) >= 0
    bias_p = jnp.where(valid_p, -slope * diff_p.astype(F32), NEG_INF)
    bias_c = jnp.where(valid_c, -slope * diff_c.astype(F32), NEG_INF)
    ones = (lax.broadcasted_iota(jnp.int32, (WINDOW, ATT_HEAD_DIM), 1) == 0).astype(F32)
    nt = (((1,), (1,)), ((), ()))
    scale = ATT_HEAD_DIM ** -0.5
    head_cols = lambda x, h: x[:, h * ATT_HEAD_DIM:(h + 1) * ATT_HEAD_DIM]

    def stack_keys(x):
        return jnp.concatenate([head_cols(x, kh) for kh in range(kvh)], axis=0).astype(BF16)

    def stack_values(x):
        one = ones[:x.shape[0]]
        return jnp.concatenate(
            [jnp.concatenate([head_cols(x, kh), one], axis=-1) for kh in range(kvh)], axis=0).astype(BF16)

    for n in range(nb):
        q = q_ref[pl.ds(n * tq, tq), :]
        kvc = kvc_ref[pl.ds(n * tq, tq), :]
        qs = (jnp.concatenate([head_cols(q, h) for h in range(n_heads)], axis=0) * scale).astype(BF16)
        s_p = lax.dot_general(qs, stack_keys(kc_ref[n]), nt, preferred_element_type=F32) + bias_p
        s_c = lax.dot_general(qs, stack_keys(kvc[:, :kvd]), nt, preferred_element_type=F32) + bias_c
        m = jnp.maximum(jnp.maximum(jnp.max(s_p, axis=-1, keepdims=True),
                                    jnp.max(s_c, axis=-1, keepdims=True)), sink)
        o_all = (jnp.dot(jnp.exp(s_p - m).astype(BF16), stack_values(vc_ref[n]),
                         preferred_element_type=F32)
                 + jnp.dot(jnp.exp(s_c - m).astype(BF16), stack_values(kvc[:, kvd:]),
                           preferred_element_type=F32))
        den = o_all[:, ATT_HEAD_DIM:ATT_HEAD_DIM + 1] + jnp.exp(sink - m)
        o = o_all[:, :ATT_HEAD_DIM] / den
        for h in range(n_heads):
            o_ref[pl.ds(n * tq, tq), pl.ds(h * ATT_HEAD_DIM, ATT_HEAD_DIM)] = o[h * tq:(h + 1) * tq]


def _attn_prefill(q, kv, sinks, b, t):
    n, dq = q.shape
    kv2 = kv.shape[-1]
    nblk = t // WINDOW
    return pl.pallas_call(
        _attn_prefill_body,
        grid=(b, nblk),
        in_specs=[
            pl.BlockSpec((WINDOW, dq), lambda i, j: (i * nblk + j, 0)),
            pl.BlockSpec((WINDOW, kv2), lambda i, j: (i * nblk + j, 0)),
            pl.BlockSpec((WINDOW, kv2), lambda i, j: (i * nblk + jnp.maximum(j - 1, 0), 0)),
            pl.BlockSpec(memory_space=pltpu.SMEM),
        ],
        out_specs=pl.BlockSpec((WINDOW, dq), lambda i, j: (i * nblk + j, 0)),
        out_shape=jax.ShapeDtypeStruct((n, dq), BF16),
        compiler_params=_params(("parallel", "arbitrary")),
        name="attn_prefill",
    )(q, kv, kv, sinks.astype(F32))


def _attn_decode(q, kv, cache_k, cache_v, sinks, b, t):
    n, dq = q.shape
    kv2 = kv.shape[-1]
    kvd = kv2 // 2
    nb = min(ATT_NB, b)
    rows = nb * t
    return pl.pallas_call(
        functools.partial(_attn_decode_body, nb=nb, tq=t),
        grid=(b // nb,),
        in_specs=[
            pl.BlockSpec((rows, dq), lambda i: (i, 0)),
            pl.BlockSpec((rows, kv2), lambda i: (i, 0)),
            pl.BlockSpec((nb, WINDOW, kvd), lambda i: (i, 0, 0)),
            pl.BlockSpec((nb, WINDOW, kvd), lambda i: (i, 0, 0)),
            pl.BlockSpec((sinks.shape[0] * t, 1), lambda i: (0, 0)),
        ],
        out_specs=pl.BlockSpec((rows, dq), lambda i: (i, 0)),
        out_shape=jax.ShapeDtypeStruct((n, dq), F32),
        compiler_params=_params(("parallel",)),
        name="attn_decode",
    )(q, kv, cache_k.reshape(b, WINDOW, kvd), cache_v.reshape(b, WINDOW, kvd),
      jnp.repeat(sinks.astype(F32), t)[:, None])


def _moe_body(te_ref, tv_ref, tok_ref, dst_ref, h_hbm, gate_ref, wg_ref, wu_ref, wd_ref, out_hbm,
              xbuf, obuf, gsem, ssem, wg16, wu16, wd16):
    t = pl.program_id(0)
    last = pl.num_programs(0) - 1
    slot = t % 2

    def split(row):
        return lax.shift_right_logical(row, 3), row & (SUBLANES - 1)

    def gather_of(buf):
        def copy(i, u, tok):
            hi, lo = split(tok)
            return pltpu.make_async_copy(h_hbm.at[hi, pl.ds(lo, 1)], xbuf.at[buf, i, pl.ds(u, 1)],
                                         gsem.at[buf])
        return tok_ref, copy

    def scatter_copy(i, u, dst):
        hi, lo = split(dst)
        return pltpu.make_async_copy(obuf.at[i, pl.ds(u, 1)], out_hbm.at[hi, pl.ds(lo, 1)], ssem.at[0])

    scatter = (dst_ref, scatter_copy)

    def for_rows(tile, rows, act):
        table, build = rows
        groups = lax.shift_right_logical(tv_ref[tile] + (SUBLANES - 1), 3)

        def body(i, carry):
            base = tile * MOE_TG + i * SUBLANES
            entries = [table[base + u] for u in range(SUBLANES)]
            for u in range(SUBLANES):
                act(build(i, u, entries[u]))
            return carry
        lax.fori_loop(0, groups, body, 0)

    start = lambda copy: copy.start()
    wait = lambda copy: copy.wait()

    groups_per_tile = MOE_TG // SUBLANES

    @pl.when(t == 0)
    def _():
        xbuf[...] = jnp.zeros_like(xbuf)
        obuf[...] = jnp.zeros_like(obuf)
        spare = pltpu.make_async_copy(
            obuf, out_hbm.at[pl.ds(out_hbm.shape[0] - groups_per_tile, groups_per_tile)], ssem.at[0])
        spare.start()
        spare.wait()
        for_rows(0, gather_of(0), start)

    nxt = jnp.minimum(t + 1, last)

    @pl.when(t < last)
    def _():
        for_rows(nxt, gather_of(1 - slot), start)

    prev = jnp.maximum(t - 1, 0)

    @pl.when((t == 0) | (te_ref[t] != te_ref[prev]))
    def _():
        wg16[...] = wg_ref[...].astype(BF16)
        wu16[...] = wu_ref[...].astype(BF16)
        wd16[...] = wd_ref[...].astype(BF16)

    @pl.when(tv_ref[t] > 0)
    def _():
        for_rows(t, gather_of(slot), wait)
        x = xbuf[slot].reshape(MOE_TG, -1).astype(BF16)
        hg = jnp.dot(x, wg16[...], preferred_element_type=F32)
        hu = jnp.dot(x, wu16[...], preferred_element_type=F32)
        act = (hg * jax.nn.sigmoid(hg)) * hu * gate_ref[...]
        o = jnp.dot(act.astype(BF16), wd16[...], preferred_element_type=F32)

        @pl.when(t > 0)
        def _():
            for_rows(prev, scatter, wait)

        obuf[...] = o.reshape(obuf.shape)
        for_rows(t, scatter, start)

        @pl.when(t == last)
        def _():
            for_rows(t, scatter, wait)

    @pl.when((tv_ref[t] == 0) & (t > 0))
    def _():
        for_rows(prev, scatter, wait)


def _moe_plan(routes, trunk_counts, n_tiles):
    e_parts, rank_parts, before = [], [], jnp.zeros((N_EXPERTS,), jnp.int32)
    for route, cnt in zip(routes, trunk_counts):
        e_tr = route[:, :2].astype(jnp.int32).reshape(-1)
        e_parts.append(e_tr)
        rank_parts.append(route[:, 4:6].astype(jnp.int32).reshape(-1) + before[e_tr])
        before = before + cnt.reshape(-1).astype(jnp.int32)
    e, rank, counts = jnp.concatenate(e_parts), jnp.concatenate(rank_parts), before
    w = jnp.concatenate([route[:, 2:4].reshape(-1) for route in routes])
    n = e.shape[0] // 2
    padded = ((counts + MOE_TG - 1) // MOE_TG) * MOE_TG
    pend = jnp.cumsum(padded)
    dest = (pend - padded)[e] + rank
    p_rows = n_tiles * MOE_TG
    vals = jnp.stack([jnp.arange(1, 2 * n + 1, dtype=jnp.int32), lax.bitcast_convert_type(w, jnp.int32)],
                     axis=1)
    placed = jnp.zeros((p_rows, 2), jnp.int32).at[dest].set(vals)
    has_pair = placed[:, 0] > 0
    pair = placed[:, 0] - 1
    row_token = jnp.where(has_pair, pair // 2, 0)
    spare = 2 * n + jnp.arange(p_rows, dtype=jnp.int32) % MOE_TG
    row_dst = jnp.where(has_pair, (pair % 2) * n + pair // 2, spare)
    row_gate = jnp.where(has_pair, lax.bitcast_convert_type(placed[:, 1], F32), 0.0)
    starts = jnp.arange(n_tiles, dtype=jnp.int32) * MOE_TG
    tile_expert = jnp.minimum(jnp.searchsorted(pend, starts, side="right"), N_EXPERTS - 1)
    tile_rows = jnp.clip(counts[tile_expert] - (starts - (pend - padded)[tile_expert]), 0, MOE_TG)
    tile_rows = jnp.where(starts < pend[-1], tile_rows, 0).astype(jnp.int32)
    last_used = jnp.max(jnp.where(tile_rows > 0, tile_expert, 0))
    tile_expert = jnp.where(tile_rows > 0, tile_expert, last_used).astype(jnp.int32)
    return row_token, row_dst, row_gate, tile_expert, tile_rows


def _moe(h, routes, trunk_counts, w_gate, w_up, w_down, layer):
    n, d = h.shape
    group = (SUBLANES, d)
    f = w_gate.shape[-1]
    n_tiles = (2 * n) // MOE_TG + N_EXPERTS
    n_out = 2 * n + MOE_TG
    assert n % SUBLANES == 0
    row_token, row_dst, row_gate, tile_expert, tile_rows = _moe_plan(routes, trunk_counts, n_tiles)
    out = pl.pallas_call(
        _moe_body,
        grid_spec=pltpu.PrefetchScalarGridSpec(
            num_scalar_prefetch=4,
            grid=(n_tiles,),
            in_specs=[
                pl.BlockSpec(memory_space=pl.ANY),
                pl.BlockSpec((MOE_TG, 1), lambda t, te, tv, rt, rd: (t, 0)),
                pl.BlockSpec((None, None, d, f), lambda t, te, tv, rt, rd: (layer, te[t], 0, 0)),
                pl.BlockSpec((None, None, d, f), lambda t, te, tv, rt, rd: (layer, te[t], 0, 0)),
                pl.BlockSpec((None, None, f, d), lambda t, te, tv, rt, rd: (layer, te[t], 0, 0)),
            ],
            out_specs=pl.BlockSpec(memory_space=pl.ANY),
            scratch_shapes=[pltpu.VMEM((2, MOE_TG // SUBLANES) + group, F32),
                            pltpu.VMEM((MOE_TG // SUBLANES,) + group, F32),
                            pltpu.SemaphoreType.DMA((2,)), pltpu.SemaphoreType.DMA((1,)),
                            pltpu.VMEM((d, f), BF16), pltpu.VMEM((d, f), BF16),
                            pltpu.VMEM((f, d), BF16)],
        ),
        out_shape=jax.ShapeDtypeStruct((n_out // SUBLANES,) + group, F32),
        compiler_params=_params(("arbitrary",)),
        name="moe_ffn",
    )(tile_expert, tile_rows, row_token, row_dst, h.reshape((n // SUBLANES,) + group),
      row_gate.reshape(-1, 1), w_gate, w_up, w_down)
    return out.reshape(n_out, d)


def _split_mod(mod, n_parts, trunk_rows):
    mod = mod[:, None, :]
    return [[(mod, k, row0) for k in range(n_parts)] for row0 in trunk_rows]


def kernel(x_prompt, x_sample, c_prompt, c_sample, state_wkv, state_shift, cache_k_win, cache_v_win, ada_w, ada_b, ln_g, ln_b, rw_mu, rw_w_rkv, rw_w0, rw_w1, rw_w2, rw_a0, rw_a1, rw_a2, rw_g1, rw_g2, rw_k_k, rw_k_a, rw_r_k, rw_lnx_w, rw_lnx_b, rw_wo, ada_kv_w, ada_kv_b, w_kv, w_q, attn_sinks, w_o_attn, router_w, router_bias, moe_w_gate, moe_w_up, moe_w_down):
    d = x_prompt.shape[-1]
    xs = [x_prompt, x_sample]
    shapes = [x.shape[:2] for x in xs]
    counts = [b * t for b, t in shapes]
    bp = shapes[0][0]

    c_all = jnp.concatenate([c_sample, c_prompt], axis=0)
    trunk_rows = [shapes[1][0], 0]
    mods = {(l, s): _split_mod(_mm(c_all, ada_w, name="mm_ada", w_lead=(l, s), bias=ada_b[l, s],
                                   in_act="silu"), 3, trunk_rows)
            for l in range(DEPTH) for s in range(2)}
    mods_kv = _split_mod(_mm(c_all, ada_kv_w, name="mm_ada_kv", bias=ada_kv_b, in_act="silu"), 2,
                         trunk_rows)

    by_slot = lambda x: x.reshape(-1, N_GROUPS, EXPERTS_PER_GROUP).swapaxes(1, 2).reshape(-1, N_EXPERTS)
    router = (by_slot(router_w), by_slot(router_bias))

    n_all = counts[0] + counts[1]
    row0 = [0, counts[0]]

    def moe_layer(h_all, routes, layer):
        out = _moe(h_all, [r for r, _ in routes], [c for _, c in routes], moe_w_gate, moe_w_up,
                   moe_w_down, layer)
        return [[(out, slot * n_all + row0[tr]) for slot in range(2)] for tr in range(2)]

    firsts = [jnp.zeros((bp, 1, d), F32), state_shift[0][:, None, :]]
    s0s = [None, state_wkv[0]]
    x1, h1, route1, states, hlasts = [], None, [], [], []
    for tr in range(2):
        b, t = shapes[tr]
        shift, scale, gate = mods[0, 0][tr]
        xmix, hlast, lw, la, lg = _premix(xs[tr], shift, scale, firsts[tr], rw_mu[0], rw_w1[0],
                                          rw_a1[0], rw_g1[0])
        r = _mm(xmix, rw_w_rkv, name="mm_r", x_lead=(0,), w_lead=(0, 0), tm=MM_TM_BF16)
        k = _mm(xmix, rw_w_rkv, name="mm_k", x_lead=(1,), w_lead=(0, 1), tm=MM_TM_BF16)
        v = _mm(xmix, rw_w_rkv, name="mm_v", x_lead=(2,), w_lead=(0, 2), tm=MM_TM_BF16)
        y, st = _rwkv_scan(r, k, v, lw, la, rw_w2[0], rw_w0[0], rw_a2[0], rw_a0[0], rw_k_k[0],
                           rw_k_a[0], s0s[tr], b, t)
        z = _rwkv_post(y, r, k, v, la, lg, rw_a2[0], rw_a0[0], rw_g2[0], rw_lnx_w[0], rw_lnx_b[0],
                       rw_r_k[0], rw_k_a[0])
        out = _mm(z, rw_wo, name="mm_wo", w_lead=(0,), tm=MM_TM_BF16)
        sh, sc, _ = mods[0, 1][tr]
        xn, h1, route, cnt = _ln_mod(xs[tr], [(out, 0)], gate, ln_g[0, 0], ln_b[0, 0], [(sh, sc)],
                                     router, (h1, n_all, row0[tr]))
        x1.append(xn), route1.append((route, cnt)), states.append(st), hlasts.append(hlast)
    moe1 = moe_layer(h1, route1, 0)

    caches = [None, (cache_k_win, cache_v_win)]
    x3, h3, route3, kvs = [], None, [], []
    for tr in range(2):
        b, t = shapes[tr]
        shq, scq, gate_q = mods[1, 0][tr]
        shk, sck = mods_kv[tr]
        x2, hq, hkv = _ln_mod(x1[tr], moe1[tr], mods[0, 1][tr][2], ln_g[0, 1], ln_b[0, 1],
                              [(shq, scq), (shk, sck)])
        q = _mm(hq, w_q, name="mm_q", w_lead=(0,), tm=MM_TM_BF16)
        kv = _mm(hkv, w_kv, name="mm_kv", tm=MM_TM_BF16)
        if caches[tr] is None:
            o = _attn_prefill(q, kv, attn_sinks[0], b, t)
        else:
            o = _attn_decode(q, kv, caches[tr][0], caches[tr][1], attn_sinks[0], b, t)
        out = _mm(o, w_o_attn, name="mm_o", w_lead=(0,), tm=MM_TM_BF16 if o.dtype == BF16 else MM_TM)
        sh, sc, _ = mods[1, 1][tr]
        xn, h3, route, cnt = _ln_mod(x2, [(out, 0)], gate_q, ln_g[1, 0], ln_b[1, 0], [(sh, sc)],
                                     router, (h3, n_all, row0[tr]))
        x3.append(xn), route3.append((route, cnt)), kvs.append(kv)
    moe3 = moe_layer(h3, route3, 1)
    ys = [_ln_mod(x3[tr], moe3[tr], mods[1, 1][tr][2], ln_g[1, 1], ln_b[1, 1], [])[0]
          for tr in range(2)]

    kvd = kvs[0].shape[-1] // 2
    kvh = kvd // ATT_HEAD_DIM
    (bp, tp), (bs, ts) = shapes
    kv_p = kvs[0].reshape(bp, tp, 2 * kvd)[:, tp - WINDOW:]
    p_k_win = kv_p[..., :kvd].reshape(bp, WINDOW, kvh, ATT_HEAD_DIM)
    p_v_win = kv_p[..., kvd:].reshape(bp, WINDOW, kvh, ATT_HEAD_DIM)
    kv_s = kvs[1].reshape(bs, ts, 2 * kvd)
    s_k_win = jnp.concatenate(
        [cache_k_win, kv_s[..., :kvd].reshape(bs, ts, kvh, ATT_HEAD_DIM)], axis=1)[:, -WINDOW:]
    s_v_win = jnp.concatenate(
        [cache_v_win, kv_s[..., kvd:].reshape(bs, ts, kvh, ATT_HEAD_DIM)], axis=1)[:, -WINDOW:]
    p_wkv = states[0][None]
    s_wkv = states[1][None]
    p_shift = hlasts[0].reshape(1, bp, d)
    s_shift = hlasts[1].reshape(1, bs, d)
    return (ys[0], ys[1], p_wkv, p_shift, p_k_win, p_v_win, s_wkv, s_shift, s_k_win, s_v_win)
```

```python
import functools

import jax
import jax.numpy as jnp
import numpy as np
from jax import lax
from jax.experimental import pallas as pl
from jax.experimental.pallas import tpu as pltpu

F32 = jnp.float32
BF16 = jnp.bfloat16

RW_HEAD = 64
ATT_HEAD_DIM = 64
GQA = 8
WINDOW = 128
N_EXPERTS = 32
EXPERTS_PER_GROUP = 4
N_GROUPS = N_EXPERTS // EXPERTS_PER_GROUP
DEPTH = 2
ALPHA = (2.0 * DEPTH) ** 0.25
LN_EPS = 1e-5
RW_LN_EPS = 64e-5
NEG_INF = -1e30
DECAY_SCALE = float(np.exp(-0.5))
MIX_RKV = (0, 2, 3)
MIX_W, MIX_A, MIX_G = 1, 4, 5

SUBLANES = 8
LANE_GROUP = 256
HEADS_PER_GROUP = LANE_GROUP // RW_HEAD
VMEM_LIMIT = 56 * 1024 * 1024

MM_TM = 1024
MM_TM_BF16 = 2048
MM_TN = 512
EW_ROWS = 256
SCAN_TT = 64
SCAN_NB = 4
SCAN_UNROLL = 8
ATT_NB = 8
MOE_TG = 256


def _params(sem):
    return pltpu.CompilerParams(dimension_semantics=sem, vmem_limit_bytes=VMEM_LIMIT)


def _act(x, kind):
    if kind is None:
        return x
    if kind == "silu":
        return x * jax.nn.sigmoid(x)
    if kind == "tanh":
        return jnp.tanh(x)
    if kind == "sigmoid":
        return jax.nn.sigmoid(x)
    if kind == "decay":
        return jnp.exp(-DECAY_SCALE * jax.nn.sigmoid(x))
    raise ValueError(kind)


def _mm_body(*refs, in_act, out_act, has_bias, stage_x):
    x_ref, w_ref = refs[0], refs[1]
    b_ref = refs[2] if has_bias else None
    o_ref = refs[2 + has_bias]
    if stage_x:
        xs_ref = refs[3 + has_bias]

        @pl.when(pl.program_id(1) == 0)
        def _():
            xs_ref[...] = _act(x_ref[...].astype(F32), in_act).astype(BF16)
    else:
        xs_ref = x_ref

    acc = jnp.dot(xs_ref[...], w_ref[...].astype(BF16), preferred_element_type=F32)
    if has_bias:
        acc = acc + b_ref[...]
    o_ref[...] = _act(acc, out_act).astype(o_ref.dtype)


def _mm(x, w, *, name, x_lead=(), w_lead=(), bias=None, in_act=None, out_act=None, out_dtype=F32,
        tm=MM_TM, tn=MM_TN):
    m, k = x.shape[-2:]
    n = w.shape[-1]
    tm = min(tm, m)
    tn = min(tn, n)
    assert m % tm == 0 and n % tn == 0, (m, tm, n, tn)
    nx, nw = len(x_lead), len(w_lead)
    in_specs = [
        pl.BlockSpec((None,) * nx + (tm, k), lambda i, j: tuple(x_lead) + (i, 0)),
        pl.BlockSpec((None,) * nw + (k, tn), lambda i, j: tuple(w_lead) + (0, j)),
    ]
    args = [x, w]
    if bias is not None:
        in_specs.append(pl.BlockSpec((1, tn), lambda i, j: (0, j)))
        args.append(bias.reshape(1, n).astype(F32))
    stage_x = x.dtype != BF16 or in_act is not None
    body = functools.partial(_mm_body, in_act=in_act, out_act=out_act, has_bias=bias is not None,
                             stage_x=stage_x)
    return pl.pallas_call(
        body,
        grid=(m // tm, n // tn),
        in_specs=in_specs,
        out_specs=pl.BlockSpec((tm, tn), lambda i, j: (i, j)),
        out_shape=jax.ShapeDtypeStruct((m, n), out_dtype),
        scratch_shapes=[pltpu.VMEM((tm, k), BF16)] if stage_x else [],
        compiler_params=_params(("parallel", "arbitrary")),
        name=name,
    )(*args)


def _geo(b, t):
    if t == SUBLANES:
        g = min(EW_ROWS // SUBLANES, b)
        return dict(groups=g, steps=b // g, mod_groups=g, steps_per_batch=1, mod_index=lambda i: i)
    assert t % EW_ROWS == 0
    spb = t // EW_ROWS
    return dict(groups=EW_ROWS // SUBLANES, steps=b * spb, mod_groups=1, steps_per_batch=spb,
                mod_index=lambda i: i // spb)


def _mod_spec(mod, mg, d, block_of_step):
    _, part, row0 = mod
    assert row0 % mg == 0, (row0, mg)
    return pl.BlockSpec((mg, 1, d), lambda i: (row0 // mg + block_of_step(i), 0, part))


def _premix_body(x_ref, xprev_ref, first_ref, shift_ref, scale_ref, mu_ref, w1_ref, a1_ref, g1_ref,
                 xmix_ref, hlast_ref, lw_ref, la_ref, lg_ref, *, steps_per_batch):
    i = pl.program_id(0)
    g, s, d = x_ref.shape
    gf = first_ref.shape[0]
    scale = scale_ref[...]
    shift = shift_ref[...]
    h3 = x_ref[...] * (1.0 + scale) + shift
    hlast_ref[...] = h3[g - gf:, s - 1:, :]
    h = h3.reshape(g * s, d)
    if steps_per_batch > 1:
        hprev_row = xprev_ref[:, s - 1:, :] * (1.0 + scale) + shift
        first = jnp.where(i % steps_per_batch == 0, first_ref[...], hprev_row)
        period = g * s
    else:
        first = first_ref[...]
        period = s
    first2 = jnp.broadcast_to(first, (g, s, d)).reshape(g * s, d)
    row = lax.broadcasted_iota(jnp.int32, (g * s, 1), 0)
    hp = jnp.where(row % period == 0, first2, pltpu.roll(h, 1, 0))
    xx = hp - h
    mix = lambda m: (h + xx * mu_ref[m]).astype(BF16)
    for slot, m in enumerate(MIX_RKV):
        xmix_ref[slot] = mix(m)
    lora = lambda m, w_ref: jnp.dot(mix(m), w_ref[...].astype(BF16), preferred_element_type=F32)
    lw_ref[...] = jnp.tanh(lora(MIX_W, w1_ref))
    la_ref[...] = lora(MIX_A, a1_ref)
    lg_ref[...] = jax.nn.sigmoid(lora(MIX_G, g1_ref))


def _premix(x, shift, scale, first, mu, w1, a1, g1):
    b, t, d = x.shape
    n = b * t
    n_mix = mu.shape[0]
    c = _geo(b, t)
    g, mg, midx = c["groups"], c["mod_groups"], c["mod_index"]
    rows = g * SUBLANES
    xg = x.reshape(-1, SUBLANES, d)
    per_batch = pl.BlockSpec((mg, 1, d), lambda i: (midx(i), 0, 0))
    mod_spec = lambda mod: _mod_spec(mod, mg, d, midx)
    whole = lambda a: pl.BlockSpec(a.shape, lambda i: (0,) * a.ndim)
    lora_out = lambda w: (pl.BlockSpec((rows, w.shape[-1]), lambda i: (i, 0)),
                          jax.ShapeDtypeStruct((n, w.shape[-1]), F32))
    lora_specs, lora_shapes = zip(lora_out(w1), lora_out(a1), lora_out(g1))
    return pl.pallas_call(
        functools.partial(_premix_body, steps_per_batch=c["steps_per_batch"]),
        grid=(c["steps"],),
        in_specs=[
            pl.BlockSpec((g, SUBLANES, d), lambda i: (i, 0, 0)),
            pl.BlockSpec((1, SUBLANES, d), lambda i: (jnp.maximum(i * g - 1, 0), 0, 0)),
            per_batch, mod_spec(shift), mod_spec(scale),
            pl.BlockSpec((n_mix, 1, d), lambda i: (0, 0, 0)),
            whole(w1), whole(a1), whole(g1),
        ],
        out_specs=[pl.BlockSpec((len(MIX_RKV), rows, d), lambda i: (0, i, 0)), per_batch,
                   *lora_specs],
        out_shape=[jax.ShapeDtypeStruct((len(MIX_RKV), n, d), BF16),
                   jax.ShapeDtypeStruct((b, 1, d), F32), *lora_shapes],
        compiler_params=_params(("arbitrary",)),
        name="premix",
    )(xg, xg, first, shift[0], scale[0], mu.reshape(n_mix, 1, d), w1, a1, g1)


def _split_bf16(x):
    hi = x.astype(BF16)
    return hi, (x - hi.astype(F32)).astype(BF16)


def _route(h, rw_ref, rb_ref):
    t = h.shape[0]
    h_hi, h_lo = _split_bf16(h)
    w_hi, w_lo = _split_bf16(rw_ref[...])
    logits = (jnp.dot(h_hi, w_hi, preferred_element_type=F32)
              + jnp.dot(h_lo, w_hi, preferred_element_type=F32)
              + jnp.dot(h_hi, w_lo, preferred_element_type=F32))
    s_all = jax.nn.sigmoid(logits)
    sb_all = s_all + rb_ref[...]
    s = [s_all[:, e * N_GROUPS:(e + 1) * N_GROUPS] for e in range(EXPERTS_PER_GROUP)]
    sb = [sb_all[:, e * N_GROUPS:(e + 1) * N_GROUPS] for e in range(EXPERTS_PER_GROUP)]
    hi01, lo01 = jnp.maximum(sb[0], sb[1]), jnp.minimum(sb[0], sb[1])
    hi23, lo23 = jnp.maximum(sb[2], sb[3]), jnp.minimum(sb[2], sb[3])
    top1 = jnp.maximum(hi01, hi23)
    top2 = jnp.maximum(jnp.minimum(hi01, hi23), jnp.maximum(lo01, lo23))
    gscore = top1 + top2
    lane = lax.broadcasted_iota(jnp.int32, (t, N_GROUPS), 1)
    gmax = jnp.max(gscore, axis=-1, keepdims=True)
    gsel = jnp.min(jnp.where(gscore == gmax, lane, N_GROUPS), axis=-1, keepdims=True)
    in_group = lane == gsel
    masked = [jnp.where(in_group, v, NEG_INF) for v in sb]
    idx = [lane * EXPERTS_PER_GROUP + e for e in range(EXPERTS_PER_GROUP)]

    def pick(vals):
        best = functools.reduce(jnp.maximum, [jnp.max(v, axis=-1, keepdims=True) for v in vals])
        return functools.reduce(jnp.minimum, [
            jnp.min(jnp.where(v == best, ix, N_EXPERTS), axis=-1, keepdims=True)
            for v, ix in zip(vals, idx)])

    def weight(sel):
        return functools.reduce(jnp.add, [
            jnp.sum(jnp.where(ix == sel, v, 0.0), axis=-1, keepdims=True) for v, ix in zip(s, idx)])

    e1 = pick(masked)
    e2 = pick([jnp.where(ix == e1, -jnp.inf, v) for v, ix in zip(masked, idx)])
    w1, w2 = weight(e1), weight(e2)
    wsum = w1 + w2
    return jnp.where(lane == 0, e1.astype(F32),
                     jnp.where(lane == 1, e2.astype(F32),
                               jnp.where(lane == 2, w1 / wsum, jnp.where(lane == 3, w2 / wsum, 0.0))))


def _ln_mod_body(*refs, n_planes, n_mods, with_route, steps):
    if with_route:
        @pl.when(pl.program_id(0) < steps)
        def _():
            _ln_mod_compute(*refs, n_planes=n_planes, n_mods=n_mods, with_route=True)

        @pl.when(pl.program_id(0) >= steps)
        def _():
            h_ref = refs[-2]
            h_ref[...] = jnp.zeros_like(h_ref)
    else:
        _ln_mod_compute(*refs, n_planes=n_planes, n_mods=n_mods, with_route=False)


def _ln_mod_compute(*refs, n_planes, n_mods, with_route):
    it = iter(refs)
    x_ref = next(it)
    o_refs = [next(it) for _ in range(n_planes)]
    gate_ref, lng_ref, lnb_ref = next(it), next(it), next(it)
    mod_refs = [(next(it), next(it)) for _ in range(n_mods)]
    rw_ref = next(it) if with_route else None
    rb_ref = next(it) if with_route else None
    if with_route:
        next(it)
    xnew_ref = next(it)
    h_refs = [next(it) for _ in range(n_mods)]
    route_ref = next(it) if with_route else None

    g, s, d = x_ref.shape

    out = o_refs[0][...]
    for o_ref in o_refs[1:]:
        out = out + o_ref[...]
    y = ALPHA * x_ref[...] + gate_ref[...] * out
    mu = jnp.mean(y, axis=-1, keepdims=True)
    yc = y - mu
    var = jnp.mean(yc * yc, axis=-1, keepdims=True)
    xn = yc * lax.rsqrt(var + LN_EPS) * lng_ref[...] + lnb_ref[...]
    xnew_ref[...] = xn
    for k, ((shift_ref, scale_ref), h_ref) in enumerate(zip(mod_refs, h_refs)):
        h = (xn * (1.0 + scale_ref[...]) + shift_ref[...]).reshape(g * s, d)
        h_ref[...] = h.astype(h_ref.dtype)
        if with_route:
            route_ref[...] = _route(h, rw_ref, rb_ref)


def _ln_mod(x, planes, gate, ln_g, ln_b, mods, router=None, moe_rows=None):
    b, t, d = x.shape
    n = b * t
    c = _geo(b, t)
    g, mg, midx = c["groups"], c["mod_groups"], c["mod_index"]
    rows = g * SUBLANES
    with_route = router is not None
    steps = c["steps"]
    tail_steps = 0
    if with_route and moe_rows[0] is None:
        assert moe_rows[2] == 0 and (moe_rows[1] - n) % rows == 0
        tail_steps = (moe_rows[1] - n) // rows
    cl = lambda i: jnp.minimum(i, steps - 1)
    tok = pl.BlockSpec((g, SUBLANES, d), lambda i: (cl(i), 0, 0))
    mod_spec = lambda mod: _mod_spec(mod, mg, d, lambda i: midx(cl(i)))
    const = pl.BlockSpec((1, 1, d), lambda i: (0, 0, 0))
    in_specs = [tok]
    args = [x.reshape(-1, SUBLANES, d)]
    for arr, row0 in planes:
        assert row0 % rows == 0, (row0, rows)
        in_specs.append(pl.BlockSpec((g, SUBLANES, d), lambda i, off=row0 // rows: (off + cl(i), 0, 0)))
        args.append(arr.reshape(-1, SUBLANES, d))
    in_specs += [mod_spec(gate), const, const]
    args += [gate[0], ln_g.reshape(1, 1, d), ln_b.reshape(1, 1, d)]
    for shift, scale in mods:
        in_specs += [mod_spec(shift), mod_spec(scale)]
        args += [shift[0], scale[0]]
    out_shape = [jax.ShapeDtypeStruct((n // SUBLANES, SUBLANES, d), F32)]
    out_specs = [tok]
    aliases = {}
    if with_route:
        in_specs += [pl.BlockSpec(a.shape, lambda i: (0, 0)) for a in router]
        args += list(router)
        buf, total, row0 = moe_rows
        if buf is None:
            in_specs.append(pl.BlockSpec((1, 128), lambda i: (0, 0)))
            args.append(jnp.zeros((1, 128), F32))
        else:
            in_specs.append(pl.BlockSpec(memory_space=pl.ANY))
            args.append(buf)
            aliases = {len(args) - 1: 1}
        out_shape.append(jax.ShapeDtypeStruct((total, d), F32))
        out_specs.append(pl.BlockSpec((rows, d), lambda i, off=row0 // rows: (off + i, 0)))
        out_shape.append(jax.ShapeDtypeStruct((n, N_GROUPS), F32))
        out_specs.append(pl.BlockSpec((rows, N_GROUPS), lambda i: (cl(i), 0)))
    else:
        for _ in mods:
            out_shape.append(jax.ShapeDtypeStruct((n, d), BF16))
            out_specs.append(pl.BlockSpec((rows, d), lambda i: (i, 0)))
    outs = pl.pallas_call(
        functools.partial(_ln_mod_body, n_planes=len(planes), n_mods=len(mods), with_route=with_route,
                          steps=steps),
        grid=(steps + tail_steps,),
        in_specs=in_specs,
        out_specs=out_specs,
        out_shape=out_shape,
        input_output_aliases=aliases,
        compiler_params=_params(("arbitrary",)),
        name="ln_mod",
    )(*args)
    return [outs[0].reshape(b, t, d)] + list(outs[1:])


def _group_consts():
    lane = np.arange(LANE_GROUP)
    bd = (lane[:, None] // RW_HEAD == lane[None, :] // RW_HEAD).astype(np.float32)
    eye = (np.arange(RW_HEAD)[:, None] == lane[None, :] % RW_HEAD).astype(np.float32)
    return jnp.asarray(bd, BF16), jnp.asarray(eye, F32)


def _head_sum(x, bd):
    hi = x.astype(BF16)
    lo = (x - hi.astype(F32)).astype(BF16)
    return jnp.dot(hi, bd, preferred_element_type=F32) + jnp.dot(lo, bd, preferred_element_type=F32)


def _head_sum_rows(x, bd):
    d = x.shape[-1]
    return jnp.concatenate(
        [_head_sum(x[:, c:c + LANE_GROUP], bd) for c in range(0, d, LANE_GROUP)], axis=-1)


def _decay_and_rate(lw, la, w2_ref, w0_ref, a2_ref, a0_ref):
    second = lambda x, w_ref: jnp.dot(x.astype(BF16), w_ref[...].astype(BF16), preferred_element_type=F32)
    return (_act(second(lw, w2_ref) + w0_ref[...], "decay"),
            jax.nn.sigmoid(second(la, a2_ref) + a0_ref[...]))


def _scan_body(*refs, has_s0):
    it = iter(refs)
    r_ref, k_ref, v_ref, lw_ref, la_ref = (next(it) for _ in range(5))
    w2_ref, w0_ref, a2_ref, a0_ref = (next(it) for _ in range(4))
    kk_ref, ka_ref, bd_ref, eye_ref = next(it), next(it), next(it), next(it)
    s0_ref = next(it) if has_s0 else None
    y_ref, s_out_ref = next(it), next(it)
    st_ref, d_ref, kn_ref, al_ref, km_ref, yw_ref, vk_ref, lhs_ref = (next(it) for _ in range(8))

    nb, tt, d = r_ref.shape
    ng = st_ref.shape[1]

    def head_lanes(m):
        return pl.ds(m * RW_HEAD, RW_HEAD)

    @pl.when(pl.program_id(1) == 0)
    def _():
        if has_s0:
            for n in range(nb):
                for g in range(ng):
                    for m in range(HEADS_PER_GROUP):
                        st_ref[n, g, :, head_lanes(m)] = s0_ref[n, g * HEADS_PER_GROUP + m]
        else:
            st_ref[...] = jnp.zeros_like(st_ref)

    bd = bd_ref[...]
    eye = eye_ref[...]
    eye16 = eye.astype(BF16)
    k = k_ref[...].reshape(nb * tt, d)
    r = r_ref[...].reshape(nb * tt, d)
    dcy, a = _decay_and_rate(lw_ref[...].reshape(nb * tt, -1), la_ref[...].reshape(nb * tt, -1),
                             w2_ref, w0_ref, a2_ref, a0_ref)
    d_ref[...] = dcy.reshape(nb, tt, d)
    kk = k * kk_ref[...]
    kn = kk * lax.rsqrt(_head_sum_rows(kk * kk, bd) + 1e-12)
    al = kn * a
    km = k * (1.0 + (a - 1.0) * ka_ref[...])
    kn_ref[...] = kn.reshape(nb, tt, d)
    al_ref[...] = al.reshape(nb, tt, d)
    km_ref[...] = km.reshape(nb, tt, d)
    yw = dcy * r - kn * _head_sum_rows(al * r, bd)
    yw_ref[...] = yw.reshape(nb, tt, d)
    vk_ref[...] = v_ref[...] * _head_sum_rows(km * r, bd).reshape(nb, tt, d)
    packed_rows = 2 * SUBLANES

    def batch_step(t, n, lhs):
        def row(ref, g):
            return ref[n, pl.ds(t, 1), pl.ds(g * LANE_GROUP, LANE_GROUP)]

        def row16(ref, g):
            one = jnp.broadcast_to(row(ref, g), (packed_rows, LANE_GROUP)).astype(BF16)
            return jnp.concatenate([one] * (RW_HEAD // packed_rows), axis=0)

        for g in range(ng):
            q = g * 3 * RW_HEAD
            s16 = st_ref[n, g].astype(BF16)
            lhs[pl.ds(q, RW_HEAD), :] = s16 * row16(kn_ref, g)
            lhs[pl.ds(q + RW_HEAD, RW_HEAD), :] = s16 * row16(yw_ref, g)
            lhs[pl.ds(q + 2 * RW_HEAD, RW_HEAD), :] = eye16 * row16(v_ref, g)
        res = jnp.dot(lhs[...], bd, preferred_element_type=F32)
        for g in range(ng):
            q = g * 3 * RW_HEAD
            skk = res[q:q + RW_HEAD]
            ysum = res[q + RW_HEAD:q + 2 * RW_HEAD]
            vcol = res[q + 2 * RW_HEAD:q + 3 * RW_HEAD]
            st_ref[n, g] = (st_ref[n, g] * row(d_ref, g) - skk * row(al_ref, g)
                            + vcol * row(km_ref, g))
            y_ref[n, pl.ds(t, 1), pl.ds(g * LANE_GROUP, LANE_GROUP)] = (
                jnp.sum(ysum * eye, axis=0, keepdims=True) + row(vk_ref, g))

    n_sub = lhs_ref.shape[0]

    def steps(i, carry):
        for u in range(n_sub):
            for n in range(nb):
                batch_step(i * n_sub + u, n, lhs_ref.at[u, n])
        return carry

    lax.fori_loop(0, tt // n_sub, steps, 0)

    @pl.when(pl.program_id(1) == pl.num_programs(1) - 1)
    def _():
        for n in range(nb):
            for g in range(ng):
                for m in range(HEADS_PER_GROUP):
                    s_out_ref[n, g * HEADS_PER_GROUP + m] = st_ref[n, g, :, head_lanes(m)]


def _rwkv_scan(r, k, v, lw, la, w2, w0, a2, a0, kk, ka, s0, b, t):
    n, d = r.shape
    ng = d // LANE_GROUP
    bd, eye = _group_consts()
    if t > SCAN_TT:
        nb, tt = b, SCAN_TT
    else:
        nb, tt = min(SCAN_NB, b), t
    tok = pl.BlockSpec((nb, tt, d), lambda i, c: (i, c, 0))
    low = lambda x: pl.BlockSpec((nb, tt, x.shape[-1]), lambda i, c: (i, c, 0))
    const2 = lambda shape: pl.BlockSpec(shape, lambda i, c: (0, 0))
    n_heads = d // RW_HEAD
    state = pl.BlockSpec((nb, n_heads, RW_HEAD, RW_HEAD), lambda i, c: (i, 0, 0, 0))
    in_specs = [tok] * 3 + [low(lw), low(la), const2(w2.shape), const2((1, d)), const2(a2.shape),
                            const2((1, d)), const2((1, d)), const2((1, d)), const2(bd.shape),
                            const2(eye.shape)]
    args = [x.reshape(b, t, x.shape[-1]) for x in (r, k, v, lw, la)]
    args += [w2, w0.reshape(1, d), a2, a0.reshape(1, d), kk.reshape(1, d), ka.reshape(1, d), bd, eye]
    if s0 is not None:
        in_specs.append(state)
        args.append(s0)
    rows = nb * ng * RW_HEAD
    y, st = pl.pallas_call(
        functools.partial(_scan_body, has_s0=s0 is not None),
        grid=(b // nb, t // tt),
        in_specs=in_specs,
        out_specs=[tok, state],
        out_shape=[jax.ShapeDtypeStruct((b, t, d), F32),
                   jax.ShapeDtypeStruct((b, n_heads, RW_HEAD, RW_HEAD), F32)],
        scratch_shapes=[pltpu.VMEM((nb, ng, RW_HEAD, LANE_GROUP), F32)]
        + [pltpu.VMEM((nb, tt, d), F32)] * 6
        + [pltpu.VMEM((SCAN_UNROLL, nb, 3 * ng * RW_HEAD, LANE_GROUP), BF16)],
        compiler_params=_params(("parallel", "arbitrary")),
        name="rwkv_scan",
    )(*args)
    return y.reshape(n, d), st


def _rwkv_post_body(y_ref, r_ref, k_ref, v_ref, la_ref, lg_ref, a2_ref, a0_ref, g2_ref, lnw_ref,
                    lnb_ref, rk_ref, ka_ref, bd_ref, z_ref):
    bd = bd_ref[...]
    y = y_ref[...]
    inv = 1.0 / RW_HEAD
    mean = _head_sum_rows(y, bd) * inv
    yc = y - mean
    var = _head_sum_rows(yc * yc, bd) * inv
    yn = yc * lax.rsqrt(var + RW_LN_EPS) * lnw_ref[...] + lnb_ref[...]
    second = lambda x, w_ref: jnp.dot(x.astype(BF16), w_ref[...].astype(BF16), preferred_element_type=F32)
    a = jax.nn.sigmoid(second(la_ref[...], a2_ref) + a0_ref[...])
    gate = second(lg_ref[...], g2_ref)
    km = k_ref[...] * (1.0 + (a - 1.0) * ka_ref[...])
    bonus = _head_sum_rows(r_ref[...] * km * rk_ref[...], bd) * v_ref[...]
    z_ref[...] = ((yn + bonus) * gate).astype(BF16)


def _rwkv_post(y, r, k, v, la, lg, a2, a0, g2, lnw, lnb, rk, ka):
    n, d = y.shape
    bd, _ = _group_consts()
    rows = min(EW_ROWS, n)
    tok = pl.BlockSpec((rows, d), lambda i: (i, 0))
    low = lambda x: pl.BlockSpec((rows, x.shape[-1]), lambda i: (i, 0))
    vec = pl.BlockSpec((1, d), lambda i: (0, 0))
    whole = lambda x: pl.BlockSpec(x.shape, lambda i: (0, 0))
    return pl.pallas_call(
        _rwkv_post_body,
        grid=(n // rows,),
        in_specs=[tok] * 4 + [low(la), low(lg), whole(a2), vec, whole(g2)] + [vec] * 4 + [whole(bd)],
        out_specs=tok,
        out_shape=jax.ShapeDtypeStruct((n, d), BF16),
        compiler_params=_params(("parallel",)),
        name="rwkv_post",
    )(y, r, k, v, la, lg, a2, a0.reshape(1, d), g2, lnw.reshape(1, d), lnb.reshape(1, d),
      rk.reshape(1, d), ka.reshape(1, d), bd)


def _attn_unit(q, k_prev, v_prev, k_cur, v_cur, sink_ref, prev_limit, o_ref, row0):
    tq = q.shape[0]
    kvh = k_cur.shape[-1] // ATT_HEAD_DIM
    n_heads = kvh * GQA
    qi = lax.broadcasted_iota(jnp.int32, (tq, 1), 0)
    jp = lax.broadcasted_iota(jnp.int32, (1, WINDOW), 1)
    jc = lax.broadcasted_iota(jnp.int32, (1, tq), 1)
    diff_p = WINDOW + qi - jp
    diff_c = qi - jc
    valid_p = diff_p < prev_limit
    valid_c = diff_c >= 0
    diff_pf, diff_cf = diff_p.astype(F32), diff_c.astype(F32)
    ones_p = (lax.broadcasted_iota(jnp.int32, (WINDOW, ATT_HEAD_DIM), 1) == 0).astype(F32)
    ones_c = ones_p[:tq]
    nt = (((1,), (1,)), ((), ()))
    scale = ATT_HEAD_DIM ** -0.5
    for kh in range(kvh):
        heads = [kh * GQA + h for h in range(GQA)]
        qs = (jnp.concatenate([q[:, h * ATT_HEAD_DIM:(h + 1) * ATT_HEAD_DIM] for h in heads], axis=0)
              * scale).astype(BF16)
        sl = slice(kh * ATT_HEAD_DIM, (kh + 1) * ATT_HEAD_DIM)
        kp, kc = k_prev[:, sl].astype(BF16), k_cur[:, sl].astype(BF16)
        vp = jnp.concatenate([v_prev[:, sl], ones_p], axis=-1).astype(BF16)
        vc = jnp.concatenate([v_cur[:, sl], ones_c], axis=-1).astype(BF16)
        s_p_all = lax.dot_general(qs, kp, nt, preferred_element_type=F32)
        s_c_all = lax.dot_general(qs, kc, nt, preferred_element_type=F32)
        p_p, p_c, m_all = [], [], []
        for hq, h in enumerate(heads):
            slope = 2.0 ** (-8.0 * (h + 1) / n_heads)
            s_p = jnp.where(valid_p, s_p_all[hq * tq:(hq + 1) * tq] - slope * diff_pf, NEG_INF)
            s_c = jnp.where(valid_c, s_c_all[hq * tq:(hq + 1) * tq] - slope * diff_cf, NEG_INF)
            if tq == WINDOW:
                m = jnp.max(jnp.maximum(s_p, s_c), axis=-1, keepdims=True)
            else:
                m = jnp.maximum(jnp.max(s_p, axis=-1, keepdims=True),
                                jnp.max(s_c, axis=-1, keepdims=True))
            m = jnp.maximum(m, sink_ref[h])
            p_p.append(jnp.exp(s_p - m))
            p_c.append(jnp.exp(s_c - m))
            m_all.append(m)
        o_all = (jnp.dot(jnp.concatenate(p_p, axis=0).astype(BF16), vp, preferred_element_type=F32)
                 + jnp.dot(jnp.concatenate(p_c, axis=0).astype(BF16), vc, preferred_element_type=F32))
        for hq, h in enumerate(heads):
            o_h = o_all[hq * tq:(hq + 1) * tq]
            den = o_h[:, ATT_HEAD_DIM:ATT_HEAD_DIM + 1] + jnp.exp(sink_ref[h] - m_all[hq])
            o_ref[pl.ds(row0, tq), pl.ds(h * ATT_HEAD_DIM, ATT_HEAD_DIM)] = (
                o_h[:, :ATT_HEAD_DIM] / den).astype(o_ref.dtype)


def _attn_prefill_body(q_ref, kvc_ref, kvp_ref, sink_ref, o_ref):
    kvd = kvc_ref.shape[-1] // 2
    kvc, kvp = kvc_ref[...], kvp_ref[...]
    prev_limit = jnp.where(pl.program_id(1) > 0, WINDOW, 0)
    _attn_unit(q_ref[...], kvp[:, :kvd], kvp[:, kvd:], kvc[:, :kvd], kvc[:, kvd:], sink_ref,
               prev_limit, o_ref, 0)


def _attn_decode_body(q_ref, kvc_ref, kc_ref, vc_ref, sink_ref, o_ref, *, nb, tq):
    kvd = kvc_ref.shape[-1] // 2
    kvh = kvd // ATT_HEAD_DIM
    n_heads = kvh * GQA
    rows = n_heads * tq
    head = lax.broadcasted_iota(jnp.int32, (rows, 1), 0) // tq
    qi = lax.broadcasted_iota(jnp.int32, (rows, 1), 0) % tq
    slope = jnp.exp2(-8.0 * (head.astype(F32) + 1.0) / n_heads)
    sink = sink_ref[...]

    def key_axis(n_keys):
        c = lax.broadcasted_iota(jnp.int32, (1, kvh * n_keys), 1)
        return c // n_keys, c % n_keys

    kh_p, jp = key_axis(WINDOW)
    kh_c, jc = key_axis(tq)
    own_p = kh_p == head // GQA
    own_c = kh_c == head // GQA
    diff_p = WINDOW + qi - jp
    diff_c = qi - jc
    valid_p = jnp.where(own_p, diff_p, WINDOW) < WINDOW
    valid_c = jnp.where(own_c, diff_c, -1) >= 0
    bias_p = jnp.where(valid_p, -slope * diff_p.astype(F32), NEG_INF)
    bias_c = jnp.where(valid_c, -slope * diff_c.astype(F32), NEG_INF)
    ones = (lax.broadcasted_iota(jnp.int32, (WINDOW, ATT_HEAD_DIM), 1) == 0).astype(F32)
    nt = (((1,), (1,)), ((), ()))
    scale = ATT_HEAD_DIM ** -0.5
    head_cols = lambda x, h: x[:, h * ATT_HEAD_DIM:(h + 1) * ATT_HEAD_DIM]

    def stack_keys(x):
        return jnp.concatenate([head_cols(x, kh) for kh in range(kvh)], axis=0).astype(BF16)

    def stack_values(x):
        one = ones[:x.shape[0]]
        return jnp.concatenate(
            [jnp.concatenate([head_cols(x, kh), one], axis=-1) for kh in range(kvh)], axis=0).astype(BF16)

    for n in range(nb):
        q = q_ref[pl.ds(n * tq, tq), :]
        kvc = kvc_ref[pl.ds(n * tq, tq), :]
        qs = (jnp.concatenate([head_cols(q, h) for h in range(n_heads)], axis=0) * scale).astype(BF16)
        s_p = lax.dot_general(qs, stack_keys(kc_ref[n]), nt, preferred_element_type=F32) + bias_p
        s_c = lax.dot_general(qs, stack_keys(kvc[:, :kvd]), nt, preferred_element_type=F32) + bias_c
        m = jnp.maximum(jnp.maximum(jnp.max(s_p, axis=-1, keepdims=True),
                                    jnp.max(s_c, axis=-1, keepdims=True)), sink)
        o_all = (jnp.dot(jnp.exp(s_p - m).astype(BF16), stack_values(vc_ref[n]),
                         preferred_element_type=F32)
                 + jnp.dot(jnp.exp(s_c - m).astype(BF16), stack_values(kvc[:, kvd:]),
                           preferred_element_type=F32))
        den = o_all[:, ATT_HEAD_DIM:ATT_HEAD_DIM + 1] + jnp.exp(sink - m)
        o = o_all[:, :ATT_HEAD_DIM] / den
        for h in range(n_heads):
            o_ref[pl.ds(n * tq, tq), pl.ds(h * ATT_HEAD_DIM, ATT_HEAD_DIM)] = o[h * tq:(h + 1) * tq]


def _attn_prefill(q, kv, sinks, b, t):
    n, dq = q.shape
    kv2 = kv.shape[-1]
    nblk = t // WINDOW
    return pl.pallas_call(
        _attn_prefill_body,
        grid=(b, nblk),
        in_specs=[
            pl.BlockSpec((WINDOW, dq), lambda i, j: (i * nblk + j, 0)),
            pl.BlockSpec((WINDOW, kv2), lambda i, j: (i * nblk + j, 0)),
            pl.BlockSpec((WINDOW, kv2), lambda i, j: (i * nblk + jnp.maximum(j - 1, 0), 0)),
            pl.BlockSpec(memory_space=pltpu.SMEM),
        ],
        out_specs=pl.BlockSpec((WINDOW, dq), lambda i, j: (i * nblk + j, 0)),
        out_shape=jax.ShapeDtypeStruct((n, dq), BF16),
        compiler_params=_params(("parallel", "arbitrary")),
        name="attn_prefill",
    )(q, kv, kv, sinks.astype(F32))


def _attn_decode(q, kv, cache_k, cache_v, sinks, b, t):
    n, dq = q.shape
    kv2 = kv.shape[-1]
    kvd = kv2 // 2
    nb = min(ATT_NB, b)
    rows = nb * t
    return pl.pallas_call(
        functools.partial(_attn_decode_body, nb=nb, tq=t),
        grid=(b // nb,),
        in_specs=[
            pl.BlockSpec((rows, dq), lambda i: (i, 0)),
            pl.BlockSpec((rows, kv2), lambda i: (i, 0)),
            pl.BlockSpec((nb, WINDOW, kvd), lambda i: (i, 0, 0)),
            pl.BlockSpec((nb, WINDOW, kvd), lambda i: (i, 0, 0)),
            pl.BlockSpec((sinks.shape[0] * t, 1), lambda i: (0, 0)),
        ],
        out_specs=pl.BlockSpec((rows, dq), lambda i: (i, 0)),
        out_shape=jax.ShapeDtypeStruct((n, dq), F32),
        compiler_params=_params(("parallel",)),
        name="attn_decode",
    )(q, kv, cache_k.reshape(b, WINDOW, kvd), cache_v.reshape(b, WINDOW, kvd),
      jnp.repeat(sinks.astype(F32), t)[:, None])


def _moe_body(te_ref, tv_ref, tok_ref, dst_ref, h_hbm, gate_ref, wg_ref, wu_ref, wd_ref, out_hbm,
              xbuf, obuf, gsem, ssem, wg16, wu16, wd16):
    t = pl.program_id(0)
    last = pl.num_programs(0) - 1
    slot = t % 2

    def split(row):
        return lax.shift_right_logical(row, 3), row & (SUBLANES - 1)

    def gather_of(buf):
        def copy(i, u, tok):
            hi, lo = split(tok)
            return pltpu.make_async_copy(h_hbm.at[hi, pl.ds(lo, 1)], xbuf.at[buf, i, pl.ds(u, 1)],
                                         gsem.at[buf])
        return tok_ref, copy

    def scatter_copy(i, u, dst):
        hi, lo = split(dst)
        return pltpu.make_async_copy(obuf.at[i, pl.ds(u, 1)], out_hbm.at[hi, pl.ds(lo, 1)], ssem.at[0])

    scatter = (dst_ref, scatter_copy)

    def for_rows(tile, rows, act):
        table, build = rows
        groups = lax.shift_right_logical(tv_ref[tile] + (SUBLANES - 1), 3)

        def body(i, carry):
            base = tile * MOE_TG + i * SUBLANES
            entries = [table[base + u] for u in range(SUBLANES)]
            for u in range(SUBLANES):
                act(build(i, u, entries[u]))
            return carry
        lax.fori_loop(0, groups, body, 0)

    start = lambda copy: copy.start()
    wait = lambda copy: copy.wait()

    groups_per_tile = MOE_TG // SUBLANES

    @pl.when(t == 0)
    def _():
        xbuf[...] = jnp.zeros_like(xbuf)
        obuf[...] = jnp.zeros_like(obuf)
        spare = pltpu.make_async_copy(
            obuf, out_hbm.at[pl.ds(out_hbm.shape[0] - groups_per_tile, groups_per_tile)], ssem.at[0])
        spare.start()
        spare.wait()
        for_rows(0, gather_of(0), start)

    nxt = jnp.minimum(t + 1, last)

    @pl.when(t < last)
    def _():
        for_rows(nxt, gather_of(1 - slot), start)

    prev = jnp.maximum(t - 1, 0)

    @pl.when((t == 0) | (te_ref[t] != te_ref[prev]))
    def _():
        wg16[...] = wg_ref[...].astype(BF16)
        wu16[...] = wu_ref[...].astype(BF16)
        wd16[...] = wd_ref[...].astype(BF16)

    @pl.when(tv_ref[t] > 0)
    def _():
        for_rows(t, gather_of(slot), wait)
        x = xbuf[slot].reshape(MOE_TG, -1).astype(BF16)
        hg = jnp.dot(x, wg16[...], preferred_element_type=F32)
        hu = jnp.dot(x, wu16[...], preferred_element_type=F32)
        act = (hg * jax.nn.sigmoid(hg)) * hu * gate_ref[...]
        o = jnp.dot(act.astype(BF16), wd16[...], preferred_element_type=F32)

        @pl.when(t > 0)
        def _():
            for_rows(prev, scatter, wait)

        obuf[...] = o.reshape(obuf.shape)
        for_rows(t, scatter, start)

        @pl.when(t == last)
        def _():
            for_rows(t, scatter, wait)

    @pl.when((tv_ref[t] == 0) & (t > 0))
    def _():
        for_rows(prev, scatter, wait)


def _moe_plan(route, n_tiles):
    n = route.shape[0]
    e = route[:, :2].astype(jnp.int32).reshape(-1)
    w = route[:, 2:4].reshape(-1)
    onehot = (e[:, None] == jnp.arange(N_EXPERTS, dtype=jnp.int32)[None, :]).astype(jnp.int32)
    csum = jnp.cumsum(onehot, axis=0)
    rank = jnp.take_along_axis(csum, e[:, None], axis=1)[:, 0] - 1
    counts = csum[-1]
    padded = ((counts + MOE_TG - 1) // MOE_TG) * MOE_TG
    pend = jnp.cumsum(padded)
    dest = (pend - padded)[e] + rank
    p_rows = n_tiles * MOE_TG
    vals = jnp.stack([jnp.arange(1, 2 * n + 1, dtype=jnp.int32), lax.bitcast_convert_type(w, jnp.int32)],
                     axis=1)
    placed = jnp.zeros((p_rows, 2), jnp.int32).at[dest].set(vals, unique_indices=True,
                                                             mode="promise_in_bounds")
    has_pair = placed[:, 0] > 0
    pair = placed[:, 0] - 1
    row_token = jnp.where(has_pair, pair // 2, 0)
    spare = 2 * n + jnp.arange(p_rows, dtype=jnp.int32) % MOE_TG
    row_dst = jnp.where(has_pair, (pair % 2) * n + pair // 2, spare)
    row_gate = jnp.where(has_pair, lax.bitcast_convert_type(placed[:, 1], F32), 0.0)
    starts = jnp.arange(n_tiles, dtype=jnp.int32) * MOE_TG
    tile_expert = jnp.minimum(jnp.searchsorted(pend, starts, side="right"), N_EXPERTS - 1)
    tile_rows = jnp.clip(counts[tile_expert] - (starts - (pend - padded)[tile_expert]), 0, MOE_TG)
    tile_rows = jnp.where(starts < pend[-1], tile_rows, 0).astype(jnp.int32)
    last_used = jnp.max(jnp.where(tile_rows > 0, tile_expert, 0))
    tile_expert = jnp.where(tile_rows > 0, tile_expert, last_used).astype(jnp.int32)
    return row_token, row_dst, row_gate, tile_expert, tile_rows


def _moe(h, route, w_gate, w_up, w_down, layer):
    n, d = h.shape
    group = (SUBLANES, d)
    f = w_gate.shape[-1]
    n_tiles = (2 * n) // MOE_TG + N_EXPERTS
    n_out = 2 * n + MOE_TG
    assert n % SUBLANES == 0
    row_token, row_dst, row_gate, tile_expert, tile_rows = _moe_plan(route, n_tiles)
    out = pl.pallas_call(
        _moe_body,
        grid_spec=pltpu.PrefetchScalarGridSpec(
            num_scalar_prefetch=4,
            grid=(n_tiles,),
            in_specs=[
                pl.BlockSpec(memory_space=pl.ANY),
                pl.BlockSpec((MOE_TG, 1), lambda t, te, tv, rt, rd: (t, 0)),
                pl.BlockSpec((None, None, d, f), lambda t, te, tv, rt, rd: (layer, te[t], 0, 0)),
                pl.BlockSpec((None, None, d, f), lambda t, te, tv, rt, rd: (layer, te[t], 0, 0)),
                pl.BlockSpec((None, None, f, d), lambda t, te, tv, rt, rd: (layer, te[t], 0, 0)),
            ],
            out_specs=pl.BlockSpec(memory_space=pl.ANY),
            scratch_shapes=[pltpu.VMEM((2, MOE_TG // SUBLANES) + group, F32),
                            pltpu.VMEM((MOE_TG // SUBLANES,) + group, F32),
                            pltpu.SemaphoreType.DMA((2,)), pltpu.SemaphoreType.DMA((1,)),
                            pltpu.VMEM((d, f), BF16), pltpu.VMEM((d, f), BF16),
                            pltpu.VMEM((f, d), BF16)],
        ),
        out_shape=jax.ShapeDtypeStruct((n_out // SUBLANES,) + group, F32),
        compiler_params=_params(("arbitrary",)),
        name="moe_ffn",
    )(tile_expert, tile_rows, row_token, row_dst, h.reshape((n // SUBLANES,) + group),
      row_gate.reshape(-1, 1), w_gate, w_up, w_down)
    return out.reshape(n_out, d)


def _split_mod(mod, n_parts, trunk_rows):
    mod = mod[:, None, :]
    return [[(mod, k, row0) for k in range(n_parts)] for row0 in trunk_rows]


def kernel(x_prompt, x_sample, c_prompt, c_sample, state_wkv, state_shift, cache_k_win, cache_v_win, ada_w, ada_b, ln_g, ln_b, rw_mu, rw_w_rkv, rw_w0, rw_w1, rw_w2, rw_a0, rw_a1, rw_a2, rw_g1, rw_g2, rw_k_k, rw_k_a, rw_r_k, rw_lnx_w, rw_lnx_b, rw_wo, ada_kv_w, ada_kv_b, w_kv, w_q, attn_sinks, w_o_attn, router_w, router_bias, moe_w_gate, moe_w_up, moe_w_down):
    d = x_prompt.shape[-1]
    xs = [x_prompt, x_sample]
    shapes = [x.shape[:2] for x in xs]
    counts = [b * t for b, t in shapes]
    bp = shapes[0][0]

    c_all = jnp.concatenate([c_sample, c_prompt], axis=0)
    trunk_rows = [shapes[1][0], 0]
    mods = {(l, s): _split_mod(_mm(c_all, ada_w, name="mm_ada", w_lead=(l, s), bias=ada_b[l, s],
                                   in_act="silu"), 3, trunk_rows)
            for l in range(DEPTH) for s in range(2)}
    mods_kv = _split_mod(_mm(c_all, ada_kv_w, name="mm_ada_kv", bias=ada_kv_b, in_act="silu"), 2,
                         trunk_rows)

    by_slot = lambda x: x.reshape(-1, N_GROUPS, EXPERTS_PER_GROUP).swapaxes(1, 2).reshape(-1, N_EXPERTS)
    router = (by_slot(router_w), by_slot(router_bias))

    n_all = counts[0] + counts[1]
    row0 = [0, counts[0]]

    def moe_layer(h_all, routes, layer):
        out = _moe(h_all, jnp.concatenate(routes, axis=0), moe_w_gate, moe_w_up, moe_w_down, layer)
        return [[(out, slot * n_all + row0[tr]) for slot in range(2)] for tr in range(2)]

    firsts = [jnp.zeros((bp, 1, d), F32), state_shift[0][:, None, :]]
    s0s = [None, state_wkv[0]]
    x1, h1, route1, states, hlasts = [], None, [], [], []
    for tr in range(2):
        b, t = shapes[tr]
        shift, scale, gate = mods[0, 0][tr]
        xmix, hlast, lw, la, lg = _premix(xs[tr], shift, scale, firsts[tr], rw_mu[0], rw_w1[0],
                                          rw_a1[0], rw_g1[0])
        r = _mm(xmix, rw_w_rkv, name="mm_r", x_lead=(0,), w_lead=(0, 0), tm=MM_TM_BF16)
        k = _mm(xmix, rw_w_rkv, name="mm_k", x_lead=(1,), w_lead=(0, 1), tm=MM_TM_BF16)
        v = _mm(xmix, rw_w_rkv, name="mm_v", x_lead=(2,), w_lead=(0, 2), tm=MM_TM_BF16)
        y, st = _rwkv_scan(r, k, v, lw, la, rw_w2[0], rw_w0[0], rw_a2[0], rw_a0[0], rw_k_k[0],
                           rw_k_a[0], s0s[tr], b, t)
        z = _rwkv_post(y, r, k, v, la, lg, rw_a2[0], rw_a0[0], rw_g2[0], rw_lnx_w[0], rw_lnx_b[0],
                       rw_r_k[0], rw_k_a[0])
        out = _mm(z, rw_wo, name="mm_wo", w_lead=(0,), tm=MM_TM_BF16)
        sh, sc, _ = mods[0, 1][tr]
        xn, h1, route = _ln_mod(xs[tr], [(out, 0)], gate, ln_g[0, 0], ln_b[0, 0], [(sh, sc)], router,
                                (h1, n_all, row0[tr]))
        x1.append(xn), route1.append(route), states.append(st), hlasts.append(hlast)
    moe1 = moe_layer(h1, route1, 0)

    caches = [None, (cache_k_win, cache_v_win)]
    x3, h3, route3, kvs = [], None, [], []
    for tr in range(2):
        b, t = shapes[tr]
        shq, scq, gate_q = mods[1, 0][tr]
        shk, sck = mods_kv[tr]
        x2, hq, hkv = _ln_mod(x1[tr], moe1[tr], mods[0, 1][tr][2], ln_g[0, 1], ln_b[0, 1],
                              [(shq, scq), (shk, sck)])
        q = _mm(hq, w_q, name="mm_q", w_lead=(0,), tm=MM_TM_BF16)
        kv = _mm(hkv, w_kv, name="mm_kv", tm=MM_TM_BF16)
        if caches[tr] is None:
            o = _attn_prefill(q, kv, attn_sinks[0], b, t)
        else:
            o = _attn_decode(q, kv, caches[tr][0], caches[tr][1], attn_sinks[0], b, t)
        out = _mm(o, w_o_attn, name="mm_o", w_lead=(0,), tm=MM_TM_BF16 if o.dtype == BF16 else MM_TM)
        sh, sc, _ = mods[1, 1][tr]
        xn, h3, route = _ln_mod(x2, [(out, 0)], gate_q, ln_g[1, 0], ln_b[1, 0], [(sh, sc)], router,
                                (h3, n_all, row0[tr]))
        x3.append(xn), route3.append(route), kvs.append(kv)
    moe3 = moe_layer(h3, route3, 1)
    ys = [_ln_mod(x3[tr], moe3[tr], mods[1, 1][tr][2], ln_g[1, 1], ln_b[1, 1], [])[0]
          for tr in range(2)]

    kvd = kvs[0].shape[-1] // 2
    kvh = kvd // ATT_HEAD_DIM
    (bp, tp), (bs, ts) = shapes
    kv_p = kvs[0].reshape(bp, tp, 2 * kvd)[:, tp - WINDOW:]
    p_k_win = kv_p[..., :kvd].reshape(bp, WINDOW, kvh, ATT_HEAD_DIM)
    p_v_win = kv_p[..., kvd:].reshape(bp, WINDOW, kvh, ATT_HEAD_DIM)
    kv_s = kvs[1].reshape(bs, ts, 2 * kvd)
    s_k_win = jnp.concatenate(
        [cache_k_win, kv_s[..., :kvd].reshape(bs, ts, kvh, ATT_HEAD_DIM)], axis=1)[:, -WINDOW:]
    s_v_win = jnp.concatenate(
        [cache_v_win, kv_s[..., kvd:].reshape(bs, ts, kvh, ATT_HEAD_DIM)], axis=1)[:, -WINDOW:]
    p_wkv = states[0][None]
    s_wkv = states[1][None]
    p_shift = hlasts[0].reshape(1, bp, d)
    s_shift = hlasts[1].reshape(1, bs, d)
    return (ys[0], ys[1], p_wkv, p_shift, p_k_win, p_v_win, s_wkv, s_shift, s_k_win, s_v_win)
```

```python
import functools

import jax
import jax.numpy as jnp
import numpy as np
from jax import lax
from jax.experimental import pallas as pl
from jax.experimental.pallas import tpu as pltpu

F32 = jnp.float32
BF16 = jnp.bfloat16

RW_HEAD = 64
ATT_HEAD_DIM = 64
GQA = 8
WINDOW = 128
N_EXPERTS = 32
EXPERTS_PER_GROUP = 4
N_GROUPS = N_EXPERTS // EXPERTS_PER_GROUP
DEPTH = 2
ALPHA = (2.0 * DEPTH) ** 0.25
LN_EPS = 1e-5
RW_LN_EPS = 64e-5
NEG_INF = -1e30
DECAY_SCALE = float(np.exp(-0.5))
MIX_RKV = (0, 2, 3)
MIX_W, MIX_A, MIX_G = 1, 4, 5

SUBLANES = 8
LANE_GROUP = 256
HEADS_PER_GROUP = LANE_GROUP // RW_HEAD
VMEM_LIMIT = 56 * 1024 * 1024

MM_TM = 1024
MM_TM_BF16 = 2048
MM_TN = 512
EW_ROWS = 256
SCAN_TT = 64
SCAN_NB = 4
SCAN_UNROLL = 8
ATT_NB = 8
MOE_TG = 256


def _params(sem):
    return pltpu.CompilerParams(dimension_semantics=sem, vmem_limit_bytes=VMEM_LIMIT)


def _act(x, kind):
    if kind is None:
        return x
    if kind == "silu":
        return x * jax.nn.sigmoid(x)
    if kind == "tanh":
        return jnp.tanh(x)
    if kind == "sigmoid":
        return jax.nn.sigmoid(x)
    if kind == "decay":
        return jnp.exp(-DECAY_SCALE * jax.nn.sigmoid(x))
    raise ValueError(kind)


def _mm_body(*refs, in_act, out_act, has_bias, stage_x):
    x_ref, w_ref = refs[0], refs[1]
    b_ref = refs[2] if has_bias else None
    o_ref = refs[2 + has_bias]
    if stage_x:
        xs_ref = refs[3 + has_bias]

        @pl.when(pl.program_id(1) == 0)
        def _():
            xs_ref[...] = _act(x_ref[...].astype(F32), in_act).astype(BF16)
    else:
        xs_ref = x_ref

    acc = jnp.dot(xs_ref[...], w_ref[...].astype(BF16), preferred_element_type=F32)
    if has_bias:
        acc = acc + b_ref[...]
    o_ref[...] = _act(acc, out_act).astype(o_ref.dtype)


def _mm(x, w, *, name, x_lead=(), w_lead=(), bias=None, in_act=None, out_act=None, out_dtype=F32,
        tm=MM_TM, tn=MM_TN):
    m, k = x.shape[-2:]
    n = w.shape[-1]
    tm = min(tm, m)
    tn = min(tn, n)
    assert m % tm == 0 and n % tn == 0, (m, tm, n, tn)
    nx, nw = len(x_lead), len(w_lead)
    in_specs = [
        pl.BlockSpec((None,) * nx + (tm, k), lambda i, j: tuple(x_lead) + (i, 0)),
        pl.BlockSpec((None,) * nw + (k, tn), lambda i, j: tuple(w_lead) + (0, j)),
    ]
    args = [x, w]
    if bias is not None:
        in_specs.append(pl.BlockSpec((1, tn), lambda i, j: (0, j)))
        args.append(bias.reshape(1, n).astype(F32))
    stage_x = x.dtype != BF16 or in_act is not None
    body = functools.partial(_mm_body, in_act=in_act, out_act=out_act, has_bias=bias is not None,
                             stage_x=stage_x)
    return pl.pallas_call(
        body,
        grid=(m // tm, n // tn),
        in_specs=in_specs,
        out_specs=pl.BlockSpec((tm, tn), lambda i, j: (i, j)),
        out_shape=jax.ShapeDtypeStruct((m, n), out_dtype),
        scratch_shapes=[pltpu.VMEM((tm, k), BF16)] if stage_x else [],
        compiler_params=_params(("parallel", "arbitrary")),
        name=name,
    )(*args)


def _geo(b, t):
    if t == SUBLANES:
        g = min(EW_ROWS // SUBLANES, b)
        return dict(groups=g, steps=b // g, mod_groups=g, steps_per_batch=1, mod_index=lambda i: i)
    assert t % EW_ROWS == 0
    spb = t // EW_ROWS
    return dict(groups=EW_ROWS // SUBLANES, steps=b * spb, mod_groups=1, steps_per_batch=spb,
                mod_index=lambda i: i // spb)


def _mod_spec(mod, mg, d, block_of_step):
    _, part, row0 = mod
    assert row0 % mg == 0, (row0, mg)
    return pl.BlockSpec((mg, 1, d), lambda i: (row0 // mg + block_of_step(i), 0, part))


def _premix_body(x_ref, xprev_ref, first_ref, shift_ref, scale_ref, mu_ref, w1_ref, a1_ref, g1_ref,
                 xmix_ref, hlast_ref, lw_ref, la_ref, lg_ref, *, steps_per_batch):
    i = pl.program_id(0)
    g, s, d = x_ref.shape
    gf = first_ref.shape[0]
    scale = scale_ref[...]
    shift = shift_ref[...]
    h3 = x_ref[...] * (1.0 + scale) + shift
    hlast_ref[...] = h3[g - gf:, s - 1:, :]
    h = h3.reshape(g * s, d)
    if steps_per_batch > 1:
        hprev_row = xprev_ref[:, s - 1:, :] * (1.0 + scale) + shift
        first = jnp.where(i % steps_per_batch == 0, first_ref[...], hprev_row)
        period = g * s
    else:
        first = first_ref[...]
        period = s
    first2 = jnp.broadcast_to(first, (g, s, d)).reshape(g * s, d)
    row = lax.broadcasted_iota(jnp.int32, (g * s, 1), 0)
    hp = jnp.where(row % period == 0, first2, pltpu.roll(h, 1, 0))
    xx = hp - h
    mix = lambda m: (h + xx * mu_ref[m]).astype(BF16)
    for slot, m in enumerate(MIX_RKV):
        xmix_ref[slot] = mix(m)
    lora = lambda m, w_ref: jnp.dot(mix(m), w_ref[...].astype(BF16), preferred_element_type=F32)
    lw_ref[...] = jnp.tanh(lora(MIX_W, w1_ref))
    la_ref[...] = lora(MIX_A, a1_ref)
    lg_ref[...] = jax.nn.sigmoid(lora(MIX_G, g1_ref))


def _premix(x, shift, scale, first, mu, w1, a1, g1):
    b, t, d = x.shape
    n = b * t
    n_mix = mu.shape[0]
    c = _geo(b, t)
    g, mg, midx = c["groups"], c["mod_groups"], c["mod_index"]
    rows = g * SUBLANES
    xg = x.reshape(-1, SUBLANES, d)
    per_batch = pl.BlockSpec((mg, 1, d), lambda i: (midx(i), 0, 0))
    mod_spec = lambda mod: _mod_spec(mod, mg, d, midx)
    whole = lambda a: pl.BlockSpec(a.shape, lambda i: (0,) * a.ndim)
    lora_out = lambda w: (pl.BlockSpec((rows, w.shape[-1]), lambda i: (i, 0)),
                          jax.ShapeDtypeStruct((n, w.shape[-1]), F32))
    lora_specs, lora_shapes = zip(lora_out(w1), lora_out(a1), lora_out(g1))
    return pl.pallas_call(
        functools.partial(_premix_body, steps_per_batch=c["steps_per_batch"]),
        grid=(c["steps"],),
        in_specs=[
            pl.BlockSpec((g, SUBLANES, d), lambda i: (i, 0, 0)),
            pl.BlockSpec((1, SUBLANES, d), lambda i: (jnp.maximum(i * g - 1, 0), 0, 0)),
            per_batch, mod_spec(shift), mod_spec(scale),
            pl.BlockSpec((n_mix, 1, d), lambda i: (0, 0, 0)),
            whole(w1), whole(a1), whole(g1),
        ],
        out_specs=[pl.BlockSpec((len(MIX_RKV), rows, d), lambda i: (0, i, 0)), per_batch,
                   *lora_specs],
        out_shape=[jax.ShapeDtypeStruct((len(MIX_RKV), n, d), BF16),
                   jax.ShapeDtypeStruct((b, 1, d), F32), *lora_shapes],
        compiler_params=_params(("arbitrary",)),
        name="premix",
    )(xg, xg, first, shift[0], scale[0], mu.reshape(n_mix, 1, d), w1, a1, g1)


def _split_bf16(x):
    hi = x.astype(BF16)
    return hi, (x - hi.astype(F32)).astype(BF16)


def _route(h, rw_ref, rb_ref):
    t = h.shape[0]
    h_hi, h_lo = _split_bf16(h)
    w_hi, w_lo = _split_bf16(rw_ref[...])
    logits = (jnp.dot(h_hi, w_hi, preferred_element_type=F32)
              + jnp.dot(h_lo, w_hi, preferred_element_type=F32)
              + jnp.dot(h_hi, w_lo, preferred_element_type=F32))
    s_all = jax.nn.sigmoid(logits)
    sb_all = s_all + rb_ref[...]
    s = [s_all[:, e * N_GROUPS:(e + 1) * N_GROUPS] for e in range(EXPERTS_PER_GROUP)]
    sb = [sb_all[:, e * N_GROUPS:(e + 1) * N_GROUPS] for e in range(EXPERTS_PER_GROUP)]
    hi01, lo01 = jnp.maximum(sb[0], sb[1]), jnp.minimum(sb[0], sb[1])
    hi23, lo23 = jnp.maximum(sb[2], sb[3]), jnp.minimum(sb[2], sb[3])
    top1 = jnp.maximum(hi01, hi23)
    top2 = jnp.maximum(jnp.minimum(hi01, hi23), jnp.maximum(lo01, lo23))
    gscore = top1 + top2
    lane = lax.broadcasted_iota(jnp.int32, (t, N_GROUPS), 1)
    gmax = jnp.max(gscore, axis=-1, keepdims=True)
    gsel = jnp.min(jnp.where(gscore == gmax, lane, N_GROUPS), axis=-1, keepdims=True)
    in_group = lane == gsel
    masked = [jnp.where(in_group, v, NEG_INF) for v in sb]
    idx = [lane * EXPERTS_PER_GROUP + e for e in range(EXPERTS_PER_GROUP)]

    def pick(vals):
        best = functools.reduce(jnp.maximum, [jnp.max(v, axis=-1, keepdims=True) for v in vals])
        return functools.reduce(jnp.minimum, [
            jnp.min(jnp.where(v == best, ix, N_EXPERTS), axis=-1, keepdims=True)
            for v, ix in zip(vals, idx)])

    def weight(sel):
        return functools.reduce(jnp.add, [
            jnp.sum(jnp.where(ix == sel, v, 0.0), axis=-1, keepdims=True) for v, ix in zip(s, idx)])

    e1 = pick(masked)
    e2 = pick([jnp.where(ix == e1, -jnp.inf, v) for v, ix in zip(masked, idx)])
    w1, w2 = weight(e1), weight(e2)
    wsum = w1 + w2
    return jnp.where(lane == 0, e1.astype(F32),
                     jnp.where(lane == 1, e2.astype(F32),
                               jnp.where(lane == 2, w1 / wsum, jnp.where(lane == 3, w2 / wsum, 0.0))))


def _ln_mod_body(*refs, n_planes, n_mods, with_route, steps):
    if with_route:
        @pl.when(pl.program_id(0) < steps)
        def _():
            _ln_mod_compute(*refs, n_planes=n_planes, n_mods=n_mods, with_route=True)

        @pl.when(pl.program_id(0) >= steps)
        def _():
            h_ref = refs[-2]
            h_ref[...] = jnp.zeros_like(h_ref)
    else:
        _ln_mod_compute(*refs, n_planes=n_planes, n_mods=n_mods, with_route=False)


def _ln_mod_compute(*refs, n_planes, n_mods, with_route):
    it = iter(refs)
    x_ref = next(it)
    o_refs = [next(it) for _ in range(n_planes)]
    gate_ref, lng_ref, lnb_ref = next(it), next(it), next(it)
    mod_refs = [(next(it), next(it)) for _ in range(n_mods)]
    rw_ref = next(it) if with_route else None
    rb_ref = next(it) if with_route else None
    if with_route:
        next(it)
    xnew_ref = next(it)
    h_refs = [next(it) for _ in range(n_mods)]
    route_ref = next(it) if with_route else None

    g, s, d = x_ref.shape

    out = o_refs[0][...]
    for o_ref in o_refs[1:]:
        out = out + o_ref[...]
    y = ALPHA * x_ref[...] + gate_ref[...] * out
    mu = jnp.mean(y, axis=-1, keepdims=True)
    yc = y - mu
    var = jnp.mean(yc * yc, axis=-1, keepdims=True)
    xn = yc * lax.rsqrt(var + LN_EPS) * lng_ref[...] + lnb_ref[...]
    xnew_ref[...] = xn
    for k, ((shift_ref, scale_ref), h_ref) in enumerate(zip(mod_refs, h_refs)):
        h = (xn * (1.0 + scale_ref[...]) + shift_ref[...]).reshape(g * s, d)
        h_ref[...] = h.astype(h_ref.dtype)
        if with_route:
            route_ref[...] = _route(h, rw_ref, rb_ref)


def _ln_mod(x, planes, gate, ln_g, ln_b, mods, router=None, moe_rows=None):
    b, t, d = x.shape
    n = b * t
    c = _geo(b, t)
    g, mg, midx = c["groups"], c["mod_groups"], c["mod_index"]
    rows = g * SUBLANES
    with_route = router is not None
    steps = c["steps"]
    tail_steps = 0
    if with_route and moe_rows[0] is None:
        assert moe_rows[2] == 0 and (moe_rows[1] - n) % rows == 0
        tail_steps = (moe_rows[1] - n) // rows
    cl = lambda i: jnp.minimum(i, steps - 1)
    tok = pl.BlockSpec((g, SUBLANES, d), lambda i: (cl(i), 0, 0))
    mod_spec = lambda mod: _mod_spec(mod, mg, d, lambda i: midx(cl(i)))
    const = pl.BlockSpec((1, 1, d), lambda i: (0, 0, 0))
    in_specs = [tok]
    args = [x.reshape(-1, SUBLANES, d)]
    for arr, row0 in planes:
        assert row0 % rows == 0, (row0, rows)
        in_specs.append(pl.BlockSpec((g, SUBLANES, d), lambda i, off=row0 // rows: (off + cl(i), 0, 0)))
        args.append(arr.reshape(-1, SUBLANES, d))
    in_specs += [mod_spec(gate), const, const]
    args += [gate[0], ln_g.reshape(1, 1, d), ln_b.reshape(1, 1, d)]
    for shift, scale in mods:
        in_specs += [mod_spec(shift), mod_spec(scale)]
        args += [shift[0], scale[0]]
    out_shape = [jax.ShapeDtypeStruct((n // SUBLANES, SUBLANES, d), F32)]
    out_specs = [tok]
    aliases = {}
    if with_route:
        in_specs += [pl.BlockSpec(a.shape, lambda i: (0, 0)) for a in router]
        args += list(router)
        buf, total, row0 = moe_rows
        if buf is None:
            in_specs.append(pl.BlockSpec((1, 128), lambda i: (0, 0)))
            args.append(jnp.zeros((1, 128), F32))
        else:
            in_specs.append(pl.BlockSpec(memory_space=pl.ANY))
            args.append(buf)
            aliases = {len(args) - 1: 1}
        out_shape.append(jax.ShapeDtypeStruct((total, d), F32))
        out_specs.append(pl.BlockSpec((rows, d), lambda i, off=row0 // rows: (off + i, 0)))
        out_shape.append(jax.ShapeDtypeStruct((n, N_GROUPS), F32))
        out_specs.append(pl.BlockSpec((rows, N_GROUPS), lambda i: (cl(i), 0)))
    else:
        for _ in mods:
            out_shape.append(jax.ShapeDtypeStruct((n, d), BF16))
            out_specs.append(pl.BlockSpec((rows, d), lambda i: (i, 0)))
    outs = pl.pallas_call(
        functools.partial(_ln_mod_body, n_planes=len(planes), n_mods=len(mods), with_route=with_route,
                          steps=steps),
        grid=(steps + tail_steps,),
        in_specs=in_specs,
        out_specs=out_specs,
        out_shape=out_shape,
        input_output_aliases=aliases,
        compiler_params=_params(("arbitrary",)),
        name="ln_mod",
    )(*args)
    return [outs[0].reshape(b, t, d)] + list(outs[1:])


def _group_consts():
    lane = np.arange(LANE_GROUP)
    bd = (lane[:, None] // RW_HEAD == lane[None, :] // RW_HEAD).astype(np.float32)
    eye = (np.arange(RW_HEAD)[:, None] == lane[None, :] % RW_HEAD).astype(np.float32)
    return jnp.asarray(bd, BF16), jnp.asarray(eye, F32)


def _head_sum(x, bd):
    hi = x.astype(BF16)
    lo = (x - hi.astype(F32)).astype(BF16)
    return jnp.dot(hi, bd, preferred_element_type=F32) + jnp.dot(lo, bd, preferred_element_type=F32)


def _head_sum_rows(x, bd):
    d = x.shape[-1]
    return jnp.concatenate(
        [_head_sum(x[:, c:c + LANE_GROUP], bd) for c in range(0, d, LANE_GROUP)], axis=-1)


def _decay_and_rate(lw, la, w2_ref, w0_ref, a2_ref, a0_ref):
    second = lambda x, w_ref: jnp.dot(x.astype(BF16), w_ref[...].astype(BF16), preferred_element_type=F32)
    return (_act(second(lw, w2_ref) + w0_ref[...], "decay"),
            jax.nn.sigmoid(second(la, a2_ref) + a0_ref[...]))


def _scan_body(*refs, has_s0):
    it = iter(refs)
    r_ref, k_ref, v_ref, lw_ref, la_ref = (next(it) for _ in range(5))
    w2_ref, w0_ref, a2_ref, a0_ref = (next(it) for _ in range(4))
    kk_ref, ka_ref, bd_ref, eye_ref = next(it), next(it), next(it), next(it)
    s0_ref = next(it) if has_s0 else None
    y_ref, st_ref = next(it), next(it)
    d_ref, kn_ref, al_ref, km_ref, yw_ref, vk_ref, lhs_ref = (next(it) for _ in range(7))

    nb, tt, d = r_ref.shape
    ng = st_ref.shape[1]

    @pl.when(pl.program_id(1) == 0)
    def _():
        if has_s0:
            st_ref[...] = s0_ref[...]
        else:
            st_ref[...] = jnp.zeros_like(st_ref)

    bd = bd_ref[...]
    eye = eye_ref[...]
    eye16 = eye.astype(BF16)
    k = k_ref[...].reshape(nb * tt, d)
    r = r_ref[...].reshape(nb * tt, d)
    dcy, a = _decay_and_rate(lw_ref[...].reshape(nb * tt, -1), la_ref[...].reshape(nb * tt, -1),
                             w2_ref, w0_ref, a2_ref, a0_ref)
    d_ref[...] = dcy.reshape(nb, tt, d)
    kk = k * kk_ref[...]
    kn = kk * lax.rsqrt(_head_sum_rows(kk * kk, bd) + 1e-12)
    al = kn * a
    km = k * (1.0 + (a - 1.0) * ka_ref[...])
    kn_ref[...] = kn.reshape(nb, tt, d)
    al_ref[...] = al.reshape(nb, tt, d)
    km_ref[...] = km.reshape(nb, tt, d)
    yw = dcy * r - kn * _head_sum_rows(al * r, bd)
    yw_ref[...] = yw.reshape(nb, tt, d)
    vk_ref[...] = v_ref[...] * _head_sum_rows(km * r, bd).reshape(nb, tt, d)
    packed_rows = 2 * SUBLANES

    def batch_step(t, n, lhs):
        def row(ref, g):
            return ref[n, pl.ds(t, 1), pl.ds(g * LANE_GROUP, LANE_GROUP)]

        def row16(ref, g):
            one = jnp.broadcast_to(row(ref, g), (packed_rows, LANE_GROUP)).astype(BF16)
            return jnp.concatenate([one] * (RW_HEAD // packed_rows), axis=0)

        for g in range(ng):
            q = g * 3 * RW_HEAD
            s16 = st_ref[n, g].astype(BF16)
            lhs[pl.ds(q, RW_HEAD), :] = s16 * row16(kn_ref, g)
            lhs[pl.ds(q + RW_HEAD, RW_HEAD), :] = s16 * row16(yw_ref, g)
            lhs[pl.ds(q + 2 * RW_HEAD, RW_HEAD), :] = eye16 * row16(v_ref, g)
        res = jnp.dot(lhs[...], bd, preferred_element_type=F32)
        for g in range(ng):
            q = g * 3 * RW_HEAD
            skk = res[q:q + RW_HEAD]
            ysum = res[q + RW_HEAD:q + 2 * RW_HEAD]
            vcol = res[q + 2 * RW_HEAD:q + 3 * RW_HEAD]
            st_ref[n, g] = (st_ref[n, g] * row(d_ref, g) - skk * row(al_ref, g)
                            + vcol * row(km_ref, g))
            y_ref[n, pl.ds(t, 1), pl.ds(g * LANE_GROUP, LANE_GROUP)] = (
                jnp.sum(ysum * eye, axis=0, keepdims=True) + row(vk_ref, g))

    n_sub = lhs_ref.shape[0]

    def steps(i, carry):
        for u in range(n_sub):
            for n in range(nb):
                batch_step(i * n_sub + u, n, lhs_ref.at[u, n])
        return carry

    lax.fori_loop(0, tt // n_sub, steps, 0)


def _rwkv_scan(r, k, v, lw, la, w2, w0, a2, a0, kk, ka, s0, b, t):
    n, d = r.shape
    ng = d // LANE_GROUP
    bd, eye = _group_consts()
    if t > SCAN_TT:
        nb, tt = b, SCAN_TT
    else:
        nb, tt = min(SCAN_NB, b), t
    tok = pl.BlockSpec((nb, tt, d), lambda i, c: (i, c, 0))
    low = lambda x: pl.BlockSpec((nb, tt, x.shape[-1]), lambda i, c: (i, c, 0))
    const2 = lambda shape: pl.BlockSpec(shape, lambda i, c: (0, 0))
    state = pl.BlockSpec((nb, ng, RW_HEAD, LANE_GROUP), lambda i, c: (i, 0, 0, 0))
    in_specs = [tok] * 3 + [low(lw), low(la), const2(w2.shape), const2((1, d)), const2(a2.shape),
                            const2((1, d)), const2((1, d)), const2((1, d)), const2(bd.shape),
                            const2(eye.shape)]
    args = [x.reshape(b, t, x.shape[-1]) for x in (r, k, v, lw, la)]
    args += [w2, w0.reshape(1, d), a2, a0.reshape(1, d), kk.reshape(1, d), ka.reshape(1, d), bd, eye]
    if s0 is not None:
        in_specs.append(state)
        args.append(s0)
    rows = nb * ng * RW_HEAD
    y, st = pl.pallas_call(
        functools.partial(_scan_body, has_s0=s0 is not None),
        grid=(b // nb, t // tt),
        in_specs=in_specs,
        out_specs=[tok, state],
        out_shape=[jax.ShapeDtypeStruct((b, t, d), F32),
                   jax.ShapeDtypeStruct((b, ng, RW_HEAD, LANE_GROUP), F32)],
        scratch_shapes=[pltpu.VMEM((nb, tt, d), F32)] * 6
        + [pltpu.VMEM((SCAN_UNROLL, nb, 3 * ng * RW_HEAD, LANE_GROUP), BF16)],
        compiler_params=_params(("parallel", "arbitrary")),
        name="rwkv_scan",
    )(*args)
    return y.reshape(n, d), st


def _state_to_groups(s):
    b, h, nv, nk = s.shape
    s = s.reshape(b, h // HEADS_PER_GROUP, HEADS_PER_GROUP, nv, nk)
    return s.transpose(0, 1, 3, 2, 4).reshape(b, h // HEADS_PER_GROUP, nv, HEADS_PER_GROUP * nk)


def _state_from_groups(s):
    b, ng, nv, _ = s.shape
    s = s.reshape(b, ng, nv, HEADS_PER_GROUP, RW_HEAD)
    return s.transpose(0, 1, 3, 2, 4).reshape(b, ng * HEADS_PER_GROUP, nv, RW_HEAD)


def _rwkv_post_body(y_ref, r_ref, k_ref, v_ref, la_ref, lg_ref, a2_ref, a0_ref, g2_ref, lnw_ref,
                    lnb_ref, rk_ref, ka_ref, bd_ref, z_ref):
    bd = bd_ref[...]
    y = y_ref[...]
    inv = 1.0 / RW_HEAD
    mean = _head_sum_rows(y, bd) * inv
    yc = y - mean
    var = _head_sum_rows(yc * yc, bd) * inv
    yn = yc * lax.rsqrt(var + RW_LN_EPS) * lnw_ref[...] + lnb_ref[...]
    second = lambda x, w_ref: jnp.dot(x.astype(BF16), w_ref[...].astype(BF16), preferred_element_type=F32)
    a = jax.nn.sigmoid(second(la_ref[...], a2_ref) + a0_ref[...])
    gate = second(lg_ref[...], g2_ref)
    km = k_ref[...] * (1.0 + (a - 1.0) * ka_ref[...])
    bonus = _head_sum_rows(r_ref[...] * km * rk_ref[...], bd) * v_ref[...]
    z_ref[...] = ((yn + bonus) * gate).astype(BF16)


def _rwkv_post(y, r, k, v, la, lg, a2, a0, g2, lnw, lnb, rk, ka):
    n, d = y.shape
    bd, _ = _group_consts()
    rows = min(EW_ROWS, n)
    tok = pl.BlockSpec((rows, d), lambda i: (i, 0))
    low = lambda x: pl.BlockSpec((rows, x.shape[-1]), lambda i: (i, 0))
    vec = pl.BlockSpec((1, d), lambda i: (0, 0))
    whole = lambda x: pl.BlockSpec(x.shape, lambda i: (0, 0))
    return pl.pallas_call(
        _rwkv_post_body,
        grid=(n // rows,),
        in_specs=[tok] * 4 + [low(la), low(lg), whole(a2), vec, whole(g2)] + [vec] * 4 + [whole(bd)],
        out_specs=tok,
        out_shape=jax.ShapeDtypeStruct((n, d), BF16),
        compiler_params=_params(("parallel",)),
        name="rwkv_post",
    )(y, r, k, v, la, lg, a2, a0.reshape(1, d), g2, lnw.reshape(1, d), lnb.reshape(1, d),
      rk.reshape(1, d), ka.reshape(1, d), bd)


def _attn_unit(q, k_prev, v_prev, k_cur, v_cur, sink_ref, prev_limit, o_ref, row0):
    tq = q.shape[0]
    kvh = k_cur.shape[-1] // ATT_HEAD_DIM
    n_heads = kvh * GQA
    qi = lax.broadcasted_iota(jnp.int32, (tq, 1), 0)
    jp = lax.broadcasted_iota(jnp.int32, (1, WINDOW), 1)
    jc = lax.broadcasted_iota(jnp.int32, (1, tq), 1)
    diff_p = WINDOW + qi - jp
    diff_c = qi - jc
    valid_p = diff_p < prev_limit
    valid_c = diff_c >= 0
    diff_pf, diff_cf = diff_p.astype(F32), diff_c.astype(F32)
    ones_p = (lax.broadcasted_iota(jnp.int32, (WINDOW, ATT_HEAD_DIM), 1) == 0).astype(F32)
    ones_c = ones_p[:tq]
    nt = (((1,), (1,)), ((), ()))
    scale = ATT_HEAD_DIM ** -0.5
    for kh in range(kvh):
        heads = [kh * GQA + h for h in range(GQA)]
        qs = (jnp.concatenate([q[:, h * ATT_HEAD_DIM:(h + 1) * ATT_HEAD_DIM] for h in heads], axis=0)
              * scale).astype(BF16)
        sl = slice(kh * ATT_HEAD_DIM, (kh + 1) * ATT_HEAD_DIM)
        kp, kc = k_prev[:, sl].astype(BF16), k_cur[:, sl].astype(BF16)
        vp = jnp.concatenate([v_prev[:, sl], ones_p], axis=-1).astype(BF16)
        vc = jnp.concatenate([v_cur[:, sl], ones_c], axis=-1).astype(BF16)
        s_p_all = lax.dot_general(qs, kp, nt, preferred_element_type=F32)
        s_c_all = lax.dot_general(qs, kc, nt, preferred_element_type=F32)
        p_p, p_c, m_all = [], [], []
        for hq, h in enumerate(heads):
            slope = 2.0 ** (-8.0 * (h + 1) / n_heads)
            s_p = jnp.where(valid_p, s_p_all[hq * tq:(hq + 1) * tq] - slope * diff_pf, NEG_INF)
            s_c = jnp.where(valid_c, s_c_all[hq * tq:(hq + 1) * tq] - slope * diff_cf, NEG_INF)
            if tq == WINDOW:
                m = jnp.max(jnp.maximum(s_p, s_c), axis=-1, keepdims=True)
            else:
                m = jnp.maximum(jnp.max(s_p, axis=-1, keepdims=True),
                                jnp.max(s_c, axis=-1, keepdims=True))
            m = jnp.maximum(m, sink_ref[h])
            p_p.append(jnp.exp(s_p - m))
            p_c.append(jnp.exp(s_c - m))
            m_all.append(m)
        o_all = (jnp.dot(jnp.concatenate(p_p, axis=0).astype(BF16), vp, preferred_element_type=F32)
                 + jnp.dot(jnp.concatenate(p_c, axis=0).astype(BF16), vc, preferred_element_type=F32))
        for hq, h in enumerate(heads):
            o_h = o_all[hq * tq:(hq + 1) * tq]
            den = o_h[:, ATT_HEAD_DIM:ATT_HEAD_DIM + 1] + jnp.exp(sink_ref[h] - m_all[hq])
            o_ref[pl.ds(row0, tq), pl.ds(h * ATT_HEAD_DIM, ATT_HEAD_DIM)] = (
                o_h[:, :ATT_HEAD_DIM] / den).astype(o_ref.dtype)


def _attn_prefill_body(q_ref, kvc_ref, kvp_ref, sink_ref, o_ref):
    kvd = kvc_ref.shape[-1] // 2
    kvc, kvp = kvc_ref[...], kvp_ref[...]
    prev_limit = jnp.where(pl.program_id(1) > 0, WINDOW, 0)
    _attn_unit(q_ref[...], kvp[:, :kvd], kvp[:, kvd:], kvc[:, :kvd], kvc[:, kvd:], sink_ref,
               prev_limit, o_ref, 0)


def _attn_decode_body(q_ref, kvc_ref, kc_ref, vc_ref, sink_ref, o_ref, *, nb, tq):
    kvd = kvc_ref.shape[-1] // 2
    kvh = kvd // ATT_HEAD_DIM
    n_heads = kvh * GQA
    rows = n_heads * tq
    head = lax.broadcasted_iota(jnp.int32, (rows, 1), 0) // tq
    qi = lax.broadcasted_iota(jnp.int32, (rows, 1), 0) % tq
    slope = jnp.exp2(-8.0 * (head.astype(F32) + 1.0) / n_heads)
    sink = sink_ref[...]

    def key_axis(n_keys):
        c = lax.broadcasted_iota(jnp.int32, (1, kvh * n_keys), 1)
        return c // n_keys, c % n_keys

    kh_p, jp = key_axis(WINDOW)
    kh_c, jc = key_axis(tq)
    own_p = kh_p == head // GQA
    own_c = kh_c == head // GQA
    diff_p = WINDOW + qi - jp
    diff_c = qi - jc
    valid_p = jnp.where(own_p, diff_p, WINDOW) < WINDOW
    valid_c = jnp.where(own_c, diff_c, -1) >= 0
    bias_p = jnp.where(valid_p, -slope * diff_p.astype(F32), NEG_INF)
    bias_c = jnp.where(valid_c, -slope * diff_c.astype(F32), NEG_INF)
    ones = (lax.broadcasted_iota(jnp.int32, (WINDOW, ATT_HEAD_DIM), 1) == 0).astype(F32)
    nt = (((1,), (1,)), ((), ()))
    scale = ATT_HEAD_DIM ** -0.5
    head_cols = lambda x, h: x[:, h * ATT_HEAD_DIM:(h + 1) * ATT_HEAD_DIM]

    def stack_keys(x):
        return jnp.concatenate([head_cols(x, kh) for kh in range(kvh)], axis=0).astype(BF16)

    def stack_values(x):
        one = ones[:x.shape[0]]
        return jnp.concatenate(
            [jnp.concatenate([head_cols(x, kh), one], axis=-1) for kh in range(kvh)], axis=0).astype(BF16)

    for n in range(nb):
        q = q_ref[pl.ds(n * tq, tq), :]
        kvc = kvc_ref[pl.ds(n * tq, tq), :]
        qs = (jnp.concatenate([head_cols(q, h) for h in range(n_heads)], axis=0) * scale).astype(BF16)
        s_p = lax.dot_general(qs, stack_keys(kc_ref[n]), nt, preferred_element_type=F32) + bias_p
        s_c = lax.dot_general(qs, stack_keys(kvc[:, :kvd]), nt, preferred_element_type=F32) + bias_c
        m = jnp.maximum(jnp.maximum(jnp.max(s_p, axis=-1, keepdims=True),
                                    jnp.max(s_c, axis=-1, keepdims=True)), sink)
        o_all = (jnp.dot(jnp.exp(s_p - m).astype(BF16), stack_values(vc_ref[n]),
                         preferred_element_type=F32)
                 + jnp.dot(jnp.exp(s_c - m).astype(BF16), stack_values(kvc[:, kvd:]),
                           preferred_element_type=F32))
        den = o_all[:, ATT_HEAD_DIM:ATT_HEAD_DIM + 1] + jnp.exp(sink - m)
        o = o_all[:, :ATT_HEAD_DIM] / den
        for h in range(n_heads):
            o_ref[pl.ds(n * tq, tq), pl.ds(h * ATT_HEAD_DIM, ATT_HEAD_DIM)] = o[h * tq:(h + 1) * tq]


def _attn_prefill(q, kv, sinks, b, t):
    n, dq = q.shape
    kv2 = kv.shape[-1]
    nblk = t // WINDOW
    return pl.pallas_call(
        _attn_prefill_body,
        grid=(b, nblk),
        in_specs=[
            pl.BlockSpec((WINDOW, dq), lambda i, j: (i * nblk + j, 0)),
            pl.BlockSpec((WINDOW, kv2), lambda i, j: (i * nblk + j, 0)),
            pl.BlockSpec((WINDOW, kv2), lambda i, j: (i * nblk + jnp.maximum(j - 1, 0), 0)),
            pl.BlockSpec(memory_space=pltpu.SMEM),
        ],
        out_specs=pl.BlockSpec((WINDOW, dq), lambda i, j: (i * nblk + j, 0)),
        out_shape=jax.ShapeDtypeStruct((n, dq), BF16),
        compiler_params=_params(("parallel", "arbitrary")),
        name="attn_prefill",
    )(q, kv, kv, sinks.astype(F32))


def _attn_decode(q, kv, cache_k, cache_v, sinks, b, t):
    n, dq = q.shape
    kv2 = kv.shape[-1]
    kvd = kv2 // 2
    nb = min(ATT_NB, b)
    rows = nb * t
    return pl.pallas_call(
        functools.partial(_attn_decode_body, nb=nb, tq=t),
        grid=(b // nb,),
        in_specs=[
            pl.BlockSpec((rows, dq), lambda i: (i, 0)),
            pl.BlockSpec((rows, kv2), lambda i: (i, 0)),
            pl.BlockSpec((nb, WINDOW, kvd), lambda i: (i, 0, 0)),
            pl.BlockSpec((nb, WINDOW, kvd), lambda i: (i, 0, 0)),
            pl.BlockSpec((sinks.shape[0] * t, 1), lambda i: (0, 0)),
        ],
        out_specs=pl.BlockSpec((rows, dq), lambda i: (i, 0)),
        out_shape=jax.ShapeDtypeStruct((n, dq), F32),
        compiler_params=_params(("parallel",)),
        name="attn_decode",
    )(q, kv, cache_k.reshape(b, WINDOW, kvd), cache_v.reshape(b, WINDOW, kvd),
      jnp.repeat(sinks.astype(F32), t)[:, None])


def _moe_body(te_ref, tv_ref, tok_ref, dst_ref, h_hbm, gate_ref, wg_ref, wu_ref, wd_ref, out_hbm,
              xbuf, obuf, gsem, ssem, wg16, wu16, wd16):
    t = pl.program_id(0)
    last = pl.num_programs(0) - 1
    slot = t % 2

    def split(row):
        return lax.shift_right_logical(row, 3), row & (SUBLANES - 1)

    def gather_of(buf):
        def copy(i, u, tok):
            hi, lo = split(tok)
            return pltpu.make_async_copy(h_hbm.at[hi, pl.ds(lo, 1)], xbuf.at[buf, i, pl.ds(u, 1)],
                                         gsem.at[buf])
        return tok_ref, copy

    def scatter_copy(i, u, dst):
        hi, lo = split(dst)
        return pltpu.make_async_copy(obuf.at[i, pl.ds(u, 1)], out_hbm.at[hi, pl.ds(lo, 1)], ssem.at[0])

    scatter = (dst_ref, scatter_copy)

    def for_rows(tile, rows, act):
        table, build = rows
        groups = lax.shift_right_logical(tv_ref[tile] + (SUBLANES - 1), 3)

        def body(i, carry):
            base = tile * MOE_TG + i * SUBLANES
            entries = [table[base + u] for u in range(SUBLANES)]
            for u in range(SUBLANES):
                act(build(i, u, entries[u]))
            return carry
        lax.fori_loop(0, groups, body, 0)

    start = lambda copy: copy.start()
    wait = lambda copy: copy.wait()

    groups_per_tile = MOE_TG // SUBLANES

    @pl.when(t == 0)
    def _():
        xbuf[...] = jnp.zeros_like(xbuf)
        obuf[...] = jnp.zeros_like(obuf)
        spare = pltpu.make_async_copy(
            obuf, out_hbm.at[pl.ds(out_hbm.shape[0] - groups_per_tile, groups_per_tile)], ssem.at[0])
        spare.start()
        spare.wait()
        for_rows(0, gather_of(0), start)

    nxt = jnp.minimum(t + 1, last)

    @pl.when(t < last)
    def _():
        for_rows(nxt, gather_of(1 - slot), start)

    prev = jnp.maximum(t - 1, 0)

    @pl.when((t == 0) | (te_ref[t] != te_ref[prev]))
    def _():
        wg16[...] = wg_ref[...].astype(BF16)
        wu16[...] = wu_ref[...].astype(BF16)
        wd16[...] = wd_ref[...].astype(BF16)

    @pl.when(tv_ref[t] > 0)
    def _():
        for_rows(t, gather_of(slot), wait)
        x = xbuf[slot].reshape(MOE_TG, -1).astype(BF16)
        hg = jnp.dot(x, wg16[...], preferred_element_type=F32)
        hu = jnp.dot(x, wu16[...], preferred_element_type=F32)
        act = (hg * jax.nn.sigmoid(hg)) * hu * gate_ref[...]
        o = jnp.dot(act.astype(BF16), wd16[...], preferred_element_type=F32)

        @pl.when(t > 0)
        def _():
            for_rows(prev, scatter, wait)

        obuf[...] = o.reshape(obuf.shape)
        for_rows(t, scatter, start)

        @pl.when(t == last)
        def _():
            for_rows(t, scatter, wait)

    @pl.when((tv_ref[t] == 0) & (t > 0))
    def _():
        for_rows(prev, scatter, wait)


def _moe_plan(route, n_tiles):
    n = route.shape[0]
    e = route[:, :2].astype(jnp.int32).reshape(-1)
    w = route[:, 2:4].reshape(-1)
    onehot = (e[:, None] == jnp.arange(N_EXPERTS, dtype=jnp.int32)[None, :]).astype(jnp.int32)
    csum = jnp.cumsum(onehot, axis=0)
    rank = jnp.take_along_axis(csum, e[:, None], axis=1)[:, 0] - 1
    counts = csum[-1]
    padded = ((counts + MOE_TG - 1) // MOE_TG) * MOE_TG
    pend = jnp.cumsum(padded)
    dest = (pend - padded)[e] + rank
    p_rows = n_tiles * MOE_TG
    vals = jnp.stack([jnp.arange(1, 2 * n + 1, dtype=jnp.int32), lax.bitcast_convert_type(w, jnp.int32)],
                     axis=1)
    placed = jnp.zeros((p_rows, 2), jnp.int32).at[dest].set(vals, unique_indices=True,
                                                             mode="promise_in_bounds")
    has_pair = placed[:, 0] > 0
    pair = placed[:, 0] - 1
    row_token = jnp.where(has_pair, pair // 2, 0)
    spare = 2 * n + jnp.arange(p_rows, dtype=jnp.int32) % MOE_TG
    row_dst = jnp.where(has_pair, (pair % 2) * n + pair // 2, spare)
    row_gate = jnp.where(has_pair, lax.bitcast_convert_type(placed[:, 1], F32), 0.0)
    starts = jnp.arange(n_tiles, dtype=jnp.int32) * MOE_TG
    tile_expert = jnp.minimum(jnp.searchsorted(pend, starts, side="right"), N_EXPERTS - 1)
    tile_rows = jnp.clip(counts[tile_expert] - (starts - (pend - padded)[tile_expert]), 0, MOE_TG)
    tile_rows = jnp.where(starts < pend[-1], tile_rows, 0).astype(jnp.int32)
    last_used = jnp.max(jnp.where(tile_rows > 0, tile_expert, 0))
    tile_expert = jnp.where(tile_rows > 0, tile_expert, last_used).astype(jnp.int32)
    return row_token, row_dst, row_gate, tile_expert, tile_rows


def _moe(h, route, w_gate, w_up, w_down, layer):
    n, d = h.shape
    group = (SUBLANES, d)
    f = w_gate.shape[-1]
    n_tiles = (2 * n) // MOE_TG + N_EXPERTS
    n_out = 2 * n + MOE_TG
    assert n % SUBLANES == 0
    row_token, row_dst, row_gate, tile_expert, tile_rows = _moe_plan(route, n_tiles)
    out = pl.pallas_call(
        _moe_body,
        grid_spec=pltpu.PrefetchScalarGridSpec(
            num_scalar_prefetch=4,
            grid=(n_tiles,),
            in_specs=[
                pl.BlockSpec(memory_space=pl.ANY),
                pl.BlockSpec((MOE_TG, 1), lambda t, te, tv, rt, rd: (t, 0)),
                pl.BlockSpec((None, None, d, f), lambda t, te, tv, rt, rd: (layer, te[t], 0, 0)),
                pl.BlockSpec((None, None, d, f), lambda t, te, tv, rt, rd: (layer, te[t], 0, 0)),
                pl.BlockSpec((None, None, f, d), lambda t, te, tv, rt, rd: (layer, te[t], 0, 0)),
            ],
            out_specs=pl.BlockSpec(memory_space=pl.ANY),
            scratch_shapes=[pltpu.VMEM((2, MOE_TG // SUBLANES) + group, F32),
                            pltpu.VMEM((MOE_TG // SUBLANES,) + group, F32),
                            pltpu.SemaphoreType.DMA((2,)), pltpu.SemaphoreType.DMA((1,)),
                            pltpu.VMEM((d, f), BF16), pltpu.VMEM((d, f), BF16),
                            pltpu.VMEM((f, d), BF16)],
        ),
        out_shape=jax.ShapeDtypeStruct((n_out // SUBLANES,) + group, F32),
        compiler_params=_params(("arbitrary",)),
        name="moe_ffn",
    )(tile_expert, tile_rows, row_token, row_dst, h.reshape((n // SUBLANES,) + group),
      row_gate.reshape(-1, 1), w_gate, w_up, w_down)
    return out.reshape(n_out, d)


def _split_mod(mod, n_parts, trunk_rows):
    mod = mod[:, None, :]
    return [[(mod, k, row0) for k in range(n_parts)] for row0 in trunk_rows]


def kernel(x_prompt, x_sample, c_prompt, c_sample, state_wkv, state_shift, cache_k_win, cache_v_win, ada_w, ada_b, ln_g, ln_b, rw_mu, rw_w_rkv, rw_w0, rw_w1, rw_w2, rw_a0, rw_a1, rw_a2, rw_g1, rw_g2, rw_k_k, rw_k_a, rw_r_k, rw_lnx_w, rw_lnx_b, rw_wo, ada_kv_w, ada_kv_b, w_kv, w_q, attn_sinks, w_o_attn, router_w, router_bias, moe_w_gate, moe_w_up, moe_w_down):
    d = x_prompt.shape[-1]
    xs = [x_prompt, x_sample]
    shapes = [x.shape[:2] for x in xs]
    counts = [b * t for b, t in shapes]
    bp = shapes[0][0]

    c_all = jnp.concatenate([c_sample, c_prompt], axis=0)
    trunk_rows = [shapes[1][0], 0]
    mods = {(l, s): _split_mod(_mm(c_all, ada_w, name="mm_ada", w_lead=(l, s), bias=ada_b[l, s],
                                   in_act="silu"), 3, trunk_rows)
            for l in range(DEPTH) for s in range(2)}
    mods_kv = _split_mod(_mm(c_all, ada_kv_w, name="mm_ada_kv", bias=ada_kv_b, in_act="silu"), 2,
                         trunk_rows)

    by_slot = lambda x: x.reshape(-1, N_GROUPS, EXPERTS_PER_GROUP).swapaxes(1, 2).reshape(-1, N_EXPERTS)
    router = (by_slot(router_w), by_slot(router_bias))

    n_all = counts[0] + counts[1]
    row0 = [0, counts[0]]

    def moe_layer(h_all, routes, layer):
        out = _moe(h_all, jnp.concatenate(routes, axis=0), moe_w_gate, moe_w_up, moe_w_down, layer)
        return [[(out, slot * n_all + row0[tr]) for slot in range(2)] for tr in range(2)]

    firsts = [jnp.zeros((bp, 1, d), F32), state_shift[0][:, None, :]]
    s0s = [None, _state_to_groups(state_wkv[0])]
    x1, h1, route1, states, hlasts = [], None, [], [], []
    for tr in range(2):
        b, t = shapes[tr]
        shift, scale, gate = mods[0, 0][tr]
        xmix, hlast, lw, la, lg = _premix(xs[tr], shift, scale, firsts[tr], rw_mu[0], rw_w1[0],
                                          rw_a1[0], rw_g1[0])
        r = _mm(xmix, rw_w_rkv, name="mm_r", x_lead=(0,), w_lead=(0, 0), tm=MM_TM_BF16)
        k = _mm(xmix, rw_w_rkv, name="mm_k", x_lead=(1,), w_lead=(0, 1), tm=MM_TM_BF16)
        v = _mm(xmix, rw_w_rkv, name="mm_v", x_lead=(2,), w_lead=(0, 2), tm=MM_TM_BF16)
        y, st = _rwkv_scan(r, k, v, lw, la, rw_w2[0], rw_w0[0], rw_a2[0], rw_a0[0], rw_k_k[0],
                           rw_k_a[0], s0s[tr], b, t)
        z = _rwkv_post(y, r, k, v, la, lg, rw_a2[0], rw_a0[0], rw_g2[0], rw_lnx_w[0], rw_lnx_b[0],
                       rw_r_k[0], rw_k_a[0])
        out = _mm(z, rw_wo, name="mm_wo", w_lead=(0,), tm=MM_TM_BF16)
        sh, sc, _ = mods[0, 1][tr]
        xn, h1, route = _ln_mod(xs[tr], [(out, 0)], gate, ln_g[0, 0], ln_b[0, 0], [(sh, sc)], router,
                                (h1, n_all, row0[tr]))
        x1.append(xn), route1.append(route), states.append(st), hlasts.append(hlast)
    moe1 = moe_layer(h1, route1, 0)

    caches = [None, (cache_k_win, cache_v_win)]
    x3, h3, route3, kvs = [], None, [], []
    for tr in range(2):
        b, t = shapes[tr]
        shq, scq, gate_q = mods[1, 0][tr]
        shk, sck = mods_kv[tr]
        x2, hq, hkv = _ln_mod(x1[tr], moe1[tr], mods[0, 1][tr][2], ln_g[0, 1], ln_b[0, 1],
                              [(shq, scq), (shk, sck)])
        q = _mm(hq, w_q, name="mm_q", w_lead=(0,), tm=MM_TM_BF16)
        kv = _mm(hkv, w_kv, name="mm_kv", tm=MM_TM_BF16)
        if caches[tr] is None:
            o = _attn_prefill(q, kv, attn_sinks[0], b, t)
        else:
            o = _attn_decode(q, kv, caches[tr][0], caches[tr][1], attn_sinks[0], b, t)
        out = _mm(o, w_o_attn, name="mm_o", w_lead=(0,), tm=MM_TM_BF16 if o.dtype == BF16 else MM_TM)
        sh, sc, _ = mods[1, 1][tr]
        xn, h3, route = _ln_mod(x2, [(out, 0)], gate_q, ln_g[1, 0], ln_b[1, 0], [(sh, sc)], router,
                                (h3, n_all, row0[tr]))
        x3.append(xn), route3.append(route), kvs.append(kv)
    moe3 = moe_layer(h3, route3, 1)
    ys = [_ln_mod(x3[tr], moe3[tr], mods[1, 1][tr][2], ln_g[1, 1], ln_b[1, 1], [])[0]
          for tr in range(2)]

    kvd = kvs[0].shape[-1] // 2
    kvh = kvd // ATT_HEAD_DIM
    (bp, tp), (bs, ts) = shapes
    kv_p = kvs[0].reshape(bp, tp, 2 * kvd)[:, tp - WINDOW:]
    p_k_win = kv_p[..., :kvd].reshape(bp, WINDOW, kvh, ATT_HEAD_DIM)
    p_v_win = kv_p[..., kvd:].reshape(bp, WINDOW, kvh, ATT_HEAD_DIM)
    kv_s = kvs[1].reshape(bs, ts, 2 * kvd)
    s_k_win = jnp.concatenate(
        [cache_k_win, kv_s[..., :kvd].reshape(bs, ts, kvh, ATT_HEAD_DIM)], axis=1)[:, -WINDOW:]
    s_v_win = jnp.concatenate(
        [cache_v_win, kv_s[..., kvd:].reshape(bs, ts, kvh, ATT_HEAD_DIM)], axis=1)[:, -WINDOW:]
    p_wkv = _state_from_groups(states[0])[None]
    s_wkv = _state_from_groups(states[1])[None]
    p_shift = hlasts[0].reshape(1, bp, d)
    s_shift = hlasts[1].reshape(1, bs, d)
    return (ys[0], ys[1], p_wkv, p_shift, p_k_win, p_v_win, s_wkv, s_shift, s_k_win, s_v_win)
```

```python
import functools

import jax
import jax.numpy as jnp
import numpy as np
from jax import lax
from jax.experimental import pallas as pl
from jax.experimental.pallas import tpu as pltpu

F32 = jnp.float32
BF16 = jnp.bfloat16

RW_HEAD = 64
ATT_HEAD_DIM = 64
GQA = 8
WINDOW = 128
N_EXPERTS = 32
EXPERTS_PER_GROUP = 4
N_GROUPS = N_EXPERTS // EXPERTS_PER_GROUP
DEPTH = 2
ALPHA = (2.0 * DEPTH) ** 0.25
LN_EPS = 1e-5
RW_LN_EPS = 64e-5
NEG_INF = -1e30
DECAY_SCALE = float(np.exp(-0.5))
MIX_RKV = (0, 2, 3)
MIX_W, MIX_A, MIX_G = 1, 4, 5

SUBLANES = 8
LANE_GROUP = 256
HEADS_PER_GROUP = LANE_GROUP // RW_HEAD
VMEM_LIMIT = 56 * 1024 * 1024

MM_TM = 1024
MM_TM_BF16 = 2048
MM_TN = 512
EW_ROWS = 256
SCAN_TT = 64
SCAN_NB = 4
SCAN_UNROLL = 8
ATT_NB = 8
MOE_TG = 256


def _params(sem):
    return pltpu.CompilerParams(dimension_semantics=sem, vmem_limit_bytes=VMEM_LIMIT)


def _act(x, kind):
    if kind is None:
        return x
    if kind == "silu":
        return x * jax.nn.sigmoid(x)
    if kind == "tanh":
        return jnp.tanh(x)
    if kind == "sigmoid":
        return jax.nn.sigmoid(x)
    if kind == "decay":
        return jnp.exp(-DECAY_SCALE * jax.nn.sigmoid(x))
    raise ValueError(kind)


def _mm_body(*refs, in_act, out_act, has_bias, stage_x):
    x_ref, w_ref = refs[0], refs[1]
    b_ref = refs[2] if has_bias else None
    o_ref = refs[2 + has_bias]
    if stage_x:
        xs_ref = refs[3 + has_bias]

        @pl.when(pl.program_id(1) == 0)
        def _():
            xs_ref[...] = _act(x_ref[...].astype(F32), in_act).astype(BF16)
    else:
        xs_ref = x_ref

    acc = jnp.dot(xs_ref[...], w_ref[...].astype(BF16), preferred_element_type=F32)
    if has_bias:
        acc = acc + b_ref[...]
    o_ref[...] = _act(acc, out_act).astype(o_ref.dtype)


def _mm(x, w, *, name, x_lead=(), w_lead=(), bias=None, in_act=None, out_act=None, out_dtype=F32,
        tm=MM_TM, tn=MM_TN):
    m, k = x.shape[-2:]
    n = w.shape[-1]
    tm = min(tm, m)
    tn = min(tn, n)
    assert m % tm == 0 and n % tn == 0, (m, tm, n, tn)
    nx, nw = len(x_lead), len(w_lead)
    in_specs = [
        pl.BlockSpec((None,) * nx + (tm, k), lambda i, j: tuple(x_lead) + (i, 0)),
        pl.BlockSpec((None,) * nw + (k, tn), lambda i, j: tuple(w_lead) + (0, j)),
    ]
    args = [x, w]
    if bias is not None:
        in_specs.append(pl.BlockSpec((1, tn), lambda i, j: (0, j)))
        args.append(bias.reshape(1, n).astype(F32))
    stage_x = x.dtype != BF16 or in_act is not None
    body = functools.partial(_mm_body, in_act=in_act, out_act=out_act, has_bias=bias is not None,
                             stage_x=stage_x)
    return pl.pallas_call(
        body,
        grid=(m // tm, n // tn),
        in_specs=in_specs,
        out_specs=pl.BlockSpec((tm, tn), lambda i, j: (i, j)),
        out_shape=jax.ShapeDtypeStruct((m, n), out_dtype),
        scratch_shapes=[pltpu.VMEM((tm, k), BF16)] if stage_x else [],
        compiler_params=_params(("parallel", "arbitrary")),
        name=name,
    )(*args)


def _geo(b, t):
    if t == SUBLANES:
        g = min(EW_ROWS // SUBLANES, b)
        return dict(groups=g, steps=b // g, mod_groups=g, steps_per_batch=1, mod_index=lambda i: i)
    assert t % EW_ROWS == 0
    spb = t // EW_ROWS
    return dict(groups=EW_ROWS // SUBLANES, steps=b * spb, mod_groups=1, steps_per_batch=spb,
                mod_index=lambda i: i // spb)


def _mod_spec(mod, mg, d, block_of_step):
    _, part, row0 = mod
    assert row0 % mg == 0, (row0, mg)
    return pl.BlockSpec((mg, 1, d), lambda i: (row0 // mg + block_of_step(i), 0, part))


def _premix_body(x_ref, xprev_ref, first_ref, shift_ref, scale_ref, mu_ref, w1_ref, a1_ref, g1_ref,
                 xmix_ref, hlast_ref, lw_ref, la_ref, lg_ref, *, steps_per_batch):
    i = pl.program_id(0)
    g, s, d = x_ref.shape
    gf = first_ref.shape[0]
    scale = scale_ref[...]
    shift = shift_ref[...]
    h3 = x_ref[...] * (1.0 + scale) + shift
    hlast_ref[...] = h3[g - gf:, s - 1:, :]
    h = h3.reshape(g * s, d)
    if steps_per_batch > 1:
        hprev_row = xprev_ref[:, s - 1:, :] * (1.0 + scale) + shift
        first = jnp.where(i % steps_per_batch == 0, first_ref[...], hprev_row)
        period = g * s
    else:
        first = first_ref[...]
        period = s
    first2 = jnp.broadcast_to(first, (g, s, d)).reshape(g * s, d)
    row = lax.broadcasted_iota(jnp.int32, (g * s, 1), 0)
    hp = jnp.where(row % period == 0, first2, pltpu.roll(h, 1, 0))
    xx = hp - h
    mix = lambda m: (h + xx * mu_ref[m]).astype(BF16)
    for slot, m in enumerate(MIX_RKV):
        xmix_ref[slot] = mix(m)
    lora = lambda m, w_ref: jnp.dot(mix(m), w_ref[...].astype(BF16), preferred_element_type=F32)
    lw_ref[...] = jnp.tanh(lora(MIX_W, w1_ref))
    la_ref[...] = lora(MIX_A, a1_ref)
    lg_ref[...] = jax.nn.sigmoid(lora(MIX_G, g1_ref))


def _premix(x, shift, scale, first, mu, w1, a1, g1):
    b, t, d = x.shape
    n = b * t
    n_mix = mu.shape[0]
    c = _geo(b, t)
    g, mg, midx = c["groups"], c["mod_groups"], c["mod_index"]
    rows = g * SUBLANES
    xg = x.reshape(-1, SUBLANES, d)
    per_batch = pl.BlockSpec((mg, 1, d), lambda i: (midx(i), 0, 0))
    mod_spec = lambda mod: _mod_spec(mod, mg, d, midx)
    whole = lambda a: pl.BlockSpec(a.shape, lambda i: (0,) * a.ndim)
    lora_out = lambda w: (pl.BlockSpec((rows, w.shape[-1]), lambda i: (i, 0)),
                          jax.ShapeDtypeStruct((n, w.shape[-1]), F32))
    lora_specs, lora_shapes = zip(lora_out(w1), lora_out(a1), lora_out(g1))
    return pl.pallas_call(
        functools.partial(_premix_body, steps_per_batch=c["steps_per_batch"]),
        grid=(c["steps"],),
        in_specs=[
            pl.BlockSpec((g, SUBLANES, d), lambda i: (i, 0, 0)),
            pl.BlockSpec((1, SUBLANES, d), lambda i: (jnp.maximum(i * g - 1, 0), 0, 0)),
            per_batch, mod_spec(shift), mod_spec(scale),
            pl.BlockSpec((n_mix, 1, d), lambda i: (0, 0, 0)),
            whole(w1), whole(a1), whole(g1),
        ],
        out_specs=[pl.BlockSpec((len(MIX_RKV), rows, d), lambda i: (0, i, 0)), per_batch,
                   *lora_specs],
        out_shape=[jax.ShapeDtypeStruct((len(MIX_RKV), n, d), BF16),
                   jax.ShapeDtypeStruct((b, 1, d), F32), *lora_shapes],
        compiler_params=_params(("arbitrary",)),
        name="premix",
    )(xg, xg, first, shift[0], scale[0], mu.reshape(n_mix, 1, d), w1, a1, g1)


def _split_bf16(x):
    hi = x.astype(BF16)
    return hi, (x - hi.astype(F32)).astype(BF16)


def _route(h, rw_ref, rb_ref):
    t = h.shape[0]
    h_hi, h_lo = _split_bf16(h)
    w_hi, w_lo = _split_bf16(rw_ref[...])
    logits = (jnp.dot(h_hi, w_hi, preferred_element_type=F32)
              + jnp.dot(h_lo, w_hi, preferred_element_type=F32)
              + jnp.dot(h_hi, w_lo, preferred_element_type=F32))
    s_all = jax.nn.sigmoid(logits)
    sb_all = s_all + rb_ref[...]
    s = [s_all[:, e * N_GROUPS:(e + 1) * N_GROUPS] for e in range(EXPERTS_PER_GROUP)]
    sb = [sb_all[:, e * N_GROUPS:(e + 1) * N_GROUPS] for e in range(EXPERTS_PER_GROUP)]
    hi01, lo01 = jnp.maximum(sb[0], sb[1]), jnp.minimum(sb[0], sb[1])
    hi23, lo23 = jnp.maximum(sb[2], sb[3]), jnp.minimum(sb[2], sb[3])
    top1 = jnp.maximum(hi01, hi23)
    top2 = jnp.maximum(jnp.minimum(hi01, hi23), jnp.maximum(lo01, lo23))
    gscore = top1 + top2
    lane = lax.broadcasted_iota(jnp.int32, (t, N_GROUPS), 1)
    gmax = jnp.max(gscore, axis=-1, keepdims=True)
    gsel = jnp.min(jnp.where(gscore == gmax, lane, N_GROUPS), axis=-1, keepdims=True)
    in_group = lane == gsel
    masked = [jnp.where(in_group, v, NEG_INF) for v in sb]
    idx = [lane * EXPERTS_PER_GROUP + e for e in range(EXPERTS_PER_GROUP)]

    def pick(vals):
        best = functools.reduce(jnp.maximum, [jnp.max(v, axis=-1, keepdims=True) for v in vals])
        return functools.reduce(jnp.minimum, [
            jnp.min(jnp.where(v == best, ix, N_EXPERTS), axis=-1, keepdims=True)
            for v, ix in zip(vals, idx)])

    def weight(sel):
        return functools.reduce(jnp.add, [
            jnp.sum(jnp.where(ix == sel, v, 0.0), axis=-1, keepdims=True) for v, ix in zip(s, idx)])

    e1 = pick(masked)
    e2 = pick([jnp.where(ix == e1, -jnp.inf, v) for v, ix in zip(masked, idx)])
    w1, w2 = weight(e1), weight(e2)
    wsum = w1 + w2
    return jnp.where(lane == 0, e1.astype(F32),
                     jnp.where(lane == 1, e2.astype(F32),
                               jnp.where(lane == 2, w1 / wsum, jnp.where(lane == 3, w2 / wsum, 0.0))))


def _ln_mod_body(*refs, n_planes, n_mods, with_route, steps):
    if with_route:
        @pl.when(pl.program_id(0) < steps)
        def _():
            _ln_mod_compute(*refs, n_planes=n_planes, n_mods=n_mods, with_route=True)

        @pl.when(pl.program_id(0) >= steps)
        def _():
            h_ref = refs[-2]
            h_ref[...] = jnp.zeros_like(h_ref)
    else:
        _ln_mod_compute(*refs, n_planes=n_planes, n_mods=n_mods, with_route=False)


def _ln_mod_compute(*refs, n_planes, n_mods, with_route):
    it = iter(refs)
    x_ref = next(it)
    o_refs = [next(it) for _ in range(n_planes)]
    gate_ref, lng_ref, lnb_ref = next(it), next(it), next(it)
    mod_refs = [(next(it), next(it)) for _ in range(n_mods)]
    rw_ref = next(it) if with_route else None
    rb_ref = next(it) if with_route else None
    if with_route:
        next(it)
    xnew_ref = next(it)
    h_refs = [next(it) for _ in range(n_mods)]
    route_ref = next(it) if with_route else None

    g, s, d = x_ref.shape

    out = o_refs[0][...]
    for o_ref in o_refs[1:]:
        out = out + o_ref[...]
    y = ALPHA * x_ref[...] + gate_ref[...] * out
    mu = jnp.mean(y, axis=-1, keepdims=True)
    yc = y - mu
    var = jnp.mean(yc * yc, axis=-1, keepdims=True)
    xn = yc * lax.rsqrt(var + LN_EPS) * lng_ref[...] + lnb_ref[...]
    xnew_ref[...] = xn
    for k, ((shift_ref, scale_ref), h_ref) in enumerate(zip(mod_refs, h_refs)):
        h = (xn * (1.0 + scale_ref[...]) + shift_ref[...]).reshape(g * s, d)
        h_ref[...] = h.astype(h_ref.dtype)
        if with_route:
            route_ref[...] = _route(h, rw_ref, rb_ref)


def _ln_mod(x, planes, gate, ln_g, ln_b, mods, router=None, moe_rows=None):
    b, t, d = x.shape
    n = b * t
    c = _geo(b, t)
    g, mg, midx = c["groups"], c["mod_groups"], c["mod_index"]
    rows = g * SUBLANES
    with_route = router is not None
    steps = c["steps"]
    tail_steps = 0
    if with_route and moe_rows[0] is None:
        assert moe_rows[2] == 0 and (moe_rows[1] - n) % rows == 0
        tail_steps = (moe_rows[1] - n) // rows
    cl = lambda i: jnp.minimum(i, steps - 1)
    tok = pl.BlockSpec((g, SUBLANES, d), lambda i: (cl(i), 0, 0))
    mod_spec = lambda mod: _mod_spec(mod, mg, d, lambda i: midx(cl(i)))
    const = pl.BlockSpec((1, 1, d), lambda i: (0, 0, 0))
    in_specs = [tok]
    args = [x.reshape(-1, SUBLANES, d)]
    for arr, row0 in planes:
        assert row0 % rows == 0, (row0, rows)
        in_specs.append(pl.BlockSpec((g, SUBLANES, d), lambda i, off=row0 // rows: (off + cl(i), 0, 0)))
        args.append(arr.reshape(-1, SUBLANES, d))
    in_specs += [mod_spec(gate), const, const]
    args += [gate[0], ln_g.reshape(1, 1, d), ln_b.reshape(1, 1, d)]
    for shift, scale in mods:
        in_specs += [mod_spec(shift), mod_spec(scale)]
        args += [shift[0], scale[0]]
    out_shape = [jax.ShapeDtypeStruct((n // SUBLANES, SUBLANES, d), F32)]
    out_specs = [tok]
    aliases = {}
    if with_route:
        in_specs += [pl.BlockSpec(a.shape, lambda i: (0, 0)) for a in router]
        args += list(router)
        buf, total, row0 = moe_rows
        if buf is None:
            in_specs.append(pl.BlockSpec((1, 128), lambda i: (0, 0)))
            args.append(jnp.zeros((1, 128), F32))
        else:
            in_specs.append(pl.BlockSpec(memory_space=pl.ANY))
            args.append(buf)
            aliases = {len(args) - 1: 1}
        out_shape.append(jax.ShapeDtypeStruct((total, d), F32))
        out_specs.append(pl.BlockSpec((rows, d), lambda i, off=row0 // rows: (off + i, 0)))
        out_shape.append(jax.ShapeDtypeStruct((n, N_GROUPS), F32))
        out_specs.append(pl.BlockSpec((rows, N_GROUPS), lambda i: (cl(i), 0)))
    else:
        for _ in mods:
            out_shape.append(jax.ShapeDtypeStruct((n, d), BF16))
            out_specs.append(pl.BlockSpec((rows, d), lambda i: (i, 0)))
    outs = pl.pallas_call(
        functools.partial(_ln_mod_body, n_planes=len(planes), n_mods=len(mods), with_route=with_route,
                          steps=steps),
        grid=(steps + tail_steps,),
        in_specs=in_specs,
        out_specs=out_specs,
        out_shape=out_shape,
        input_output_aliases=aliases,
        compiler_params=_params(("arbitrary",)),
        name="ln_mod",
    )(*args)
    return [outs[0].reshape(b, t, d)] + list(outs[1:])


def _group_consts():
    lane = np.arange(LANE_GROUP)
    bd = (lane[:, None] // RW_HEAD == lane[None, :] // RW_HEAD).astype(np.float32)
    eye = (np.arange(RW_HEAD)[:, None] == lane[None, :] % RW_HEAD).astype(np.float32)
    return jnp.asarray(bd, BF16), jnp.asarray(eye, F32)


def _head_sum(x, bd):
    hi = x.astype(BF16)
    lo = (x - hi.astype(F32)).astype(BF16)
    return jnp.dot(hi, bd, preferred_element_type=F32) + jnp.dot(lo, bd, preferred_element_type=F32)


def _head_sum_rows(x, bd):
    d = x.shape[-1]
    return jnp.concatenate(
        [_head_sum(x[:, c:c + LANE_GROUP], bd) for c in range(0, d, LANE_GROUP)], axis=-1)


def _decay_and_rate(lw, la, w2_ref, w0_ref, a2_ref, a0_ref):
    second = lambda x, w_ref: jnp.dot(x.astype(BF16), w_ref[...].astype(BF16), preferred_element_type=F32)
    return (_act(second(lw, w2_ref) + w0_ref[...], "decay"),
            jax.nn.sigmoid(second(la, a2_ref) + a0_ref[...]))


def _scan_body(*refs, has_s0):
    it = iter(refs)
    r_ref, k_ref, v_ref, lw_ref, la_ref = (next(it) for _ in range(5))
    w2_ref, w0_ref, a2_ref, a0_ref = (next(it) for _ in range(4))
    kk_ref, ka_ref, bd_ref, eye_ref = next(it), next(it), next(it), next(it)
    s0_ref = next(it) if has_s0 else None
    y_ref, s_out_ref = next(it), next(it)
    st_ref, d_ref, kn_ref, al_ref, km_ref, yw_ref, vk_ref, lhs_ref = (next(it) for _ in range(8))

    nb, tt, d = r_ref.shape
    ng = st_ref.shape[1]

    def head_lanes(m):
        return pl.ds(m * RW_HEAD, RW_HEAD)

    @pl.when(pl.program_id(1) == 0)
    def _():
        if has_s0:
            for n in range(nb):
                for g in range(ng):
                    for m in range(HEADS_PER_GROUP):
                        st_ref[n, g, :, head_lanes(m)] = s0_ref[n, g * HEADS_PER_GROUP + m]
        else:
            st_ref[...] = jnp.zeros_like(st_ref)

    bd = bd_ref[...]
    eye = eye_ref[...]
    eye16 = eye.astype(BF16)
    k = k_ref[...].reshape(nb * tt, d)
    r = r_ref[...].reshape(nb * tt, d)
    dcy, a = _decay_and_rate(lw_ref[...].reshape(nb * tt, -1), la_ref[...].reshape(nb * tt, -1),
                             w2_ref, w0_ref, a2_ref, a0_ref)
    d_ref[...] = dcy.reshape(nb, tt, d)
    kk = k * kk_ref[...]
    kn = kk * lax.rsqrt(_head_sum_rows(kk * kk, bd) + 1e-12)
    al = kn * a
    km = k * (1.0 + (a - 1.0) * ka_ref[...])
    kn_ref[...] = kn.reshape(nb, tt, d)
    al_ref[...] = al.reshape(nb, tt, d)
    km_ref[...] = km.reshape(nb, tt, d)
    yw = dcy * r - kn * _head_sum_rows(al * r, bd)
    yw_ref[...] = yw.reshape(nb, tt, d)
    vk_ref[...] = v_ref[...] * _head_sum_rows(km * r, bd).reshape(nb, tt, d)
    packed_rows = 2 * SUBLANES

    def batch_step(t, n, lhs):
        def row(ref, g):
            return ref[n, pl.ds(t, 1), pl.ds(g * LANE_GROUP, LANE_GROUP)]

        def row16(ref, g):
            one = jnp.broadcast_to(row(ref, g), (packed_rows, LANE_GROUP)).astype(BF16)
            return jnp.concatenate([one] * (RW_HEAD // packed_rows), axis=0)

        for g in range(ng):
            q = g * 3 * RW_HEAD
            s16 = st_ref[n, g].astype(BF16)
            lhs[pl.ds(q, RW_HEAD), :] = s16 * row16(kn_ref, g)
            lhs[pl.ds(q + RW_HEAD, RW_HEAD), :] = s16 * row16(yw_ref, g)
            lhs[pl.ds(q + 2 * RW_HEAD, RW_HEAD), :] = eye16 * row16(v_ref, g)
        res = jnp.dot(lhs[...], bd, preferred_element_type=F32)
        for g in range(ng):
            q = g * 3 * RW_HEAD
            skk = res[q:q + RW_HEAD]
            ysum = res[q + RW_HEAD:q + 2 * RW_HEAD]
            vcol = res[q + 2 * RW_HEAD:q + 3 * RW_HEAD]
            st_ref[n, g] = (st_ref[n, g] * row(d_ref, g) - skk * row(al_ref, g)
                            + vcol * row(km_ref, g))
            y_ref[n, pl.ds(t, 1), pl.ds(g * LANE_GROUP, LANE_GROUP)] = (
                jnp.sum(ysum * eye, axis=0, keepdims=True) + row(vk_ref, g))

    n_sub = lhs_ref.shape[0]

    def steps(i, carry):
        for u in range(n_sub):
            for n in range(nb):
                batch_step(i * n_sub + u, n, lhs_ref.at[u, n])
        return carry

    lax.fori_loop(0, tt // n_sub, steps, 0)

    @pl.when(pl.program_id(1) == pl.num_programs(1) - 1)
    def _():
        for n in range(nb):
            for g in range(ng):
                for m in range(HEADS_PER_GROUP):
                    s_out_ref[n, g * HEADS_PER_GROUP + m] = st_ref[n, g, :, head_lanes(m)]


def _rwkv_scan(r, k, v, lw, la, w2, w0, a2, a0, kk, ka, s0, b, t):
    n, d = r.shape
    ng = d // LANE_GROUP
    bd, eye = _group_consts()
    if t > SCAN_TT:
        nb, tt = b, SCAN_TT
    else:
        nb, tt = min(SCAN_NB, b), t
    tok = pl.BlockSpec((nb, tt, d), lambda i, c: (i, c, 0))
    low = lambda x: pl.BlockSpec((nb, tt, x.shape[-1]), lambda i, c: (i, c, 0))
    const2 = lambda shape: pl.BlockSpec(shape, lambda i, c: (0, 0))
    n_heads = d // RW_HEAD
    state = pl.BlockSpec((nb, n_heads, RW_HEAD, RW_HEAD), lambda i, c: (i, 0, 0, 0))
    in_specs = [tok] * 3 + [low(lw), low(la), const2(w2.shape), const2((1, d)), const2(a2.shape),
                            const2((1, d)), const2((1, d)), const2((1, d)), const2(bd.shape),
                            const2(eye.shape)]
    args = [x.reshape(b, t, x.shape[-1]) for x in (r, k, v, lw, la)]
    args += [w2, w0.reshape(1, d), a2, a0.reshape(1, d), kk.reshape(1, d), ka.reshape(1, d), bd, eye]
    if s0 is not None:
        in_specs.append(state)
        args.append(s0)
    rows = nb * ng * RW_HEAD
    y, st = pl.pallas_call(
        functools.partial(_scan_body, has_s0=s0 is not None),
        grid=(b // nb, t // tt),
        in_specs=in_specs,
        out_specs=[tok, state],
        out_shape=[jax.ShapeDtypeStruct((b, t, d), F32),
                   jax.ShapeDtypeStruct((b, n_heads, RW_HEAD, RW_HEAD), F32)],
        scratch_shapes=[pltpu.VMEM((nb, ng, RW_HEAD, LANE_GROUP), F32)]
        + [pltpu.VMEM((nb, tt, d), F32)] * 6
        + [pltpu.VMEM((SCAN_UNROLL, nb, 3 * ng * RW_HEAD, LANE_GROUP), BF16)],
        compiler_params=_params(("parallel", "arbitrary")),
        name="rwkv_scan",
    )(*args)
    return y.reshape(n, d), st


def _rwkv_post_body(y_ref, r_ref, k_ref, v_ref, la_ref, lg_ref, a2_ref, a0_ref, g2_ref, lnw_ref,
                    lnb_ref, rk_ref, ka_ref, bd_ref, z_ref):
    bd = bd_ref[...]
    y = y_ref[...]
    inv = 1.0 / RW_HEAD
    mean = _head_sum_rows(y, bd) * inv
    yc = y - mean
    var = _head_sum_rows(yc * yc, bd) * inv
    yn = yc * lax.rsqrt(var + RW_LN_EPS) * lnw_ref[...] + lnb_ref[...]
    second = lambda x, w_ref: jnp.dot(x.astype(BF16), w_ref[...].astype(BF16), preferred_element_type=F32)
    a = jax.nn.sigmoid(second(la_ref[...], a2_ref) + a0_ref[...])
    gate = second(lg_ref[...], g2_ref)
    km = k_ref[...] * (1.0 + (a - 1.0) * ka_ref[...])
    bonus = _head_sum_rows(r_ref[...] * km * rk_ref[...], bd) * v_ref[...]
    z_ref[...] = ((yn + bonus) * gate).astype(BF16)


def _rwkv_post(y, r, k, v, la, lg, a2, a0, g2, lnw, lnb, rk, ka):
    n, d = y.shape
    bd, _ = _group_consts()
    rows = min(EW_ROWS, n)
    tok = pl.BlockSpec((rows, d), lambda i: (i, 0))
    low = lambda x: pl.BlockSpec((rows, x.shape[-1]), lambda i: (i, 0))
    vec = pl.BlockSpec((1, d), lambda i: (0, 0))
    whole = lambda x: pl.BlockSpec(x.shape, lambda i: (0, 0))
    return pl.pallas_call(
        _rwkv_post_body,
        grid=(n // rows,),
        in_specs=[tok] * 4 + [low(la), low(lg), whole(a2), vec, whole(g2)] + [vec] * 4 + [whole(bd)],
        out_specs=tok,
        out_shape=jax.ShapeDtypeStruct((n, d), BF16),
        compiler_params=_params(("parallel",)),
        name="rwkv_post",
    )(y, r, k, v, la, lg, a2, a0.reshape(1, d), g2, lnw.reshape(1, d), lnb.reshape(1, d),
      rk.reshape(1, d), ka.reshape(1, d), bd)


def _attn_unit(q, k_prev, v_prev, k_cur, v_cur, sink_ref, prev_limit, o_ref, row0):
    tq = q.shape[0]
    kvh = k_cur.shape[-1] // ATT_HEAD_DIM
    n_heads = kvh * GQA
    qi = lax.broadcasted_iota(jnp.int32, (tq, 1), 0)
    jp = lax.broadcasted_iota(jnp.int32, (1, WINDOW), 1)
    jc = lax.broadcasted_iota(jnp.int32, (1, tq), 1)
    diff_p = WINDOW + qi - jp
    diff_c = qi - jc
    valid_p = diff_p < prev_limit
    valid_c = diff_c >= 0
    diff_pf, diff_cf = diff_p.astype(F32), diff_c.astype(F32)
    ones_p = (lax.broadcasted_iota(jnp.int32, (WINDOW, ATT_HEAD_DIM), 1) == 0).astype(F32)
    ones_c = ones_p[:tq]
    nt = (((1,), (1,)), ((), ()))
    scale = ATT_HEAD_DIM ** -0.5
    for kh in range(kvh):
        heads = [kh * GQA + h for h in range(GQA)]
        qs = (jnp.concatenate([q[:, h * ATT_HEAD_DIM:(h + 1) * ATT_HEAD_DIM] for h in heads], axis=0)
              * scale).astype(BF16)
        sl = slice(kh * ATT_HEAD_DIM, (kh + 1) * ATT_HEAD_DIM)
        kp, kc = k_prev[:, sl].astype(BF16), k_cur[:, sl].astype(BF16)
        vp = jnp.concatenate([v_prev[:, sl], ones_p], axis=-1).astype(BF16)
        vc = jnp.concatenate([v_cur[:, sl], ones_c], axis=-1).astype(BF16)
        s_p_all = lax.dot_general(qs, kp, nt, preferred_element_type=F32)
        s_c_all = lax.dot_general(qs, kc, nt, preferred_element_type=F32)
        p_p, p_c, m_all = [], [], []
        for hq, h in enumerate(heads):
            slope = 2.0 ** (-8.0 * (h + 1) / n_heads)
            s_p = jnp.where(valid_p, s_p_all[hq * tq:(hq + 1) * tq] - slope * diff_pf, NEG_INF)
            s_c = jnp.where(valid_c, s_c_all[hq * tq:(hq + 1) * tq] - slope * diff_cf, NEG_INF)
            if tq == WINDOW:
                m = jnp.max(jnp.maximum(s_p, s_c), axis=-1, keepdims=True)
            else:
                m = jnp.maximum(jnp.max(s_p, axis=-1, keepdims=True),
                                jnp.max(s_c, axis=-1, keepdims=True))
            m = jnp.maximum(m, sink_ref[h])
            p_p.append(jnp.exp(s_p - m))
            p_c.append(jnp.exp(s_c - m))
            m_all.append(m)
        o_all = (jnp.dot(jnp.concatenate(p_p, axis=0).astype(BF16), vp, preferred_element_type=F32)
                 + jnp.dot(jnp.concatenate(p_c, axis=0).astype(BF16), vc, preferred_element_type=F32))
        for hq, h in enumerate(heads):
            o_h = o_all[hq * tq:(hq + 1) * tq]
            den = o_h[:, ATT_HEAD_DIM:ATT_HEAD_DIM + 1] + jnp.exp(sink_ref[h] - m_all[hq])
            o_ref[pl.ds(row0, tq), pl.ds(h * ATT_HEAD_DIM, ATT_HEAD_DIM)] = (
                o_h[:, :ATT_HEAD_DIM] / den).astype(o_ref.dtype)


def _attn_prefill_body(q_ref, kvc_ref, kvp_ref, sink_ref, o_ref):
    kvd = kvc_ref.shape[-1] // 2
    kvc, kvp = kvc_ref[...], kvp_ref[...]
    prev_limit = jnp.where(pl.program_id(1) > 0, WINDOW, 0)
    _attn_unit(q_ref[...], kvp[:, :kvd], kvp[:, kvd:], kvc[:, :kvd], kvc[:, kvd:], sink_ref,
               prev_limit, o_ref, 0)


def _attn_decode_body(q_ref, kvc_ref, kc_ref, vc_ref, sink_ref, o_ref, *, nb, tq):
    kvd = kvc_ref.shape[-1] // 2
    kvh = kvd // ATT_HEAD_DIM
    n_heads = kvh * GQA
    rows = n_heads * tq
    head = lax.broadcasted_iota(jnp.int32, (rows, 1), 0) // tq
    qi = lax.broadcasted_iota(jnp.int32, (rows, 1), 0) % tq
    slope = jnp.exp2(-8.0 * (head.astype(F32) + 1.0) / n_heads)
    sink = sink_ref[...]

    def key_axis(n_keys):
        c = lax.broadcasted_iota(jnp.int32, (1, kvh * n_keys), 1)
        return c // n_keys, c % n_keys

    kh_p, jp = key_axis(WINDOW)
    kh_c, jc = key_axis(tq)
    own_p = kh_p == head // GQA
    own_c = kh_c == head // GQA
    diff_p = WINDOW + qi - jp
    diff_c = qi - jc
    valid_p = jnp.where(own_p, diff_p, WINDOW) < WINDOW
    valid_c = jnp.where(own_c, diff_c, -1) >= 0
    bias_p = jnp.where(valid_p, -slope * diff_p.astype(F32), NEG_INF)
    bias_c = jnp.where(valid_c, -slope * diff_c.astype(F32), NEG_INF)
    ones = (lax.broadcasted_iota(jnp.int32, (WINDOW, ATT_HEAD_DIM), 1) == 0).astype(F32)
    nt = (((1,), (1,)), ((), ()))
    scale = ATT_HEAD_DIM ** -0.5
    head_cols = lambda x, h: x[:, h * ATT_HEAD_DIM:(h + 1) * ATT_HEAD_DIM]

    def stack_keys(x):
        return jnp.concatenate([head_cols(x, kh) for kh in range(kvh)], axis=0).astype(BF16)

    def stack_values(x):
        one = ones[:x.shape[0]]
        return jnp.concatenate(
            [jnp.concatenate([head_cols(x, kh), one], axis=-1) for kh in range(kvh)], axis=0).astype(BF16)

    for n in range(nb):
        q = q_ref[pl.ds(n * tq, tq), :]
        kvc = kvc_ref[pl.ds(n * tq, tq), :]
        qs = (jnp.concatenate([head_cols(q, h) for h in range(n_heads)], axis=0) * scale).astype(BF16)
        s_p = lax.dot_general(qs, stack_keys(kc_ref[n]), nt, preferred_element_type=F32) + bias_p
        s_c = lax.dot_general(qs, stack_keys(kvc[:, :kvd]), nt, preferred_element_type=F32) + bias_c
        m = jnp.maximum(jnp.maximum(jnp.max(s_p, axis=-1, keepdims=True),
                                    jnp.max(s_c, axis=-1, keepdims=True)), sink)
        o_all = (jnp.dot(jnp.exp(s_p - m).astype(BF16), stack_values(vc_ref[n]),
                         preferred_element_type=F32)
                 + jnp.dot(jnp.exp(s_c - m).astype(BF16), stack_values(kvc[:, kvd:]),
                           preferred_element_type=F32))
        den = o_all[:, ATT_HEAD_DIM:ATT_HEAD_DIM + 1] + jnp.exp(sink - m)
        o = o_all[:, :ATT_HEAD_DIM] / den
        for h in range(n_heads):
            o_ref[pl.ds(n * tq, tq), pl.ds(h * ATT_HEAD_DIM, ATT_HEAD_DIM)] = o[h * tq:(h + 1) * tq]


def _attn_prefill(q, kv, sinks, b, t):
    n, dq = q.shape
    kv2 = kv.shape[-1]
    nblk = t // WINDOW
    return pl.pallas_call(
        _attn_prefill_body,
        grid=(b, nblk),
        in_specs=[
            pl.BlockSpec((WINDOW, dq), lambda i, j: (i * nblk + j, 0)),
            pl.BlockSpec((WINDOW, kv2), lambda i, j: (i * nblk + j, 0)),
            pl.BlockSpec((WINDOW, kv2), lambda i, j: (i * nblk + jnp.maximum(j - 1, 0), 0)),
            pl.BlockSpec(memory_space=pltpu.SMEM),
        ],
        out_specs=pl.BlockSpec((WINDOW, dq), lambda i, j: (i * nblk + j, 0)),
        out_shape=jax.ShapeDtypeStruct((n, dq), BF16),
        compiler_params=_params(("parallel", "arbitrary")),
        name="attn_prefill",
    )(q, kv, kv, sinks.astype(F32))


def _attn_decode(q, kv, cache_k, cache_v, sinks, b, t):
    n, dq = q.shape
    kv2 = kv.shape[-1]
    kvd = kv2 // 2
    nb = min(ATT_NB, b)
    rows = nb * t
    return pl.pallas_call(
        functools.partial(_attn_decode_body, nb=nb, tq=t),
        grid=(b // nb,),
        in_specs=[
            pl.BlockSpec((rows, dq), lambda i: (i, 0)),
            pl.BlockSpec((rows, kv2), lambda i: (i, 0)),
            pl.BlockSpec((nb, WINDOW, kvd), lambda i: (i, 0, 0)),
            pl.BlockSpec((nb, WINDOW, kvd), lambda i: (i, 0, 0)),
            pl.BlockSpec((sinks.shape[0] * t, 1), lambda i: (0, 0)),
        ],
        out_specs=pl.BlockSpec((rows, dq), lambda i: (i, 0)),
        out_shape=jax.ShapeDtypeStruct((n, dq), F32),
        compiler_params=_params(("parallel",)),
        name="attn_decode",
    )(q, kv, cache_k.reshape(b, WINDOW, kvd), cache_v.reshape(b, WINDOW, kvd),
      jnp.repeat(sinks.astype(F32), t)[:, None])


def _moe_body(te_ref, tv_ref, tok_ref, dst_ref, h_hbm, gate_ref, wg_ref, wu_ref, wd_ref, out_hbm,
              xbuf, obuf, gsem, ssem, wg16, wu16, wd16):
    t = pl.program_id(0)
    last = pl.num_programs(0) - 1
    slot = t % 2

    def split(row):
        return lax.shift_right_logical(row, 3), row & (SUBLANES - 1)

    def gather_of(buf):
        def copy(i, u, tok):
            hi, lo = split(tok)
            return pltpu.make_async_copy(h_hbm.at[hi, pl.ds(lo, 1)], xbuf.at[buf, i, pl.ds(u, 1)],
                                         gsem.at[buf])
        return tok_ref, copy

    def scatter_copy(i, u, dst):
        hi, lo = split(dst)
        return pltpu.make_async_copy(obuf.at[i, pl.ds(u, 1)], out_hbm.at[hi, pl.ds(lo, 1)], ssem.at[0])

    scatter = (dst_ref, scatter_copy)

    def for_rows(tile, rows, act):
        table, build = rows
        groups = lax.shift_right_logical(tv_ref[tile] + (SUBLANES - 1), 3)

        def body(i, carry):
            base = tile * MOE_TG + i * SUBLANES
            entries = [table[base + u] for u in range(SUBLANES)]
            for u in range(SUBLANES):
                act(build(i, u, entries[u]), u)
            return carry
        lax.fori_loop(0, groups, body, 0)

    start = lambda copy, u: copy.start(priority=u % 2)
    wait = lambda copy, u: copy.wait()

    groups_per_tile = MOE_TG // SUBLANES

    @pl.when(t == 0)
    def _():
        xbuf[...] = jnp.zeros_like(xbuf)
        obuf[...] = jnp.zeros_like(obuf)
        spare = pltpu.make_async_copy(
            obuf, out_hbm.at[pl.ds(out_hbm.shape[0] - groups_per_tile, groups_per_tile)], ssem.at[0])
        spare.start()
        spare.wait()
        for_rows(0, gather_of(0), start)

    nxt = jnp.minimum(t + 1, last)

    @pl.when(t < last)
    def _():
        for_rows(nxt, gather_of(1 - slot), start)

    prev = jnp.maximum(t - 1, 0)

    @pl.when((t == 0) | (te_ref[t] != te_ref[prev]))
    def _():
        wg16[...] = wg_ref[...].astype(BF16)
        wu16[...] = wu_ref[...].astype(BF16)
        wd16[...] = wd_ref[...].astype(BF16)

    @pl.when(tv_ref[t] > 0)
    def _():
        for_rows(t, gather_of(slot), wait)
        x = xbuf[slot].reshape(MOE_TG, -1).astype(BF16)
        hg = jnp.dot(x, wg16[...], preferred_element_type=F32)
        hu = jnp.dot(x, wu16[...], preferred_element_type=F32)
        act = (hg * jax.nn.sigmoid(hg)) * hu * gate_ref[...]
        o = jnp.dot(act.astype(BF16), wd16[...], preferred_element_type=F32)

        @pl.when(t > 0)
        def _():
            for_rows(prev, scatter, wait)

        obuf[...] = o.reshape(obuf.shape)
        for_rows(t, scatter, start)

        @pl.when(t == last)
        def _():
            for_rows(t, scatter, wait)

    @pl.when((tv_ref[t] == 0) & (t > 0))
    def _():
        for_rows(prev, scatter, wait)


def _moe_plan(route, n_tiles):
    n = route.shape[0]
    e = route[:, :2].astype(jnp.int32).reshape(-1)
    w = route[:, 2:4].reshape(-1)
    onehot = (e[:, None] == jnp.arange(N_EXPERTS, dtype=jnp.int32)[None, :]).astype(jnp.int32)
    csum = jnp.cumsum(onehot, axis=0)
    rank = jnp.take_along_axis(csum, e[:, None], axis=1)[:, 0] - 1
    counts = csum[-1]
    padded = ((counts + MOE_TG - 1) // MOE_TG) * MOE_TG
    pend = jnp.cumsum(padded)
    dest = (pend - padded)[e] + rank
    p_rows = n_tiles * MOE_TG
    vals = jnp.stack([jnp.arange(1, 2 * n + 1, dtype=jnp.int32), lax.bitcast_convert_type(w, jnp.int32)],
                     axis=1)
    placed = jnp.zeros((p_rows, 2), jnp.int32).at[dest].set(vals, unique_indices=True,
                                                             mode="promise_in_bounds")
    has_pair = placed[:, 0] > 0
    pair = placed[:, 0] - 1
    row_token = jnp.where(has_pair, pair // 2, 0)
    spare = 2 * n + jnp.arange(p_rows, dtype=jnp.int32) % MOE_TG
    row_dst = jnp.where(has_pair, (pair % 2) * n + pair // 2, spare)
    row_gate = jnp.where(has_pair, lax.bitcast_convert_type(placed[:, 1], F32), 0.0)
    starts = jnp.arange(n_tiles, dtype=jnp.int32) * MOE_TG
    tile_expert = jnp.minimum(jnp.searchsorted(pend, starts, side="right"), N_EXPERTS - 1)
    tile_rows = jnp.clip(counts[tile_expert] - (starts - (pend - padded)[tile_expert]), 0, MOE_TG)
    tile_rows = jnp.where(starts < pend[-1], tile_rows, 0).astype(jnp.int32)
    last_used = jnp.max(jnp.where(tile_rows > 0, tile_expert, 0))
    tile_expert = jnp.where(tile_rows > 0, tile_expert, last_used).astype(jnp.int32)
    return row_token, row_dst, row_gate, tile_expert, tile_rows


def _moe(h, route, w_gate, w_up, w_down, layer):
    n, d = h.shape
    group = (SUBLANES, d)
    f = w_gate.shape[-1]
    n_tiles = (2 * n) // MOE_TG + N_EXPERTS
    n_out = 2 * n + MOE_TG
    assert n % SUBLANES == 0
    row_token, row_dst, row_gate, tile_expert, tile_rows = _moe_plan(route, n_tiles)
    out = pl.pallas_call(
        _moe_body,
        grid_spec=pltpu.PrefetchScalarGridSpec(
            num_scalar_prefetch=4,
            grid=(n_tiles,),
            in_specs=[
                pl.BlockSpec(memory_space=pl.ANY),
                pl.BlockSpec((MOE_TG, 1), lambda t, te, tv, rt, rd: (t, 0)),
                pl.BlockSpec((None, None, d, f), lambda t, te, tv, rt, rd: (layer, te[t], 0, 0)),
                pl.BlockSpec((None, None, d, f), lambda t, te, tv, rt, rd: (layer, te[t], 0, 0)),
                pl.BlockSpec((None, None, f, d), lambda t, te, tv, rt, rd: (layer, te[t], 0, 0)),
            ],
            out_specs=pl.BlockSpec(memory_space=pl.ANY),
            scratch_shapes=[pltpu.VMEM((2, MOE_TG // SUBLANES) + group, F32),
                            pltpu.VMEM((MOE_TG // SUBLANES,) + group, F32),
                            pltpu.SemaphoreType.DMA((2,)), pltpu.SemaphoreType.DMA((1,)),
                            pltpu.VMEM((d, f), BF16), pltpu.VMEM((d, f), BF16),
                            pltpu.VMEM((f, d), BF16)],
        ),
        out_shape=jax.ShapeDtypeStruct((n_out // SUBLANES,) + group, F32),
        compiler_params=_params(("arbitrary",)),
        name="moe_ffn",
    )(tile_expert, tile_rows, row_token, row_dst, h.reshape((n // SUBLANES,) + group),
      row_gate.reshape(-1, 1), w_gate, w_up, w_down)
    return out.reshape(n_out, d)


def _split_mod(mod, n_parts, trunk_rows):
    mod = mod[:, None, :]
    return [[(mod, k, row0) for k in range(n_parts)] for row0 in trunk_rows]


def kernel(x_prompt, x_sample, c_prompt, c_sample, state_wkv, state_shift, cache_k_win, cache_v_win, ada_w, ada_b, ln_g, ln_b, rw_mu, rw_w_rkv, rw_w0, rw_w1, rw_w2, rw_a0, rw_a1, rw_a2, rw_g1, rw_g2, rw_k_k, rw_k_a, rw_r_k, rw_lnx_w, rw_lnx_b, rw_wo, ada_kv_w, ada_kv_b, w_kv, w_q, attn_sinks, w_o_attn, router_w, router_bias, moe_w_gate, moe_w_up, moe_w_down):
    d = x_prompt.shape[-1]
    xs = [x_prompt, x_sample]
    shapes = [x.shape[:2] for x in xs]
    counts = [b * t for b, t in shapes]
    bp = shapes[0][0]

    c_all = jnp.concatenate([c_sample, c_prompt], axis=0)
    trunk_rows = [shapes[1][0], 0]
    mods = {(l, s): _split_mod(_mm(c_all, ada_w, name="mm_ada", w_lead=(l, s), bias=ada_b[l, s],
                                   in_act="silu"), 3, trunk_rows)
            for l in range(DEPTH) for s in range(2)}
    mods_kv = _split_mod(_mm(c_all, ada_kv_w, name="mm_ada_kv", bias=ada_kv_b, in_act="silu"), 2,
                         trunk_rows)

    by_slot = lambda x: x.reshape(-1, N_GROUPS, EXPERTS_PER_GROUP).swapaxes(1, 2).reshape(-1, N_EXPERTS)
    router = (by_slot(router_w), by_slot(router_bias))

    n_all = counts[0] + counts[1]
    row0 = [0, counts[0]]

    def moe_layer(h_all, routes, layer):
        out = _moe(h_all, jnp.concatenate(routes, axis=0), moe_w_gate, moe_w_up, moe_w_down, layer)
        return [[(out, slot * n_all + row0[tr]) for slot in range(2)] for tr in range(2)]

    firsts = [jnp.zeros((bp, 1, d), F32), state_shift[0][:, None, :]]
    s0s = [None, state_wkv[0]]
    x1, h1, route1, states, hlasts = [], None, [], [], []
    for tr in range(2):
        b, t = shapes[tr]
        shift, scale, gate = mods[0, 0][tr]
        xmix, hlast, lw, la, lg = _premix(xs[tr], shift, scale, firsts[tr], rw_mu[0], rw_w1[0],
                                          rw_a1[0], rw_g1[0])
        r = _mm(xmix, rw_w_rkv, name="mm_r", x_lead=(0,), w_lead=(0, 0), tm=MM_TM_BF16)
        k = _mm(xmix, rw_w_rkv, name="mm_k", x_lead=(1,), w_lead=(0, 1), tm=MM_TM_BF16)
        v = _mm(xmix, rw_w_rkv, name="mm_v", x_lead=(2,), w_lead=(0, 2), tm=MM_TM_BF16)
        y, st = _rwkv_scan(r, k, v, lw, la, rw_w2[0], rw_w0[0], rw_a2[0], rw_a0[0], rw_k_k[0],
                           rw_k_a[0], s0s[tr], b, t)
        z = _rwkv_post(y, r, k, v, la, lg, rw_a2[0], rw_a0[0], rw_g2[0], rw_lnx_w[0], rw_lnx_b[0],
                       rw_r_k[0], rw_k_a[0])
        out = _mm(z, rw_wo, name="mm_wo", w_lead=(0,), tm=MM_TM_BF16)
        sh, sc, _ = mods[0, 1][tr]
        xn, h1, route = _ln_mod(xs[tr], [(out, 0)], gate, ln_g[0, 0], ln_b[0, 0], [(sh, sc)], router,
                                (h1, n_all, row0[tr]))
        x1.append(xn), route1.append(route), states.append(st), hlasts.append(hlast)
    moe1 = moe_layer(h1, route1, 0)

    caches = [None, (cache_k_win, cache_v_win)]
    x3, h3, route3, kvs = [], None, [], []
    for tr in range(2):
        b, t = shapes[tr]
        shq, scq, gate_q = mods[1, 0][tr]
        shk, sck = mods_kv[tr]
        x2, hq, hkv = _ln_mod(x1[tr], moe1[tr], mods[0, 1][tr][2], ln_g[0, 1], ln_b[0, 1],
                              [(shq, scq), (shk, sck)])
        q = _mm(hq, w_q, name="mm_q", w_lead=(0,), tm=MM_TM_BF16)
        kv = _mm(hkv, w_kv, name="mm_kv", tm=MM_TM_BF16)
        if caches[tr] is None:
            o = _attn_prefill(q, kv, attn_sinks[0], b, t)
        else:
            o = _attn_decode(q, kv, caches[tr][0], caches[tr][1], attn_sinks[0], b, t)
        out = _mm(o, w_o_attn, name="mm_o", w_lead=(0,), tm=MM_TM_BF16 if o.dtype == BF16 else MM_TM)
        sh, sc, _ = mods[1, 1][tr]
        xn, h3, route = _ln_mod(x2, [(out, 0)], gate_q, ln_g[1, 0], ln_b[1, 0], [(sh, sc)], router,
                                (h3, n_all, row0[tr]))
        x3.append(xn), route3.append(route), kvs.append(kv)
    moe3 = moe_layer(h3, route3, 1)
    ys = [_ln_mod(x3[tr], moe3[tr], mods[1, 1][tr][2], ln_g[1, 1], ln_b[1, 1], [])[0]
          for tr in range(2)]

    kvd = kvs[0].shape[-1] // 2
    kvh = kvd // ATT_HEAD_DIM
    (bp, tp), (bs, ts) = shapes
    kv_p = kvs[0].reshape(bp, tp, 2 * kvd)[:, tp - WINDOW:]
    p_k_win = kv_p[..., :kvd].reshape(bp, WINDOW, kvh, ATT_HEAD_DIM)
    p_v_win = kv_p[..., kvd:].reshape(bp, WINDOW, kvh, ATT_HEAD_DIM)
    kv_s = kvs[1].reshape(bs, ts, 2 * kvd)
    s_k_win = jnp.concatenate(
        [cache_k_win, kv_s[..., :kvd].reshape(bs, ts, kvh, ATT_HEAD_DIM)], axis=1)[:, -WINDOW:]
    s_v_win = jnp.concatenate(
        [cache_v_win, kv_s[..., kvd:].reshape(bs, ts, kvh, ATT_HEAD_DIM)], axis=1)[:, -WINDOW:]
    p_wkv = states[0][None]
    s_wkv = states[1][None]
    p_shift = hlasts[0].reshape(1, bp, d)
    s_shift = hlasts[1].reshape(1, bs, d)
    return (ys[0], ys[1], p_wkv, p_shift, p_k_win, p_v_win, s_wkv, s_shift, s_k_win, s_v_win)
```
